```python
import math
import jax, jax.numpy as jnp
from jax import lax
import numpy as np

D_MODEL = 2048
BATCH = 2
SEQ = 8192
DEPTH = 2

GRID_W = 64
CTX_LEN = 256
N_MOD = 6
EPS = 1e-6
ROPE_BASE = 10000.0

RET_HEADS = 4
RET_DIM = 128
RET_W = RET_HEADS * RET_DIM
RET_DECAY_BASE_EXP = 5
CHUNK = 128

DIFF_HEADS = 8
DIFF_QK = 64
DIFF_V = 2 * DIFF_QK
DIFF_QK_W = DIFF_HEADS * 2 * DIFF_QK
DIFF_W = DIFF_HEADS * DIFF_V
Q_BLOCK = 128

S5_GROUP = 16
S5_CH = 512
S5_GROUPS = S5_CH // S5_GROUP
S5_STATE = 64
DT_MIN = 0.001
DT_MAX = 0.1

MIX_W = RET_W + DIFF_W + S5_CH
SPLIT_POINTS = (RET_W, 2 * RET_W, 3 * RET_W, 4 * RET_W, 4 * RET_W + DIFF_QK_W, 4 * RET_W + 2 * DIFF_QK_W, 4 * RET_W + 2 * DIFF_QK_W + DIFF_W)
IN_W = 4 * RET_W + 2 * DIFF_QK_W + DIFF_W + S5_CH

N_EXPERTS = 64
EXPERT_FF = 512
SHARED_FF = 512
TOP_K = 8
N_EXPERT_GROUPS = 8
TOPK_GROUPS = 4
ROUTE_SCALE = 2.5
TOKEN_BLOCK = 128

F32 = jnp.float32

kernel_name = 'hybrid_retention_diffattn_s5_moe_dit'


def rms_norm(x, w):
    x = x.astype(F32)
    return x * lax.rsqrt(jnp.mean(x * x, axis=-1, keepdims=True) + EPS) * w.astype(F32)


def split_heads(t, n_heads, head_dim):
    b, l, _ = t.shape
    return t.astype(F32).reshape(b, l, n_heads, head_dim).transpose(0, 2, 1, 3)


def merge_heads(t):
    b, h, l, d = t.shape
    return t.transpose(0, 2, 1, 3).reshape(b, l, h * d)


def axial_rope_tables(rows, head_dim):
    axis_dim = head_dim // 2
    inv_freq = ROPE_BASE ** (-jnp.arange(0, axis_dim, 2, dtype=F32) / axis_dim)
    row = jnp.repeat(jnp.arange(rows, dtype=F32), GRID_W)
    col = jnp.tile(jnp.arange(GRID_W, dtype=F32), rows)
    ang = jnp.concatenate([row[:, None] * inv_freq, col[:, None] * inv_freq], axis=-1)
    return jnp.cos(ang), jnp.sin(ang)


def apply_rope(x, rope):
    cos, sin = rope
    x1, x2 = jnp.split(x, 2, axis=-1)
    return jnp.concatenate([x1 * cos - x2 * sin, x1 * sin + x2 * cos], axis=-1)


def adaln(cond, w_ada, b_ada):
    return jax.nn.silu(cond.astype(F32)) @ w_ada.astype(F32) + b_ada.astype(F32)


def split_proj(proj):
    return jnp.split(proj, SPLIT_POINTS, axis=-1)


def retention_scan(q, k, v, log_gamma, s0):
    b, h, n, d = q.shape
    nc = n // CHUNK

    def to_chunks(t):
        return t.reshape(b, h, nc, CHUNK, d).transpose(2, 0, 1, 3, 4)

    idx = jnp.arange(CHUNK, dtype=F32)
    rel = idx[:, None] - idx[None, :]
    decay_mask = jnp.where(rel >= 0, jnp.exp(log_gamma[:, None, None] * jnp.maximum(rel, 0.0)), 0.0)
    q_decay = jnp.exp(log_gamma[:, None] * (idx + 1.0))[None, :, :, None]
    k_decay = jnp.exp(log_gamma[:, None] * (CHUNK - 1.0 - idx))[None, :, :, None]
    chunk_decay = jnp.exp(log_gamma * CHUNK)[None, :, None, None]

    def step(s, qkv):
        qb, kb, vb = qkv
        scores = jnp.einsum('bhid,bhjd->bhij', qb, kb) * decay_mask
        inner = jnp.einsum('bhij,bhjd->bhid', scores, vb)
        cross = jnp.einsum('bhid,bhde->bhie', qb, s) * q_decay
        s_new = s * chunk_decay + jnp.einsum('bhjd,bhje->bhde', kb * k_decay, vb)
        return s_new, inner + cross

    s_final, out = lax.scan(step, s0, (to_chunks(q), to_chunks(k), to_chunks(v)))
    return out.transpose(1, 2, 0, 3, 4).reshape(b, h, n, d), s_final


def retention_out(o, g, norm_w):
    mu = jnp.mean(o, axis=-1, keepdims=True)
    var = jnp.mean(jnp.square(o - mu), axis=-1, keepdims=True)
    o = merge_heads((o - mu) * lax.rsqrt(var + EPS)) * norm_w.astype(F32)
    return jax.nn.silu(g.astype(F32)) * o


def retention_mixer(p_lat, p_ctx, decay_logit, norm_w, rope, need_ctx):
    rq, rk, rv, rg = p_lat
    cq, ck, cv, cg = p_ctx
    scale = RET_DIM ** -0.5
    ql = apply_rope(split_heads(rq, RET_HEADS, RET_DIM), rope)
    kl = apply_rope(split_heads(rk, RET_HEADS, RET_DIM), rope) * scale
    vl = split_heads(rv, RET_HEADS, RET_DIM)
    qc = split_heads(cq, RET_HEADS, RET_DIM)
    kc = split_heads(ck, RET_HEADS, RET_DIM) * scale
    vc = split_heads(cv, RET_HEADS, RET_DIM)
    log_g = jax.nn.log_sigmoid(decay_logit.astype(F32))
    zero = jnp.zeros((qc.shape[0], RET_HEADS, RET_DIM, RET_DIM), F32)
    flip = lambda t: t[:, :, ::-1]
    oc_f, sc_f = retention_scan(qc, kc, vc, log_g[0], zero)
    oc_b, sc_b = retention_scan(flip(qc), flip(kc), flip(vc), log_g[1], zero)
    ol_f, _ = retention_scan(ql, kl, vl, log_g[0], sc_f)
    ol_b, _ = retention_scan(flip(ql), flip(kl), flip(vl), log_g[1], sc_b)
    out_lat = retention_out(ol_f + flip(ol_b), rg, norm_w)
    out_ctx = None
    if need_ctx:
        out_ctx = retention_out(oc_f + flip(oc_b), cg, norm_w)
    return out_lat, out_ctx


def diff_heads(t, qk_norm_w):
    b, l, _ = t.shape
    t = t.astype(F32).reshape(b, l, DIFF_HEADS, 2, DIFF_QK).transpose(0, 2, 3, 1, 4)
    return rms_norm(t, qk_norm_w)


def diff_attend(q, k, v, lam):
    s = jnp.einsum('bhcqd,bhckd->bhcqk', q, k) * (DIFF_QK ** -0.5)
    a = jax.nn.softmax(s, axis=-1)
    w = a[:, :, 0] - lam * a[:, :, 1]
    return jnp.einsum('bhqk,bhkd->bhqd', w, v)


def diff_attention_mixer(p_lat, p_ctx, qnorm_w, knorm_w, lam_vecs, subln_w, lambda_init, rope, need_ctx):
    dq, dk, dv = p_lat
    cq, ck, cv = p_ctx
    ql = apply_rope(diff_heads(dq, qnorm_w), rope)
    kl = apply_rope(diff_heads(dk, knorm_w), rope)
    vl = split_heads(dv, DIFF_HEADS, DIFF_V)
    kc = diff_heads(ck, knorm_w)
    vc = split_heads(cv, DIFF_HEADS, DIFF_V)
    lv = lam_vecs.astype(F32)
    lam = jnp.exp(jnp.sum(lv[0] * lv[1])) - jnp.exp(jnp.sum(lv[2] * lv[3])) + lambda_init
    k_all = jnp.concatenate([kc, kl], axis=3)
    v_all = jnp.concatenate([vc, vl], axis=2)
    b, h, _, n, _ = ql.shape
    nb = n // Q_BLOCK
    q_blocks = ql.reshape(b, h, 2, nb, Q_BLOCK, DIFF_QK).transpose(3, 0, 1, 2, 4, 5)
    o_blocks = lax.map(lambda qb: diff_attend(qb, k_all, v_all, lam), q_blocks)
    o_lat = o_blocks.transpose(1, 2, 0, 3, 4).reshape(b, h, n, DIFF_V)
    out_lat = merge_heads(rms_norm(o_lat, subln_w) * (1.0 - lambda_init))
    out_ctx = None
    if need_ctx:
        qc = diff_heads(cq, qnorm_w)
        out_ctx = merge_heads(rms_norm(diff_attend(qc, kc, vc, lam), subln_w) * (1.0 - lambda_init))
    return out_lat, out_ctx


def s5_discretize(lam_re, lam_im, log_dt, b_re, b_im):
    lam_re, lam_im = lam_re.astype(F32), lam_im.astype(F32)
    b_re, b_im = b_re.astype(F32), b_im.astype(F32)
    dt = jnp.exp(log_dt.astype(F32))[:, None]
    mag = jnp.exp(lam_re * dt)
    a_re = mag * jnp.cos(lam_im * dt)
    a_im = mag * jnp.sin(lam_im * dt)
    den = lam_re * lam_re + lam_im * lam_im
    nr = a_re - 1.0
    f_re = (nr * lam_re + a_im * lam_im) / den
    f_im = (a_im * lam_re - nr * lam_im) / den
    bb_re = f_re[..., None] * b_re - f_im[..., None] * b_im
    bb_im = f_re[..., None] * b_im + f_im[..., None] * b_re
    return a_re, a_im, bb_re, bb_im


def s5_scan(u, disc, x0_re, x0_im):
    a_re, a_im, bb_re, bb_im = disc
    bu_re = jnp.einsum('blgh,gph->blgp', u, bb_re)
    bu_im = jnp.einsum('blgh,gph->blgp', u, bb_im)
    bu_re = bu_re.at[:, 0].add(a_re * x0_re - a_im * x0_im)
    bu_im = bu_im.at[:, 0].add(a_re * x0_im + a_im * x0_re)
    ar = jnp.broadcast_to(a_re, bu_re.shape)
    ai = jnp.broadcast_to(a_im, bu_re.shape)

    def combine(e1, e2):
        a1r, a1i, b1r, b1i = e1
        a2r, a2i, b2r, b2i = e2
        return (a2r * a1r - a2i * a1i, a2r * a1i + a2i * a1r,
                a2r * b1r - a2i * b1i + b2r, a2r * b1i + a2i * b1r + b2i)

    _, _, x_re, x_im = lax.associative_scan(combine, (ar, ai, bu_re, bu_im), axis=1)
    return x_re, x_im


def s5_readout(x_re, x_im, u, c_re, c_im, d, w_glu):
    b, l = u.shape[:2]
    y = jnp.einsum('blgp,ghp->blgh', x_re, c_re.astype(F32)) - jnp.einsum('blgp,ghp->blgh', x_im, c_im.astype(F32))
    y = jax.nn.gelu((y + d.astype(F32).reshape(S5_GROUPS, S5_GROUP) * u).reshape(b, l, S5_CH))
    return y * jax.nn.sigmoid(y @ w_glu.astype(F32))


def s5_mixer(u_lat, u_ctx, lam_re, lam_im, log_dt, b_re, b_im, c_re, c_im, d, w_glu, need_ctx):
    b, n, _ = u_lat.shape
    ul = u_lat.astype(F32).reshape(b, n, S5_GROUPS, S5_GROUP)
    uc = u_ctx.astype(F32).reshape(b, u_ctx.shape[1], S5_GROUPS, S5_GROUP)
    fwd = s5_discretize(lam_re[0], lam_im[0], log_dt[0], b_re, b_im)
    bwd = s5_discretize(lam_re[1], lam_im[1], log_dt[1], b_re, b_im)
    zero = jnp.zeros((b, S5_GROUPS, S5_STATE), F32)
    cf_re, cf_im = s5_scan(uc, fwd, zero, zero)
    cb_re, cb_im = s5_scan(uc[:, ::-1], bwd, zero, zero)
    lf_re, lf_im = s5_scan(ul, fwd, cf_re[:, -1], cf_im[:, -1])
    lb_re, lb_im = s5_scan(ul[:, ::-1], bwd, cb_re[:, -1], cb_im[:, -1])
    out_lat = s5_readout(lf_re + lb_re[:, ::-1], lf_im + lb_im[:, ::-1], ul, c_re, c_im, d, w_glu)
    out_ctx = None
    if need_ctx:
        out_ctx = s5_readout(cf_re + cb_re[:, ::-1], cf_im + cb_im[:, ::-1], uc, c_re, c_im, d, w_glu)
    return out_lat, out_ctx


def moe_ffn(h, router_w, router_bias, w_gate, w_up, w_down, s_gate, s_up, s_down):
    b, l, d = h.shape
    t = h.reshape(b * l, d)
    n_tok = t.shape[0]
    scores = jax.nn.sigmoid(t @ router_w.astype(F32))
    sel = scores + router_bias.astype(F32)
    per_group = N_EXPERTS // N_EXPERT_GROUPS
    grp_score = lax.top_k(sel.reshape(n_tok, N_EXPERT_GROUPS, per_group), 2)[0].sum(-1)
    _, top_grp = lax.top_k(grp_score, TOPK_GROUPS)
    grp_mask = jnp.any(top_grp[:, :, None] == jnp.arange(N_EXPERT_GROUPS)[None, None, :], axis=1)
    exp_mask = jnp.repeat(grp_mask, per_group, axis=1)
    _, top_e = lax.top_k(jnp.where(exp_mask, sel, -jnp.inf), TOP_K)
    w = jnp.take_along_axis(scores, top_e, axis=1)
    w = ROUTE_SCALE * w / jnp.sum(w, axis=1, keepdims=True)
    gates = jnp.sum(jax.nn.one_hot(top_e, N_EXPERTS, dtype=F32) * w[:, :, None], axis=1)

    def block(args):
        tb, gb = args
        hid = jax.nn.silu(jnp.einsum('td,edf->tef', tb, w_gate)) * jnp.einsum('td,edf->tef', tb, w_up)
        return jnp.einsum('tef,efd->td', hid * gb[:, :, None], w_down)

    nb = n_tok // TOKEN_BLOCK
    routed = lax.map(block, (t.reshape(nb, TOKEN_BLOCK, d), gates.reshape(nb, TOKEN_BLOCK, N_EXPERTS)))
    shared = (jax.nn.silu(t @ s_gate) * (t @ s_up)) @ s_down
    return (routed.reshape(n_tok, d) + shared).reshape(b, l, d)


def setup_inputs(seed: int = 0) -> dict:
    key = jax.random.key(seed)
    ks = jax.random.split(key, 34)
    nrm = lambda k, shape, s: s * jax.random.normal(k, shape, F32)
    L, D, G, P = DEPTH, D_MODEL, S5_GROUPS, S5_STATE
    ret_logit0 = jnp.asarray(np.log(2.0 ** (RET_DECAY_BASE_EXP + np.arange(RET_HEADS)) - 1.0), F32)
    return {
        'x': nrm(ks[0], (BATCH, SEQ, D), 1.0),
        'c': nrm(ks[1], (BATCH, D), 1.0),
        'ctx': nrm(ks[2], (BATCH, CTX_LEN, D), 1.0),
        'c_ctx': nrm(ks[3], (D,), 1.0),
        'w_ada': nrm(ks[4], (L, D, N_MOD * D), 0.5 * D ** -0.5),
        'b_ada': nrm(ks[5], (L, N_MOD * D), 0.01),
        'norm1_w': 1.0 + nrm(ks[6], (L, D), 0.01),
        'norm2_w': 1.0 + nrm(ks[7], (L, D), 0.01),
        'w_in': nrm(ks[8], (L, D, IN_W), D ** -0.5),
        'ret_decay_logit': ret_logit0 + nrm(ks[9], (L, 2, RET_HEADS), 0.01),
        'ret_norm_w': 1.0 + nrm(ks[10], (L, RET_W), 0.01),
        'diff_qnorm_w': 1.0 + nrm(ks[11], (L, DIFF_QK), 0.01),
        'diff_knorm_w': 1.0 + nrm(ks[12], (L, DIFF_QK), 0.01),
        'diff_lambda': nrm(ks[13], (L, 4, DIFF_QK), 0.1),
        'diff_subln_w': 1.0 + nrm(ks[14], (L, DIFF_V), 0.01),
        's5_lambda_re': -0.5 + nrm(ks[15], (L, 2, G, P), 0.01),
        's5_lambda_im': math.pi * jnp.arange(P, dtype=F32) + nrm(ks[16], (L, 2, G, P), 0.01),
        's5_log_dt': jax.random.uniform(ks[17], (L, 2, G), F32, math.log(DT_MIN), math.log(DT_MAX)),
        's5_b_re': nrm(ks[18], (L, G, P, S5_GROUP), (2.0 * S5_GROUP) ** -0.5),
        's5_b_im': nrm(ks[19], (L, G, P, S5_GROUP), (2.0 * S5_GROUP) ** -0.5),
        's5_c_re': nrm(ks[20], (L, G, S5_GROUP, P), 0.5),
        's5_c_im': nrm(ks[21], (L, G, S5_GROUP, P), 0.5),
        's5_d': nrm(ks[22], (L, S5_CH), 1.0),
        's5_w_glu': nrm(ks[23], (L, S5_CH, S5_CH), S5_CH ** -0.5),
        'w_out': nrm(ks[24], (L, MIX_W, D), MIX_W ** -0.5),
        'router_w': nrm(ks[25], (L, D, N_EXPERTS), D ** -0.5),
        'router_bias': nrm(ks[26], (L, N_EXPERTS), 0.01),
        'exp_w_gate': nrm(ks[27], (L, N_EXPERTS, D, EXPERT_FF), D ** -0.5),
        'exp_w_up': nrm(ks[28], (L, N_EXPERTS, D, EXPERT_FF), D ** -0.5),
        'exp_w_down': nrm(ks[29], (L, N_EXPERTS, EXPERT_FF, D), EXPERT_FF ** -0.5),
        'shared_w_gate': nrm(ks[30], (L, D, SHARED_FF), D ** -0.5),
        'shared_w_up': nrm(ks[31], (L, D, SHARED_FF), D ** -0.5),
        'shared_w_down': nrm(ks[32], (L, SHARED_FF, D), SHARED_FF ** -0.5),
    }


def reference(x, c, ctx, c_ctx, w_ada, b_ada, norm1_w, norm2_w, w_in, ret_decay_logit, ret_norm_w,
              diff_qnorm_w, diff_knorm_w, diff_lambda, diff_subln_w, s5_lambda_re, s5_lambda_im,
              s5_log_dt, s5_b_re, s5_b_im, s5_c_re, s5_c_im, s5_d, s5_w_glu, w_out, router_w,
              router_bias, exp_w_gate, exp_w_up, exp_w_down, shared_w_gate, shared_w_up, shared_w_down):
    out_dtype = x.dtype
    n = x.shape[1]
    rows = n // GRID_W
    rope_ret = axial_rope_tables(rows, RET_DIM)
    rope_diff = axial_rope_tables(rows, DIFF_QK)
    x = x.astype(F32)
    ctx = ctx.astype(F32)
    for l in range(DEPTH):
        need_ctx = l < DEPTH - 1
        lambda_init = 0.8 - 0.6 * math.exp(-0.3 * l)
        sh1, sc1, g1, sh2, sc2, g2 = jnp.split(adaln(c, w_ada[l], b_ada[l])[:, None, :], N_MOD, axis=-1)
        csh1, csc1, cg1, csh2, csc2, cg2 = jnp.split(adaln(c_ctx[None, :], w_ada[l], b_ada[l])[:, None, :], N_MOD, axis=-1)
        h_lat = rms_norm(x, norm1_w[l]) * (1.0 + sc1) + sh1
        h_ctx = rms_norm(ctx, norm1_w[l]) * (1.0 + csc1) + csh1
        p_lat = split_proj(h_lat @ w_in[l])
        p_ctx = split_proj(h_ctx @ w_in[l])
        ret_lat, ret_ctx = retention_mixer(p_lat[0:4], p_ctx[0:4], ret_decay_logit[l], ret_norm_w[l], rope_ret, need_ctx)
        diff_lat, diff_ctx = diff_attention_mixer(p_lat[4:7], p_ctx[4:7], diff_qnorm_w[l], diff_knorm_w[l],
                                                  diff_lambda[l], diff_subln_w[l], lambda_init, rope_diff, need_ctx)
        s5_lat, s5_ctx = s5_mixer(p_lat[7], p_ctx[7], s5_lambda_re[l], s5_lambda_im[l], s5_log_dt[l],
                                  s5_b_re[l], s5_b_im[l], s5_c_re[l], s5_c_im[l], s5_d[l], s5_w_glu[l], need_ctx)
        x = x + g1 * (jnp.concatenate([ret_lat, diff_lat, s5_lat], axis=-1) @ w_out[l])
        h2 = rms_norm(x, norm2_w[l]) * (1.0 + sc2) + sh2
        x = x + g2 * moe_ffn(h2, router_w[l], router_bias[l], exp_w_gate[l], exp_w_up[l], exp_w_down[l],
                             shared_w_gate[l], shared_w_up[l], shared_w_down[l])
        if need_ctx:
            ctx = ctx + cg1 * (jnp.concatenate([ret_ctx, diff_ctx, s5_ctx], axis=-1) @ w_out[l])
            h2c = rms_norm(ctx, norm2_w[l]) * (1.0 + csc2) + csh2
            ctx = ctx + cg2 * moe_ffn(h2c, router_w[l], router_bias[l], exp_w_gate[l], exp_w_up[l], exp_w_down[l],
                                      shared_w_gate[l], shared_w_up[l], shared_w_down[l])
    return x.astype(out_dtype)
```

```python
import functools
import math

import jax
import jax.numpy as jnp
from jax import lax
from jax.experimental import pallas as pl
from jax.experimental.pallas import tpu as pltpu

F32 = jnp.float32
BF16 = jnp.bfloat16
U32 = jnp.uint32
I32 = jnp.int32

GRID_W = 64
N_MOD = 6
EPS = 1e-6
ROPE_BASE = 10000.0
RET_HEADS = 4
RET_DIM = 128
RET_W = RET_HEADS * RET_DIM
DIFF_HEADS = 8
DIFF_QK = 64
DIFF_V = 2 * DIFF_QK
DIFF_QK_W = DIFF_HEADS * 2 * DIFF_QK
DIFF_W = DIFF_HEADS * DIFF_V
S5_GROUP = 16
S5_CH = 512
S5_GROUPS = S5_CH // S5_GROUP
S5_STATE = 64
S5_W = S5_GROUPS * S5_STATE
N_EXPERTS = 64
EXPERT_FF = 512
TOP_K = 8
N_EXPERT_GROUPS = 8
TOPK_GROUPS = 4
ROUTE_SCALE = 2.5

V7X_VMEM_BYTES = 64 * 1024 * 1024
LANES = 128
SUBLANES = 8
TOK_TILE = 256
RET_CHUNK = 256
ATT_TQ = 256
ATT_TK = 256
S5_BLOCK = 256
EXP_TILE = 256
ADA_TN = 1024


def _cparams(n_axes, vmem_mb=48):
    return pltpu.CompilerParams(
        dimension_semantics=("arbitrary",) * n_axes,
        vmem_limit_bytes=vmem_mb * 1024 * 1024,
    )


def _dot(a, b):
    return jnp.dot(a, b, preferred_element_type=F32)


def _dot_nt(a, b):
    return lax.dot_general(a, b, (((1,), (1,)), ((), ())), preferred_element_type=F32)


def _dot_tn(a, b):
    return lax.dot_general(a, b, (((0,), (0,)), ((), ())), preferred_element_type=F32)


def _split(x):
    hi = x.astype(BF16)
    lo = (x - hi.astype(F32)).astype(BF16)
    return hi, lo


def _dot3(a, b):
    ah, al = _split(a)
    bh, bl = _split(b)
    return _dot(ah, bh) + _dot(al, bh) + _dot(ah, bl)


def _dot3_nt(a, b):
    ah, al = _split(a)
    bh, bl = _split(b)
    return _dot_nt(ah, bh) + _dot_nt(al, bh) + _dot_nt(ah, bl)


def _silu(x):
    return x * jax.nn.sigmoid(x)


def _pack_halves(lo, hi):
    lo_b = pltpu.bitcast(lo.astype(BF16).astype(F32), U32) >> 16
    hi_b = pltpu.bitcast(hi.astype(BF16).astype(F32), U32) & jnp.uint32(0xFFFF0000)
    return hi_b | lo_b


def _unpack_halves(p):
    lo = pltpu.bitcast(p << 16, F32)
    hi = pltpu.bitcast(p & jnp.uint32(0xFFFF0000), F32)
    return lo, hi


def _ada_kernel(cond_ref, w_ref, b_ref, o_ref):
    o_ref[...] = _dot3(_silu(cond_ref[...]), w_ref[...]) + b_ref[...]


def _adaln(cond, w_ada, b_ada):
    n_layers, d, n = w_ada.shape
    rows = cond.shape[0]
    return pl.pallas_call(
        _ada_kernel,
        out_shape=jax.ShapeDtypeStruct((n_layers, rows, n), F32),
        grid=(n_layers, n // ADA_TN),
        in_specs=[
            pl.BlockSpec((rows, d), lambda l, j: (0, 0)),
            pl.BlockSpec((None, d, ADA_TN), lambda l, j: (l, 0, j)),
            pl.BlockSpec((None, 1, ADA_TN), lambda l, j: (l, 0, j)),
        ],
        out_specs=pl.BlockSpec((None, rows, ADA_TN), lambda l, j: (l, 0, j)),
        compiler_params=_cparams(2),
        name="adaln",
    )(cond, w_ada, b_ada.reshape(n_layers, 1, n))


def _proj_kernel(x_ref, nw_ref, sc_ref, sh_ref, w_ref, o_ref):
    x = x_ref[...]
    ms = jnp.mean(x * x, axis=-1, keepdims=True)
    h = x * lax.rsqrt(ms + EPS) * nw_ref[...]
    h = h * (1.0 + sc_ref[...]) + sh_ref[...]
    o_ref[...] = _dot(h.astype(BF16), w_ref[...]).astype(BF16)


def _in_proj(z, norm_w, sc, sh, w_in, mod_row):
    t, d = z.shape
    n = w_in.shape[1]
    mod_spec = pl.BlockSpec((None, 1, d), lambda i: (mod_row(i), 0, 0))
    return pl.pallas_call(
        _proj_kernel,
        out_shape=jax.ShapeDtypeStruct((t, n), BF16),
        grid=(t // TOK_TILE,),
        in_specs=[
            pl.BlockSpec((TOK_TILE, d), lambda i: (i, 0)),
            pl.BlockSpec((1, d), lambda i: (0, 0)),
            mod_spec,
            mod_spec,
            pl.BlockSpec((d, n), lambda i: (0, 0), pipeline_mode=pl.Buffered(1)),
        ],
        out_specs=pl.BlockSpec((TOK_TILE, n), lambda i: (i, 0)),
        compiler_params=_cparams(1, 56),
        name="in_proj",
    )(z, norm_w, sc, sh, w_in)


def _ret_kernel(lg_ref, q_ref, k_ref, v_ref, cos_ref, sin_ref, *rest, reverse, chunk):
    if reverse:
        of_ref, g_ref, nw_ref, o_ref, state_ref = rest
    else:
        o_ref, state_ref = rest
    t = pl.program_id(1)

    @pl.when(t == 0)
    def _():
        state_ref[...] = jnp.zeros_like(state_ref)

    ii = lax.broadcasted_iota(I32, (chunk, chunk), 0)
    jj = lax.broadcasted_iota(I32, (chunk, chunk), 1)
    rel = ((jj - ii) if reverse else (ii - jj)).astype(F32)
    idx = lax.broadcasted_iota(I32, (chunk, 1), 0).astype(F32)
    cos2 = cos_ref[...]
    sin2 = sin_ref[...]
    scale = RET_DIM ** -0.5
    direction = 1 if reverse else 0

    def rope(x):
        return x * cos2 + pltpu.roll(x, RET_DIM // 2, 1) * sin2

    for h in range(RET_HEADS):
        sl = slice(h * RET_DIM, (h + 1) * RET_DIM)
        lg = lg_ref[direction, h]
        dmask = jnp.where(rel >= 0.0, jnp.exp(lg * jnp.maximum(rel, 0.0)), 0.0)
        if reverse:
            qdec = jnp.exp(lg * (chunk - idx))
            kdec = jnp.exp(lg * idx)
        else:
            qdec = jnp.exp(lg * (idx + 1.0))
            kdec = jnp.exp(lg * (chunk - 1.0 - idx))
        cdec = jnp.exp(jnp.full((1, 1), lg * chunk, F32))
        q = rope(q_ref[:, sl].astype(F32))
        k = rope(k_ref[:, sl].astype(F32)) * scale
        v = v_ref[:, sl]
        qb = q.astype(BF16)
        scores = _dot_nt(qb, k.astype(BF16)) * dmask
        inner = _dot(scores.astype(BF16), v)
        s_old = state_ref[h]
        cross = _dot(qb, s_old.astype(BF16)) * qdec
        state_ref[h] = s_old * cdec + _dot_tn((k * kdec).astype(BF16), v)
        o = inner + cross
        if reverse:
            o = o + of_ref[:, sl]
            mu = jnp.mean(o, axis=-1, keepdims=True)
            oc = o - mu
            var = jnp.mean(oc * oc, axis=-1, keepdims=True)
            on = oc * lax.rsqrt(var + EPS) * nw_ref[:, sl]
            o_ref[:, sl] = (_silu(g_ref[:, sl].astype(F32)) * on).astype(BF16)
        else:
            o_ref[:, sl] = o


def _retention(proj3, log_gamma, cos2, sin2, norm_w, ctx_len):
    b, s, _ = proj3.shape
    c = RET_CHUNK
    nch = s // c
    nctx = ctx_len // c

    def fwd_chunk(t):
        return t

    def bwd_chunk(t):
        return jnp.where(t < nctx, nctx - 1 - t, nch - 1 + nctx - t)

    def specs(chunk_of):
        col = lambda j: pl.BlockSpec((None, c, RET_W), lambda bi, t, j=j: (bi, chunk_of(t), j))
        tab = pl.BlockSpec((c, RET_DIM), lambda bi, t: (chunk_of(t), 0))
        return col, tab

    smem = pl.BlockSpec(memory_space=pltpu.SMEM)
    scratch = [pltpu.VMEM((RET_HEADS, RET_DIM, RET_DIM), F32)]

    col, tab = specs(fwd_chunk)
    o_f = pl.pallas_call(
        functools.partial(_ret_kernel, reverse=False, chunk=c),
        out_shape=jax.ShapeDtypeStruct((b, s, RET_W), F32),
        grid=(b, nch),
        in_specs=[smem, col(0), col(1), col(2), tab, tab],
        out_specs=col(0),
        scratch_shapes=scratch,
        compiler_params=_cparams(2),
        name="retention_fwd",
    )(log_gamma, proj3, proj3, proj3, cos2, sin2)

    col, tab = specs(bwd_chunk)
    return pl.pallas_call(
        functools.partial(_ret_kernel, reverse=True, chunk=c),
        out_shape=jax.ShapeDtypeStruct((b, s, RET_W), BF16),
        grid=(b, nch),
        in_specs=[smem, col(0), col(1), col(2), tab, tab, col(0), col(3),
                  pl.BlockSpec((1, RET_W), lambda bi, t: (0, 0))],
        out_specs=col(0),
        scratch_shapes=scratch,
        compiler_params=_cparams(2),
        name="retention_bwd",
    )(log_gamma, proj3, proj3, proj3, cos2, sin2, o_f, proj3, norm_w)


def _dprep_kernel(q_ref, k_ref, qw_ref, kw_ref, c_ref, s1_ref, s2_ref, bd_ref,
                  q0_ref, q1_ref, kk_ref):
    cc = c_ref[...]
    s1 = s1_ref[...]
    s2 = s2_ref[...]
    bd = bd_ref[...]
    rows = q_ref.shape[0]
    first = lax.broadcasted_iota(I32, (rows, LANES), 1) < DIFF_QK
    quarter = DIFF_QK // 2

    def prep(x, w):
        x = x.astype(F32)
        hi, lo = _split(x * x)
        ss = _dot(hi, bd) + _dot(lo, bd)
        xn = x * lax.rsqrt(ss * (1.0 / DIFF_QK) + EPS) * w
        return xn * cc + pltpu.roll(xn, LANES - quarter, 1) * s1 + pltpu.roll(xn, quarter, 1) * s2

    for h in range(DIFF_HEADS):
        sl = slice(h * LANES, (h + 1) * LANES)
        q = prep(q_ref[:, sl], qw_ref[...]) * (DIFF_QK ** -0.5)
        k = prep(k_ref[:, sl], kw_ref[...])
        q0_ref[:, sl] = jnp.where(first, q, 0.0).astype(BF16)
        q1_ref[:, sl] = jnp.where(first, 0.0, q).astype(BF16)
        kk_ref[:, sl] = k.astype(BF16)


def _diff_prep(proj, qnorm_w, knorm_w, dc, ds1, ds2, seq):
    t, _ = proj.shape
    tiles_per_seq = seq // TOK_TILE
    qcol = (4 * RET_W) // DIFF_QK_W
    blockdiag = jnp.kron(jnp.eye(2, dtype=F32), jnp.ones((DIFF_QK, DIFF_QK), F32)).astype(BF16)
    tab = pl.BlockSpec((TOK_TILE, LANES), lambda i: (i % tiles_per_seq, 0))
    row = pl.BlockSpec((1, LANES), lambda i: (0, 0))
    out = pl.BlockSpec((TOK_TILE, DIFF_QK_W), lambda i: (i, 0))
    shp = jax.ShapeDtypeStruct((t, DIFF_QK_W), BF16)
    return pl.pallas_call(
        _dprep_kernel,
        out_shape=(shp, shp, shp),
        grid=(t // TOK_TILE,),
        in_specs=[
            pl.BlockSpec((TOK_TILE, DIFF_QK_W), lambda i: (i, qcol)),
            pl.BlockSpec((TOK_TILE, DIFF_QK_W), lambda i: (i, qcol + 1)),
            row, row, tab, tab, tab,
            pl.BlockSpec((LANES, LANES), lambda i: (0, 0)),
        ],
        out_specs=(out, out, out),
        compiler_params=_cparams(1),
        name="diff_prep",
    )(proj, proj, jnp.tile(qnorm_w, 2)[None], jnp.tile(knorm_w, 2)[None], dc, ds1, ds2, blockdiag)


def _flash_kernel(lam_ref, q0_ref, q1_ref, k_ref, v_ref, sw_ref, o_ref, *, nctx_tiles, ctx_kv,
                  all_kv, lambda_init):
    i = pl.program_id(2)
    tq = q0_ref.shape[0]
    q = jnp.concatenate([q0_ref[...], q1_ref[...]], axis=0)
    nkv = jnp.where(i < nctx_tiles, ctx_kv, all_kv)

    def body(j, carry):
        m, l, acc = carry
        off = pl.multiple_of(j * ATT_TK, ATT_TK)
        kt = k_ref[pl.ds(off, ATT_TK), :]
        vt = v_ref[pl.ds(off, ATT_TK), :]
        s = _dot_nt(q, kt)
        m_new = jnp.maximum(m, jnp.max(s, axis=-1, keepdims=True))
        alpha = jnp.exp(m - m_new)
        p = jnp.exp(s - m_new)
        l = alpha * l + jnp.sum(p, axis=-1, keepdims=True)
        acc = alpha * acc + _dot(p.astype(BF16), vt)
        return m_new, l, acc

    init = (jnp.full((2 * tq, 1), -jnp.inf, F32), jnp.zeros((2 * tq, 1), F32),
            jnp.zeros((2 * tq, DIFF_V), F32))
    _, l, acc = lax.fori_loop(0, nkv, body, init)
    o = acc / l
    lv = lam_ref[...]
    lam = (jnp.exp(jnp.sum(lv[0:1] * lv[1:2], axis=-1, keepdims=True))
           - jnp.exp(jnp.sum(lv[2:3] * lv[3:4], axis=-1, keepdims=True)) + lambda_init)
    d = o[:tq] - lam * o[tq:]
    ms = jnp.mean(d * d, axis=-1, keepdims=True)
    o_ref[...] = (d * lax.rsqrt(ms + EPS) * sw_ref[...] * (1.0 - lambda_init)).astype(BF16)


def _diff_attention(q0, q1, kk, proj3, lam_vecs, subln_w, ctx_len, lambda_init):
    b, s, _ = proj3.shape
    q0 = q0.reshape(b, s, DIFF_QK_W)
    q1 = q1.reshape(b, s, DIFF_QK_W)
    kk = kk.reshape(b, s, DIFF_QK_W)
    vcol = (4 * RET_W + 2 * DIFF_QK_W) // DIFF_V
    qspec = pl.BlockSpec((None, ATT_TQ, DIFF_V), lambda bi, h, i: (bi, i, h))
    return pl.pallas_call(
        functools.partial(_flash_kernel, nctx_tiles=ctx_len // ATT_TQ, ctx_kv=ctx_len // ATT_TK,
                          all_kv=s // ATT_TK, lambda_init=lambda_init),
        out_shape=jax.ShapeDtypeStruct((b, s, DIFF_W), BF16),
        grid=(b, DIFF_HEADS, s // ATT_TQ),
        in_specs=[
            pl.BlockSpec((4, DIFF_QK), lambda bi, h, i: (0, 0)),
            qspec, qspec,
            pl.BlockSpec((None, s, DIFF_V), lambda bi, h, i: (bi, 0, h)),
            pl.BlockSpec((None, s, DIFF_V), lambda bi, h, i: (bi, 0, vcol + h)),
            pl.BlockSpec((1, DIFF_V), lambda bi, h, i: (0, 0)),
        ],
        out_specs=qspec,
        compiler_params=_cparams(3),
        name="diff_flash",
    )(lam_vecs, q0, q1, kk, proj3, subln_w[None])


def _s5_kernel(u_ref, br_ref, bi_ref, cr_ref, ci_ref, tab_ref, *rest, reverse, ntiles):
    if reverse:
        yf_ref, d_ref, wg_ref, o_ref, xr_s, xi_s, car_s, cai_s = rest
    else:
        o_ref, xr_s, xi_s, car_s, cai_s = rest
    t = pl.program_id(1)

    @pl.when(t == 0)
    def _():
        car_s[...] = jnp.zeros_like(car_s)
        cai_s[...] = jnp.zeros_like(cai_s)

    u = u_ref[...]
    xr_s[...] = _dot(u, br_ref[...])
    xi_s[...] = _dot(u, bi_ref[...])

    def body(n, carry):
        cr, ci = carry
        tile = (ntiles - 1 - n) if reverse else n
        off = pl.multiple_of(tile * SUBLANES, SUBLANES)
        xr = xr_s[pl.ds(off, SUBLANES), :]
        xi = xi_s[pl.ds(off, SUBLANES), :]
        for lvl, k in enumerate((1, 2, 4)):
            pr = tab_ref[2 * lvl]
            pi = tab_ref[2 * lvl + 1]
            shift = (SUBLANES - k) if reverse else k
            sr = pltpu.roll(xr, shift, 0)
            si = pltpu.roll(xi, shift, 0)
            xr, xi = xr + pr * sr - pi * si, xi + pr * si + pi * sr
        ar = tab_ref[6]
        ai = tab_ref[7]
        xr, xi = xr + ar * cr - ai * ci, xi + ar * ci + ai * cr
        xr_s[pl.ds(off, SUBLANES), :] = xr
        xi_s[pl.ds(off, SUBLANES), :] = xi
        last = 0 if reverse else SUBLANES - 1
        return xr[last:last + 1], xi[last:last + 1]

    cr, ci = lax.fori_loop(0, ntiles, body, (car_s[...], cai_s[...]))
    car_s[...] = cr
    cai_s[...] = ci
    y = _dot(xr_s[...].astype(BF16), cr_ref[...]) + _dot(xi_s[...].astype(BF16), ci_ref[...])
    if reverse:
        y = y + yf_ref[...] + d_ref[...] * u.astype(F32)
        y = jax.nn.gelu(y, approximate=True)
        o_ref[...] = (y * jax.nn.sigmoid(_dot(y.astype(BF16), wg_ref[...]))).astype(BF16)
    else:
        o_ref[...] = y


def _s5_tables(a_re_log, a_im_ang, reverse):
    w = a_re_log.reshape(1, S5_W)
    th = a_im_ang.reshape(1, S5_W)
    row = jnp.arange(SUBLANES, dtype=F32)[:, None]
    tabs = []
    for k in (1, 2, 4):
        keep = (row <= SUBLANES - 1 - k) if reverse else (row >= k)
        mag = jnp.exp(w * k)
        tabs.append(jnp.where(keep, mag * jnp.cos(th * k), 0.0))
        tabs.append(jnp.where(keep, mag * jnp.sin(th * k), 0.0))
    e = (SUBLANES - row) if reverse else (row + 1.0)
    mag = jnp.exp(w * e)
    tabs.append(mag * jnp.cos(th * e))
    tabs.append(mag * jnp.sin(th * e))
    return jnp.stack(tabs)


def _s5_params(lam_re, lam_im, log_dt, b_re, b_im):
    dt = jnp.exp(log_dt)[:, None]
    wlog = lam_re * dt
    ang = lam_im * dt
    mag = jnp.exp(wlog)
    a_re = mag * jnp.cos(ang)
    a_im = mag * jnp.sin(ang)
    den = lam_re * lam_re + lam_im * lam_im
    nr = a_re - 1.0
    f_re = (nr * lam_re + a_im * lam_im) / den
    f_im = (a_im * lam_re - nr * lam_im) / den
    bb_re = f_re[..., None] * b_re - f_im[..., None] * b_im
    bb_im = f_re[..., None] * b_im + f_im[..., None] * b_re
    eye = jnp.eye(S5_GROUPS, dtype=F32)
    bmat_re = jnp.einsum('gph,gk->ghkp', bb_re, eye).reshape(S5_CH, S5_W).astype(BF16)
    bmat_im = jnp.einsum('gph,gk->ghkp', bb_im, eye).reshape(S5_CH, S5_W).astype(BF16)
    return wlog, ang, bmat_re, bmat_im


def _s5_mixer(proj3, lam_re, lam_im, log_dt, b_re, b_im, c_re, c_im, d, w_glu, ctx_len):
    b, s, _ = proj3.shape
    tb = S5_BLOCK
    nb = s // tb
    nctx = ctx_len // tb
    ucol = (4 * RET_W + 2 * DIFF_QK_W + DIFF_W) // S5_CH
    eye = jnp.eye(S5_GROUPS, dtype=F32)
    cmat_re = jnp.einsum('ghp,gk->gpkh', c_re, eye).reshape(S5_W, S5_CH).astype(BF16)
    cmat_im = (-jnp.einsum('ghp,gk->gpkh', c_im, eye)).reshape(S5_W, S5_CH).astype(BF16)
    full = lambda shape: pl.BlockSpec(shape, lambda bi, t: tuple(0 for _ in shape))
    scratch = [pltpu.VMEM((tb, S5_W), F32), pltpu.VMEM((tb, S5_W), F32),
               pltpu.VMEM((1, S5_W), F32), pltpu.VMEM((1, S5_W), F32)]
    weights = [full((S5_CH, S5_W)), full((S5_CH, S5_W)), full((S5_W, S5_CH)), full((S5_W, S5_CH)),
               full((8, SUBLANES, S5_W))]

    def blk_f(t):
        return t

    def blk_b(t):
        return jnp.where(t < nctx, nctx - 1 - t, nb - 1 + nctx - t)

    wl, ang, bre, bim = _s5_params(lam_re[0], lam_im[0], log_dt[0], b_re, b_im)
    y_f = pl.pallas_call(
        functools.partial(_s5_kernel, reverse=False, ntiles=tb // SUBLANES),
        out_shape=jax.ShapeDtypeStruct((b, s, S5_CH), F32),
        grid=(b, nb),
        in_specs=[pl.BlockSpec((None, tb, S5_CH), lambda bi, t: (bi, blk_f(t), ucol))] + weights,
        out_specs=pl.BlockSpec((None, tb, S5_CH), lambda bi, t: (bi, blk_f(t), 0)),
        scratch_shapes=scratch,
        compiler_params=_cparams(2),
        name="s5_fwd",
    )(proj3, bre, bim, cmat_re, cmat_im, _s5_tables(wl, ang, False))

    wl, ang, bre, bim = _s5_params(lam_re[1], lam_im[1], log_dt[1], b_re, b_im)
    return pl.pallas_call(
        functools.partial(_s5_kernel, reverse=True, ntiles=tb // SUBLANES),
        out_shape=jax.ShapeDtypeStruct((b, s, S5_CH), BF16),
        grid=(b, nb),
        in_specs=[pl.BlockSpec((None, tb, S5_CH), lambda bi, t: (bi, blk_b(t), ucol))] + weights + [
            pl.BlockSpec((None, tb, S5_CH), lambda bi, t: (bi, blk_b(t), 0)),
            full((1, S5_CH)), full((S5_CH, S5_CH))],
        out_specs=pl.BlockSpec((None, tb, S5_CH), lambda bi, t: (bi, blk_b(t), 0)),
        scratch_shapes=scratch,
        compiler_params=_cparams(2),
        name="s5_bwd",
    )(proj3, bre, bim, cmat_re, cmat_im, _s5_tables(wl, ang, True), y_f, d[None],
      w_glu.astype(BF16))


def _neg_inf_like(x):
    return jnp.full(x.shape, -jnp.inf, x.dtype)


def _out_kernel(x_ref, ret_ref, dif_ref, s5_ref, wo_ref, g1_ref, nw_ref, sc_ref, sh_ref,
                rw_ref, rb_ref, tri_ref,
                x1_ref, h2_ref, h2p_ref, te_ref, tw_ref, rk_ref, cnt_ref, cnt_s):
    i = pl.program_id(0)

    @pl.when(i == 0)
    def _():
        cnt_s[...] = jnp.zeros_like(cnt_s)

    d = x_ref.shape[1]
    tm = x_ref.shape[0]
    mix = (_dot(ret_ref[...], wo_ref[0:RET_W, :])
           + _dot(dif_ref[...], wo_ref[RET_W:RET_W + DIFF_W, :])
           + _dot(s5_ref[...], wo_ref[RET_W + DIFF_W:, :]))
    x1 = x_ref[...] + g1_ref[...] * mix
    x1_ref[...] = x1
    ms = jnp.mean(x1 * x1, axis=-1, keepdims=True)
    h2 = x1 * lax.rsqrt(ms + EPS) * nw_ref[...]
    h2 = h2 * (1.0 + sc_ref[...]) + sh_ref[...]
    h2_ref[...] = h2.astype(BF16)
    h2p_ref[...] = _pack_halves(h2[:, :d // 2], h2[:, d // 2:])

    ng = N_EXPERT_GROUPS
    pg = N_EXPERTS // N_EXPERT_GROUPS
    logits = _dot3_nt(rw_ref[...], h2)
    scores = jax.nn.sigmoid(logits)
    sel3 = (scores + rb_ref[...]).reshape(ng, pg, tm)
    scores3 = scores.reshape(ng, pg, tm)
    in_grp = lax.broadcasted_iota(I32, (ng, pg, tm), 1).astype(F32)
    grp = lax.broadcasted_iota(I32, (ng, pg, tm), 0).astype(F32)
    eidx = grp * pg + in_grp
    gidx = lax.broadcasted_iota(I32, (ng, 1, tm), 0).astype(F32)

    m1 = jnp.max(sel3, axis=1, keepdims=True)
    first = jnp.min(jnp.where(sel3 == m1, in_grp, float(pg)), axis=1, keepdims=True)
    m2 = jnp.max(jnp.where(in_grp == first, -jnp.inf, sel3), axis=1, keepdims=True)
    rem = m1 + m2
    gsel = jnp.zeros((ng, 1, tm), F32)
    for _ in range(TOPK_GROUPS):
        mx = jnp.max(rem, axis=0, keepdims=True)
        fi = jnp.min(jnp.where(rem == mx, gidx, float(ng)), axis=0, keepdims=True)
        pick = gidx == fi
        gsel = jnp.where(pick, 1.0, gsel)
        rem = jnp.where(pick, -jnp.inf, rem)
    masked = jnp.where(gsel > 0.0, sel3, -jnp.inf)

    def red2(fn, x):
        return fn(fn(x, axis=1, keepdims=True), axis=0, keepdims=True)

    chosen = jnp.zeros((ng, pg, tm), F32)
    picks = []
    weights = []
    for _ in range(TOP_K):
        mx = red2(jnp.max, masked)
        fi = red2(jnp.min, jnp.where(masked == mx, eidx, float(N_EXPERTS)))
        pick = eidx == fi
        picks.append(fi)
        weights.append(red2(jnp.sum, jnp.where(pick, scores3, 0.0)))
        chosen = jnp.where(pick, 1.0, chosen)
        masked = jnp.where(pick, -jnp.inf, masked)
    wsum = weights[0]
    for w in weights[1:]:
        wsum = wsum + w
    inv = ROUTE_SCALE / wsum

    chosen2 = chosen.reshape(N_EXPERTS, tm)
    cum = _dot(chosen2.astype(BF16), tri_ref[...]) + cnt_s[...]
    cum3 = cum.reshape(ng, pg, tm)
    for k in range(TOP_K):
        te_ref[k:k + 1, :] = picks[k].reshape(1, tm).astype(I32)
        tw_ref[k:k + 1, :] = (weights[k] * inv).reshape(1, tm)
        rk = red2(jnp.sum, jnp.where(eidx == picks[k], cum3, 0.0))
        rk_ref[k:k + 1, :] = rk.reshape(1, tm).astype(I32)
    cnt_new = cnt_s[...] + jnp.sum(chosen2, axis=-1, keepdims=True)
    cnt_s[...] = cnt_new
    cnt_ref[...] = jnp.broadcast_to(cnt_new, cnt_ref.shape)


def _out_proj_route(z, ret, dif, s5, w_out, g1, norm_w, sc, sh, router_w, router_bias, mod_row):
    t, d = z.shape
    tm = TOK_TILE
    tri = (jnp.arange(tm)[:, None] < jnp.arange(tm)[None, :]).astype(BF16)
    mod_spec = pl.BlockSpec((None, 1, d), lambda i: (mod_row(i), 0, 0))
    tok = lambda w: pl.BlockSpec((tm, w), lambda i: (i, 0))
    const = lambda shape: pl.BlockSpec(shape, lambda i: tuple(0 for _ in shape))
    lane_out = pl.BlockSpec((TOP_K, tm), lambda i: (0, i))
    return pl.pallas_call(
        _out_kernel,
        out_shape=(
            jax.ShapeDtypeStruct((t, d), F32),
            jax.ShapeDtypeStruct((t, d), BF16),
            jax.ShapeDtypeStruct((t, d // 2), U32),
            jax.ShapeDtypeStruct((TOP_K, t), I32),
            jax.ShapeDtypeStruct((TOP_K, t), F32),
            jax.ShapeDtypeStruct((TOP_K, t), I32),
            jax.ShapeDtypeStruct((N_EXPERTS, LANES), F32),
        ),
        grid=(t // tm,),
        in_specs=[tok(d), tok(RET_W), tok(DIFF_W), tok(S5_CH), const((d, d)),
                  mod_spec, const((1, d)), mod_spec, mod_spec,
                  const((N_EXPERTS, d)), const((N_EXPERTS, 1)), const((tm, tm))],
        out_specs=(tok(d), tok(d), tok(d // 2), lane_out, lane_out, lane_out,
                   const((N_EXPERTS, LANES))),
        scratch_shapes=[pltpu.VMEM((N_EXPERTS, 1), F32)],
        compiler_params=_cparams(1),
        name="out_proj_route",
    )(z, ret, dif, s5, w_out, g1, norm_w, sc, sh, router_w.T, router_bias[:, None], tri)


def _dispatch_kernel(pos_hbm, x_ref, xs_in, xs_out, pos_s, sem, psem):
    del xs_in
    i = pl.program_id(0)
    tm = x_ref.shape[0]
    n = tm * TOP_K
    cp = pltpu.make_async_copy(pos_hbm.at[pl.ds(pl.multiple_of(i * n, n), n)], pos_s, psem)
    cp.start()
    cp.wait()

    def row_copy(r, p):
        return pltpu.make_async_copy(x_ref.at[pl.ds(r, 1), :], xs_out.at[pl.ds(p, 1), :], sem)

    def issue(r, c):
        for k in range(TOP_K):
            row_copy(r, pos_s[r * TOP_K + k]).start()
        return c

    lax.fori_loop(0, tm, issue, 0)

    def drain(r, c):
        for k in range(TOP_K):
            row_copy(0, 0).wait()
        return c

    lax.fori_loop(0, tm, drain, 0)


def _dispatch(h2p, pos_flat, n_rows):
    t, w = h2p.shape
    tm = TOK_TILE
    xs0 = jnp.zeros((n_rows, w), U32)
    return pl.pallas_call(
        _dispatch_kernel,
        out_shape=jax.ShapeDtypeStruct((n_rows, w), U32),
        grid=(t // tm,),
        in_specs=[pl.BlockSpec(memory_space=pl.ANY),
                  pl.BlockSpec((tm, w), lambda i: (i, 0)),
                  pl.BlockSpec(memory_space=pl.ANY)],
        out_specs=pl.BlockSpec(memory_space=pl.ANY),
        scratch_shapes=[pltpu.SMEM((tm * TOP_K,), I32), pltpu.SemaphoreType.DMA(()),
                        pltpu.SemaphoreType.DMA(())],
        input_output_aliases={2: 0},
        compiler_params=_cparams(1),
        name="moe_dispatch",
    )(pos_flat, h2p, xs0)


def _expert_kernel(te_ref, tv_ref, xs_ref, wg_ref, wu_ref, wd_ref, ys_ref):
    del te_ref
    i = pl.program_id(0)
    half = xs_ref.shape[1]

    @pl.when(tv_ref[i] != 0)
    def _():
        lo, hi = _unpack_halves(xs_ref[...])
        lo = lo.astype(BF16)
        hi = hi.astype(BF16)
        g = _dot(lo, wg_ref[0:half, :]) + _dot(hi, wg_ref[half:, :])
        u = _dot(lo, wu_ref[0:half, :]) + _dot(hi, wu_ref[half:, :])
        y = _dot((_silu(g) * u).astype(BF16), wd_ref[...])
        ys_ref[...] = _pack_halves(y[:, :half], y[:, half:])

    @pl.when(tv_ref[i] == 0)
    def _():
        ys_ref[...] = jnp.zeros_like(ys_ref)


def _experts(xs, tile_expert, tile_valid, w_gate, w_up, w_down):
    n_rows, half = xs.shape
    e, d, f = w_gate.shape
    tm = EXP_TILE
    grid_spec = pltpu.PrefetchScalarGridSpec(
        num_scalar_prefetch=2,
        grid=(n_rows // tm,),
        in_specs=[
            pl.BlockSpec((tm, half), lambda i, te, tv: (i, 0)),
            pl.BlockSpec((None, d, f), lambda i, te, tv: (te[i], 0, 0)),
            pl.BlockSpec((None, d, f), lambda i, te, tv: (te[i], 0, 0)),
            pl.BlockSpec((None, f, d), lambda i, te, tv: (te[i], 0, 0)),
        ],
        out_specs=pl.BlockSpec((tm, half), lambda i, te, tv: (i, 0)),
    )
    return pl.pallas_call(
        _expert_kernel,
        out_shape=jax.ShapeDtypeStruct((n_rows, half), U32),
        grid_spec=grid_spec,
        compiler_params=_cparams(1),
        name="moe_experts",
    )(tile_expert, tile_valid, xs, w_gate, w_up, w_down)


def _combine_kernel(pos_hbm, ys_hbm, x1_ref, h2_ref, w_ref, g2_ref, sg_ref, su_ref, sd_ref,
                    o_ref, pos_s, buf, sem, psem):
    i = pl.program_id(0)
    tm = x1_ref.shape[0]
    half = buf.shape[2]
    n = tm * TOP_K
    cp = pltpu.make_async_copy(pos_hbm.at[pl.ds(pl.multiple_of(i * n, n), n)], pos_s, psem)
    cp.start()
    cp.wait()

    def row_copy(r, k, p):
        return pltpu.make_async_copy(ys_hbm.at[pl.ds(p, 1), :], buf.at[k, pl.ds(r, 1), :], sem)

    def issue(r, c):
        for k in range(TOP_K):
            row_copy(r, k, pos_s[r * TOP_K + k]).start()
        return c

    lax.fori_loop(0, tm, issue, 0)

    h2 = h2_ref[...]
    hid = _silu(_dot(h2, sg_ref[...])) * _dot(h2, su_ref[...])
    shared = _dot(hid.astype(BF16), sd_ref[...])

    def drain(r, c):
        for k in range(TOP_K):
            row_copy(0, k, 0).wait()
        return c

    lax.fori_loop(0, tm, drain, 0)

    w = w_ref[...]
    acc_lo = shared[:, :half]
    acc_hi = shared[:, half:]
    for k in range(TOP_K):
        lo, hi = _unpack_halves(buf[k])
        wk = w[:, k:k + 1]
        acc_lo = acc_lo + wk * lo
        acc_hi = acc_hi + wk * hi
    g2 = g2_ref[...]
    o_ref[:, :half] = x1_ref[:, :half] + g2[:, :half] * acc_lo
    o_ref[:, half:] = x1_ref[:, half:] + g2[:, half:] * acc_hi


def _combine(pos_flat, ys, x1, h2, tw, g2, s_gate, s_up, s_down, mod_row):
    t, d = x1.shape
    tm = TOK_TILE
    f = s_gate.shape[1]
    tok = lambda w: pl.BlockSpec((tm, w), lambda i: (i, 0))
    const = lambda shape: pl.BlockSpec(shape, lambda i: tuple(0 for _ in shape))
    return pl.pallas_call(
        _combine_kernel,
        out_shape=jax.ShapeDtypeStruct((t, d), F32),
        grid=(t // tm,),
        in_specs=[pl.BlockSpec(memory_space=pl.ANY), pl.BlockSpec(memory_space=pl.ANY),
                  tok(d), tok(d), tok(TOP_K),
                  pl.BlockSpec((None, 1, d), lambda i: (mod_row(i), 0, 0)),
                  const((d, f)), const((d, f)), const((f, d))],
        out_specs=tok(d),
        scratch_shapes=[pltpu.SMEM((tm * TOP_K,), I32), pltpu.VMEM((TOP_K, tm, d // 2), U32),
                        pltpu.SemaphoreType.DMA(()), pltpu.SemaphoreType.DMA(())],
        compiler_params=_cparams(1),
        name="moe_combine",
    )(pos_flat, ys, x1, h2, tw, g2, s_gate, s_up, s_down)


def _moe(x1, h2, h2p, te, tw, rk, cnt, g2, w_gate, w_up, w_down, s_gate, s_up, s_down, mod_row):
    t = x1.shape[0]
    tm = EXP_TILE
    n_tiles = (t * TOP_K) // tm + N_EXPERTS
    counts = cnt[:, 0].astype(I32)
    padded = ((counts + tm - 1) // tm) * tm
    pad_end = jnp.cumsum(padded)
    pad_off = pad_end - padded
    pos_flat = (pad_off[te] + rk).T.reshape(-1)
    tile_start = jnp.arange(n_tiles, dtype=I32) * tm
    tile_expert = jnp.minimum(jnp.searchsorted(pad_end, tile_start, side='right'),
                              N_EXPERTS - 1).astype(I32)
    tile_valid = (tile_start < pad_end[-1]).astype(I32)
    xs = _dispatch(h2p, pos_flat, n_tiles * tm)
    ys = _experts(xs, tile_expert, tile_valid, w_gate, w_up, w_down)
    return _combine(pos_flat, ys, x1, h2, tw.T, g2, s_gate, s_up, s_down, mod_row)


def _rope_angles(rows, head_dim):
    axis_dim = head_dim // 2
    inv_freq = ROPE_BASE ** (-jnp.arange(0, axis_dim, 2, dtype=F32) / axis_dim)
    row = jnp.repeat(jnp.arange(rows, dtype=F32), GRID_W)
    col = jnp.tile(jnp.arange(GRID_W, dtype=F32), rows)
    return jnp.concatenate([row[:, None] * inv_freq, col[:, None] * inv_freq], axis=-1)


def _rope_tables(n, ctx_len):
    rows = n // GRID_W
    ang = _rope_angles(rows, RET_DIM)
    cos, sin = jnp.cos(ang), jnp.sin(ang)
    ret_cos = jnp.concatenate([cos, cos], axis=-1)
    ret_sin = jnp.concatenate([-sin, sin], axis=-1)
    ang = _rope_angles(rows, DIFF_QK)
    cos, sin = jnp.cos(ang), jnp.sin(ang)
    zero = jnp.zeros_like(sin)
    dc = jnp.tile(jnp.concatenate([cos, cos], axis=-1), (1, 2))
    ds1 = jnp.tile(jnp.concatenate([-sin, zero], axis=-1), (1, 2))
    ds2 = jnp.tile(jnp.concatenate([zero, sin], axis=-1), (1, 2))

    def with_ctx(tab, fill):
        return jnp.concatenate([jnp.full((ctx_len, tab.shape[1]), fill, F32), tab], axis=0)

    return (with_ctx(ret_cos, 1.0), with_ctx(ret_sin, 0.0),
            with_ctx(dc, 1.0), with_ctx(ds1, 0.0), with_ctx(ds2, 0.0))


def kernel(x, c, ctx, c_ctx, w_ada, b_ada, norm1_w, norm2_w, w_in, ret_decay_logit, ret_norm_w,
           diff_qnorm_w, diff_knorm_w, diff_lambda, diff_subln_w, s5_lambda_re, s5_lambda_im,
           s5_log_dt, s5_b_re, s5_b_im, s5_c_re, s5_c_im, s5_d, s5_w_glu, w_out, router_w,
           router_bias, exp_w_gate, exp_w_up, exp_w_down, shared_w_gate, shared_w_up, shared_w_down):
    out_dtype = x.dtype
    batch, n, d = x.shape
    ctx_len = ctx.shape[1]
    depth = w_ada.shape[0]
    s = ctx_len + n
    t = batch * s
    assert ctx_len % TOK_TILE == 0 and n % TOK_TILE == 0 and n % GRID_W == 0
    assert batch + 1 <= SUBLANES
    tiles_per_seq = s // TOK_TILE
    ctx_tiles = ctx_len // TOK_TILE

    def mod_row(i):
        return jnp.where(i % tiles_per_seq < ctx_tiles, batch, i // tiles_per_seq)

    z = jnp.concatenate([ctx.astype(F32), x.astype(F32)], axis=1).reshape(t, d)
    cond = jnp.zeros((SUBLANES, d), F32).at[:batch].set(c.astype(F32)).at[batch].set(c_ctx.astype(F32))
    mod = _adaln(cond, w_ada.astype(F32), b_ada.astype(F32))
    ret_cos, ret_sin, dc, ds1, ds2 = _rope_tables(n, ctx_len)

    for l in range(depth):
        lambda_init = 0.8 - 0.6 * math.exp(-0.3 * l)
        m = mod[l].reshape(SUBLANES, N_MOD, 1, d)
        sh1, sc1, g1, sh2, sc2, g2 = (m[:, j] for j in range(N_MOD))
        proj = _in_proj(z, norm1_w[l][None], sc1, sh1, w_in[l].astype(BF16), mod_row)
        proj3 = proj.reshape(batch, s, proj.shape[1])
        log_gamma = jax.nn.log_sigmoid(ret_decay_logit[l].astype(F32))
        ret = _retention(proj3, log_gamma, ret_cos, ret_sin, ret_norm_w[l][None], ctx_len)
        q0, q1, kk = _diff_prep(proj, diff_qnorm_w[l], diff_knorm_w[l], dc, ds1, ds2, s)
        dif = _diff_attention(q0, q1, kk, proj3, diff_lambda[l], diff_subln_w[l], ctx_len,
                              lambda_init)
        s5 = _s5_mixer(proj3, s5_lambda_re[l], s5_lambda_im[l], s5_log_dt[l], s5_b_re[l],
                       s5_b_im[l], s5_c_re[l], s5_c_im[l], s5_d[l], s5_w_glu[l], ctx_len)
        x1, h2, h2p, te, tw, rk, cnt = _out_proj_route(
            z, ret.reshape(t, RET_W), dif.reshape(t, DIFF_W), s5.reshape(t, S5_CH),
            w_out[l].astype(BF16), g1, norm2_w[l][None], sc2, sh2, router_w[l], router_bias[l],
            mod_row)
        z = _moe(x1, h2, h2p, te, tw, rk, cnt, g2, exp_w_gate[l].astype(BF16),
                 exp_w_up[l].astype(BF16), exp_w_down[l].astype(BF16),
                 shared_w_gate[l].astype(BF16), shared_w_up[l].astype(BF16),
                 shared_w_down[l].astype(BF16), mod_row)
    return z.reshape(batch, s, d)[:, ctx_len:].astype(out_dtype)
```

```python
import functools
import math

import jax
import jax.numpy as jnp
from jax import lax
from jax.experimental import pallas as pl
from jax.experimental.pallas import tpu as pltpu

F32 = jnp.float32
BF16 = jnp.bfloat16
U32 = jnp.uint32
I32 = jnp.int32

GRID_W = 64
N_MOD = 6
EPS = 1e-6
ROPE_BASE = 10000.0
RET_HEADS = 4
RET_DIM = 128
RET_W = RET_HEADS * RET_DIM
DIFF_HEADS = 8
DIFF_QK = 64
DIFF_V = 2 * DIFF_QK
DIFF_QK_W = DIFF_HEADS * 2 * DIFF_QK
DIFF_W = DIFF_HEADS * DIFF_V
S5_GROUP = 16
S5_CH = 512
S5_GROUPS = S5_CH // S5_GROUP
S5_STATE = 64
S5_W = S5_GROUPS * S5_STATE
N_EXPERTS = 64
EXPERT_FF = 512
TOP_K = 8
N_EXPERT_GROUPS = 8
TOPK_GROUPS = 4
ROUTE_SCALE = 2.5

V7X_VMEM_BYTES = 64 * 1024 * 1024
LANES = 128
SUBLANES = 8
TOK_TILE = 256
RET_CHUNK = 256
ATT_TQ = 256
ATT_TK = 256
ATT_HEADS = 2
LOG2E = math.log2(math.e)
S5_BLOCK = 256
EXP_TILE = 256
ADA_TN = 1024


def _cparams(n_axes, vmem_mb=48):
    return pltpu.CompilerParams(
        dimension_semantics=("arbitrary",) * n_axes,
        vmem_limit_bytes=vmem_mb * 1024 * 1024,
    )


def _dot(a, b):
    return jnp.dot(a, b, preferred_element_type=F32)


def _dot_nt(a, b):
    return lax.dot_general(a, b, (((1,), (1,)), ((), ())), preferred_element_type=F32)


def _dot_tn(a, b):
    return lax.dot_general(a, b, (((0,), (0,)), ((), ())), preferred_element_type=F32)


def _split(x):
    hi = x.astype(BF16)
    lo = (x - hi.astype(F32)).astype(BF16)
    return hi, lo


def _dot3(a, b):
    ah, al = _split(a)
    bh, bl = _split(b)
    return _dot(ah, bh) + _dot(al, bh) + _dot(ah, bl)


def _dot3_nt(a, b):
    ah, al = _split(a)
    bh, bl = _split(b)
    return _dot_nt(ah, bh) + _dot_nt(al, bh) + _dot_nt(ah, bl)


def _silu(x):
    return x * jax.nn.sigmoid(x)


def _pack_halves(lo, hi):
    lo_b = pltpu.bitcast(lo.astype(BF16).astype(F32), U32) >> 16
    hi_b = pltpu.bitcast(hi.astype(BF16).astype(F32), U32) & jnp.uint32(0xFFFF0000)
    return hi_b | lo_b


def _unpack_halves(p):
    lo = pltpu.bitcast(p << 16, F32)
    hi = pltpu.bitcast(p & jnp.uint32(0xFFFF0000), F32)
    return lo, hi


def _ada_kernel(cond_ref, w_ref, b_ref, o_ref):
    o_ref[...] = _dot3(_silu(cond_ref[...]), w_ref[...]) + b_ref[...]


def _adaln(cond, w_ada, b_ada):
    n_layers, d, n = w_ada.shape
    rows = cond.shape[0]
    return pl.pallas_call(
        _ada_kernel,
        out_shape=jax.ShapeDtypeStruct((n_layers, rows, n), F32),
        grid=(n_layers, n // ADA_TN),
        in_specs=[
            pl.BlockSpec((rows, d), lambda l, j: (0, 0)),
            pl.BlockSpec((None, d, ADA_TN), lambda l, j: (l, 0, j)),
            pl.BlockSpec((None, 1, ADA_TN), lambda l, j: (l, 0, j)),
        ],
        out_specs=pl.BlockSpec((None, rows, ADA_TN), lambda l, j: (l, 0, j)),
        compiler_params=_cparams(2),
        name="adaln",
    )(cond, w_ada, b_ada.reshape(n_layers, 1, n))


def _proj_kernel(x_ref, nw_ref, sc_ref, sh_ref, w_ref, o_ref):
    x = x_ref[...]
    ms = jnp.mean(x * x, axis=-1, keepdims=True)
    h = x * lax.rsqrt(ms + EPS) * nw_ref[...]
    h = h * (1.0 + sc_ref[...]) + sh_ref[...]
    o_ref[...] = _dot(h.astype(BF16), w_ref[...]).astype(BF16)


def _in_proj(z, norm_w, sc, sh, w_in, mod_row):
    t, d = z.shape
    n = w_in.shape[1]
    mod_spec = pl.BlockSpec((None, 1, d), lambda i: (mod_row(i), 0, 0))
    return pl.pallas_call(
        _proj_kernel,
        out_shape=jax.ShapeDtypeStruct((t, n), BF16),
        grid=(t // TOK_TILE,),
        in_specs=[
            pl.BlockSpec((TOK_TILE, d), lambda i: (i, 0)),
            pl.BlockSpec((1, d), lambda i: (0, 0)),
            mod_spec,
            mod_spec,
            pl.BlockSpec((d, n), lambda i: (0, 0), pipeline_mode=pl.Buffered(1)),
        ],
        out_specs=pl.BlockSpec((TOK_TILE, n), lambda i: (i, 0)),
        compiler_params=_cparams(1, 56),
        name="in_proj",
    )(z, norm_w, sc, sh, w_in)


def _ret_kernel(lg_ref, q_ref, k_ref, v_ref, cos_ref, sin_ref, *rest, reverse, chunk):
    if reverse:
        of_ref, g_ref, nw_ref, o_ref, state_ref = rest
    else:
        o_ref, state_ref = rest
    t = pl.program_id(1)

    @pl.when(t == 0)
    def _():
        state_ref[...] = jnp.zeros_like(state_ref)

    ii = lax.broadcasted_iota(I32, (chunk, chunk), 0)
    jj = lax.broadcasted_iota(I32, (chunk, chunk), 1)
    rel = ((jj - ii) if reverse else (ii - jj)).astype(F32)
    idx = lax.broadcasted_iota(I32, (chunk, 1), 0).astype(F32)
    cos2 = cos_ref[...]
    sin2 = sin_ref[...]
    scale = RET_DIM ** -0.5
    direction = 1 if reverse else 0

    def rope(x):
        return x * cos2 + pltpu.roll(x, RET_DIM // 2, 1) * sin2

    for h in range(RET_HEADS):
        sl = slice(h * RET_DIM, (h + 1) * RET_DIM)
        lg = lg_ref[direction, h]
        dmask = jnp.where(rel >= 0.0, jnp.exp(lg * jnp.maximum(rel, 0.0)), 0.0)
        if reverse:
            qdec = jnp.exp(lg * (chunk - idx))
            kdec = jnp.exp(lg * idx)
        else:
            qdec = jnp.exp(lg * (idx + 1.0))
            kdec = jnp.exp(lg * (chunk - 1.0 - idx))
        cdec = jnp.exp(jnp.full((1, 1), lg * chunk, F32))
        q = rope(q_ref[:, sl].astype(F32))
        k = rope(k_ref[:, sl].astype(F32)) * scale
        v = v_ref[:, sl]
        qb = q.astype(BF16)
        scores = _dot_nt(qb, k.astype(BF16)) * dmask
        inner = _dot(scores.astype(BF16), v)
        s_old = state_ref[h]
        cross = _dot(qb, s_old.astype(BF16)) * qdec
        state_ref[h] = s_old * cdec + _dot_tn((k * kdec).astype(BF16), v)
        o = inner + cross
        if reverse:
            o = o + of_ref[:, sl]
            mu = jnp.mean(o, axis=-1, keepdims=True)
            oc = o - mu
            var = jnp.mean(oc * oc, axis=-1, keepdims=True)
            on = oc * lax.rsqrt(var + EPS) * nw_ref[:, sl]
            o_ref[:, sl] = (_silu(g_ref[:, sl].astype(F32)) * on).astype(BF16)
        else:
            o_ref[:, sl] = o


def _retention(proj3, log_gamma, cos2, sin2, norm_w, ctx_len):
    b, s, _ = proj3.shape
    c = RET_CHUNK
    nch = s // c
    nctx = ctx_len // c

    def fwd_chunk(t):
        return t

    def bwd_chunk(t):
        return jnp.where(t < nctx, nctx - 1 - t, nch - 1 + nctx - t)

    def specs(chunk_of):
        col = lambda j: pl.BlockSpec((None, c, RET_W), lambda bi, t, j=j: (bi, chunk_of(t), j))
        tab = pl.BlockSpec((c, RET_DIM), lambda bi, t: (chunk_of(t), 0))
        return col, tab

    smem = pl.BlockSpec(memory_space=pltpu.SMEM)
    scratch = [pltpu.VMEM((RET_HEADS, RET_DIM, RET_DIM), F32)]

    col, tab = specs(fwd_chunk)
    o_f = pl.pallas_call(
        functools.partial(_ret_kernel, reverse=False, chunk=c),
        out_shape=jax.ShapeDtypeStruct((b, s, RET_W), F32),
        grid=(b, nch),
        in_specs=[smem, col(0), col(1), col(2), tab, tab],
        out_specs=col(0),
        scratch_shapes=scratch,
        compiler_params=_cparams(2),
        name="retention_fwd",
    )(log_gamma, proj3, proj3, proj3, cos2, sin2)

    col, tab = specs(bwd_chunk)
    return pl.pallas_call(
        functools.partial(_ret_kernel, reverse=True, chunk=c),
        out_shape=jax.ShapeDtypeStruct((b, s, RET_W), BF16),
        grid=(b, nch),
        in_specs=[smem, col(0), col(1), col(2), tab, tab, col(0), col(3),
                  pl.BlockSpec((1, RET_W), lambda bi, t: (0, 0))],
        out_specs=col(0),
        scratch_shapes=scratch,
        compiler_params=_cparams(2),
        name="retention_bwd",
    )(log_gamma, proj3, proj3, proj3, cos2, sin2, o_f, proj3, norm_w)


def _dprep_kernel(q_ref, k_ref, v_ref, qw_ref, kw_ref, c_ref, s1_ref, s2_ref, bd_ref,
                  q0t_ref, q1t_ref, kk_ref, vt_ref):
    cc = c_ref[...]
    s1 = s1_ref[...]
    s2 = s2_ref[...]
    bd = bd_ref[...]
    rows = q_ref.shape[0]
    first = lax.broadcasted_iota(I32, (LANES, rows), 0) < DIFF_QK
    quarter = DIFF_QK // 2

    def prep(x, w):
        x = x.astype(F32)
        hi, lo = _split(x * x)
        ss = _dot(hi, bd) + _dot(lo, bd)
        xn = x * lax.rsqrt(ss * (1.0 / DIFF_QK) + EPS) * w
        return xn * cc + pltpu.roll(xn, LANES - quarter, 1) * s1 + pltpu.roll(xn, quarter, 1) * s2

    for h in range(DIFF_HEADS):
        sl = slice(h * LANES, (h + 1) * LANES)
        qt = (prep(q_ref[:, sl], qw_ref[...]) * (DIFF_QK ** -0.5 * LOG2E)).T
        q0t_ref[sl, :] = jnp.where(first, qt, 0.0).astype(BF16)
        q1t_ref[sl, :] = jnp.where(first, 0.0, qt).astype(BF16)
        kk_ref[:, sl] = prep(k_ref[:, sl], kw_ref[...]).astype(BF16)
        vt_ref[sl, :] = v_ref[:, sl].astype(F32).T.astype(BF16)


def _diff_prep(proj, qnorm_w, knorm_w, dc, ds1, ds2, batch, seq):
    t, _ = proj.shape
    tiles_per_seq = seq // TOK_TILE
    qcol = (4 * RET_W) // DIFF_QK_W
    blockdiag = jnp.kron(jnp.eye(2, dtype=F32), jnp.ones((DIFF_QK, DIFF_QK), F32)).astype(BF16)
    tab = pl.BlockSpec((TOK_TILE, LANES), lambda i: (i % tiles_per_seq, 0))
    row = pl.BlockSpec((1, LANES), lambda i: (0, 0))
    col = lambda j: pl.BlockSpec((TOK_TILE, DIFF_QK_W), lambda i, j=j: (i, j))
    tr = pl.BlockSpec((None, DIFF_QK_W, TOK_TILE),
                      lambda i: (i // tiles_per_seq, 0, i % tiles_per_seq))
    shp_t = jax.ShapeDtypeStruct((batch, DIFF_QK_W, seq), BF16)
    return pl.pallas_call(
        _dprep_kernel,
        out_shape=(shp_t, shp_t, jax.ShapeDtypeStruct((t, DIFF_QK_W), BF16), shp_t),
        grid=(t // TOK_TILE,),
        in_specs=[col(qcol), col(qcol + 1), col(qcol + 2), row, row, tab, tab, tab,
                  pl.BlockSpec((LANES, LANES), lambda i: (0, 0))],
        out_specs=(tr, tr, col(0), tr),
        compiler_params=_cparams(1),
        name="diff_prep",
    )(proj, proj, proj, jnp.tile(qnorm_w, 2)[None], jnp.tile(knorm_w, 2)[None], dc, ds1, ds2,
      blockdiag)


def _flash_kernel(lam_ref, q0t_ref, q1t_ref, k_ref, vt_ref, sw_ref, o_ref, m_s, l_s, acc_s, s_s, *,
                  heads, nctx_tiles, ctx_kv, all_kv, lambda_init):
    i = pl.program_id(2)
    nkv = jnp.where(i < nctx_tiles, ctx_kv, all_kv)
    m_s[...] = jnp.full(m_s.shape, -jnp.inf, F32)
    l_s[...] = jnp.zeros(l_s.shape, F32)
    acc_s[...] = jnp.zeros(acc_s.shape, F32)

    def scores(j, buf):
        off = pl.multiple_of(j * ATT_TK, ATT_TK)
        for h in range(heads):
            sl = slice(h * DIFF_V, (h + 1) * DIFF_V)
            kt = k_ref[pl.ds(off, ATT_TK), sl]
            for comp, qt_ref in enumerate((q0t_ref, q1t_ref)):
                s_s[buf, 2 * h + comp] = _dot(kt, qt_ref[sl, :])

    def softmax_pv(j, buf):
        off = pl.multiple_of(j * ATT_TK, ATT_TK)
        for h in range(heads):
            sl = slice(h * DIFF_V, (h + 1) * DIFF_V)
            vt = vt_ref[sl, pl.ds(off, ATT_TK)]
            for comp in range(2):
                c = 2 * h + comp
                st = s_s[buf, c]
                m_old = m_s[c]
                m_new = jnp.maximum(m_old, jnp.max(st, axis=0, keepdims=True))
                alpha = jnp.exp2(m_old - m_new)
                pt = jnp.exp2(st - m_new)
                l_s[c] = alpha * l_s[c] + jnp.sum(pt, axis=0, keepdims=True)
                acc_s[c] = alpha * acc_s[c] + _dot(vt, pt.astype(BF16))
                m_s[c] = m_new

    scores(0, 0)

    def body(p, carry):
        scores(2 * p + 1, 1)
        softmax_pv(2 * p, 0)
        scores(2 * p + 2, 0)
        softmax_pv(2 * p + 1, 1)
        return carry

    lax.fori_loop(0, (nkv - 1) // 2, body, 0)
    softmax_pv(nkv - 1, 0)
    lv = lam_ref[...]
    lam = (jnp.exp(jnp.sum(lv[0:1] * lv[1:2], axis=-1, keepdims=True))
           - jnp.exp(jnp.sum(lv[2:3] * lv[3:4], axis=-1, keepdims=True)) + lambda_init)
    for h in range(heads):
        sl = slice(h * DIFF_V, (h + 1) * DIFF_V)
        o0 = acc_s[2 * h] / l_s[2 * h]
        o1 = acc_s[2 * h + 1] / l_s[2 * h + 1]
        d = (o0 - lam * o1).T
        ms = jnp.mean(d * d, axis=-1, keepdims=True)
        o_ref[:, sl] = (d * lax.rsqrt(ms + EPS) * sw_ref[...] * (1.0 - lambda_init)).astype(BF16)


def _diff_attention(q0t, q1t, kk, vt, lam_vecs, subln_w, ctx_len, lambda_init):
    b, _, s = q0t.shape
    kk = kk.reshape(b, s, DIFF_QK_W)
    hb = ATT_HEADS
    w = hb * DIFF_V
    assert (s // ATT_TK) % 2 == 1 and (ctx_len // ATT_TK) % 2 == 1
    qspec = pl.BlockSpec((None, w, ATT_TQ), lambda bi, h, i: (bi, h, i))
    return pl.pallas_call(
        functools.partial(_flash_kernel, heads=hb, nctx_tiles=ctx_len // ATT_TQ,
                          ctx_kv=ctx_len // ATT_TK, all_kv=s // ATT_TK, lambda_init=lambda_init),
        out_shape=jax.ShapeDtypeStruct((b, s, DIFF_W), BF16),
        grid=(b, DIFF_HEADS // hb, s // ATT_TQ),
        in_specs=[
            pl.BlockSpec((4, DIFF_QK), lambda bi, h, i: (0, 0)),
            qspec, qspec,
            pl.BlockSpec((None, s, w), lambda bi, h, i: (bi, 0, h)),
            pl.BlockSpec((None, w, s), lambda bi, h, i: (bi, h, 0)),
            pl.BlockSpec((1, DIFF_V), lambda bi, h, i: (0, 0)),
        ],
        out_specs=pl.BlockSpec((None, ATT_TQ, w), lambda bi, h, i: (bi, i, h)),
        scratch_shapes=[pltpu.VMEM((2 * hb, 1, ATT_TQ), F32), pltpu.VMEM((2 * hb, 1, ATT_TQ), F32),
                        pltpu.VMEM((2 * hb, DIFF_V, ATT_TQ), F32),
                        pltpu.VMEM((2, 2 * hb, ATT_TK, ATT_TQ), F32)],
        compiler_params=_cparams(3),
        name="diff_flash",
    )(lam_vecs, q0t, q1t, kk, vt, subln_w[None])


def _s5_kernel(u_ref, br_ref, bi_ref, cr_ref, ci_ref, tab_ref, *rest, reverse, ntiles):
    if reverse:
        yf_ref, d_ref, wg_ref, o_ref, xr_s, xi_s, car_s, cai_s = rest
    else:
        o_ref, xr_s, xi_s, car_s, cai_s = rest
    t = pl.program_id(1)

    @pl.when(t == 0)
    def _():
        car_s[...] = jnp.zeros_like(car_s)
        cai_s[...] = jnp.zeros_like(cai_s)

    u = u_ref[...]
    xr_s[...] = _dot(u, br_ref[...])
    xi_s[...] = _dot(u, bi_ref[...])

    def body(n, carry):
        cr, ci = carry
        tile = (ntiles - 1 - n) if reverse else n
        off = pl.multiple_of(tile * SUBLANES, SUBLANES)
        xr = xr_s[pl.ds(off, SUBLANES), :]
        xi = xi_s[pl.ds(off, SUBLANES), :]
        for lvl, k in enumerate((1, 2, 4)):
            pr = tab_ref[2 * lvl]
            pi = tab_ref[2 * lvl + 1]
            shift = (SUBLANES - k) if reverse else k
            sr = pltpu.roll(xr, shift, 0)
            si = pltpu.roll(xi, shift, 0)
            xr, xi = xr + pr * sr - pi * si, xi + pr * si + pi * sr
        ar = tab_ref[6]
        ai = tab_ref[7]
        xr, xi = xr + ar * cr - ai * ci, xi + ar * ci + ai * cr
        xr_s[pl.ds(off, SUBLANES), :] = xr
        xi_s[pl.ds(off, SUBLANES), :] = xi
        last = 0 if reverse else SUBLANES - 1
        return xr[last:last + 1], xi[last:last + 1]

    cr, ci = lax.fori_loop(0, ntiles, body, (car_s[...], cai_s[...]))
    car_s[...] = cr
    cai_s[...] = ci
    y = _dot(xr_s[...].astype(BF16), cr_ref[...]) + _dot(xi_s[...].astype(BF16), ci_ref[...])
    if reverse:
        y = y + yf_ref[...] + d_ref[...] * u.astype(F32)
        y = jax.nn.gelu(y, approximate=True)
        o_ref[...] = (y * jax.nn.sigmoid(_dot(y.astype(BF16), wg_ref[...]))).astype(BF16)
    else:
        o_ref[...] = y


def _s5_tables(a_re_log, a_im_ang, reverse):
    w = a_re_log.reshape(1, S5_W)
    th = a_im_ang.reshape(1, S5_W)
    row = jnp.arange(SUBLANES, dtype=F32)[:, None]
    tabs = []
    for k in (1, 2, 4):
        keep = (row <= SUBLANES - 1 - k) if reverse else (row >= k)
        mag = jnp.exp(w * k)
        tabs.append(jnp.where(keep, mag * jnp.cos(th * k), 0.0))
        tabs.append(jnp.where(keep, mag * jnp.sin(th * k), 0.0))
    e = (SUBLANES - row) if reverse else (row + 1.0)
    mag = jnp.exp(w * e)
    tabs.append(mag * jnp.cos(th * e))
    tabs.append(mag * jnp.sin(th * e))
    return jnp.stack(tabs)


def _s5_params(lam_re, lam_im, log_dt, b_re, b_im):
    dt = jnp.exp(log_dt)[:, None]
    wlog = lam_re * dt
    ang = lam_im * dt
    mag = jnp.exp(wlog)
    a_re = mag * jnp.cos(ang)
    a_im = mag * jnp.sin(ang)
    den = lam_re * lam_re + lam_im * lam_im
    nr = a_re - 1.0
    f_re = (nr * lam_re + a_im * lam_im) / den
    f_im = (a_im * lam_re - nr * lam_im) / den
    bb_re = f_re[..., None] * b_re - f_im[..., None] * b_im
    bb_im = f_re[..., None] * b_im + f_im[..., None] * b_re
    eye = jnp.eye(S5_GROUPS, dtype=F32)
    bmat_re = jnp.einsum('gph,gk->ghkp', bb_re, eye).reshape(S5_CH, S5_W).astype(BF16)
    bmat_im = jnp.einsum('gph,gk->ghkp', bb_im, eye).reshape(S5_CH, S5_W).astype(BF16)
    return wlog, ang, bmat_re, bmat_im


def _s5_mixer(proj3, lam_re, lam_im, log_dt, b_re, b_im, c_re, c_im, d, w_glu, ctx_len):
    b, s, _ = proj3.shape
    tb = S5_BLOCK
    nb = s // tb
    nctx = ctx_len // tb
    ucol = (4 * RET_W + 2 * DIFF_QK_W + DIFF_W) // S5_CH
    eye = jnp.eye(S5_GROUPS, dtype=F32)
    cmat_re = jnp.einsum('ghp,gk->gpkh', c_re, eye).reshape(S5_W, S5_CH).astype(BF16)
    cmat_im = (-jnp.einsum('ghp,gk->gpkh', c_im, eye)).reshape(S5_W, S5_CH).astype(BF16)
    full = lambda shape: pl.BlockSpec(shape, lambda bi, t: tuple(0 for _ in shape))
    scratch = [pltpu.VMEM((tb, S5_W), F32), pltpu.VMEM((tb, S5_W), F32),
               pltpu.VMEM((1, S5_W), F32), pltpu.VMEM((1, S5_W), F32)]
    weights = [full((S5_CH, S5_W)), full((S5_CH, S5_W)), full((S5_W, S5_CH)), full((S5_W, S5_CH)),
               full((8, SUBLANES, S5_W))]

    def blk_f(t):
        return t

    def blk_b(t):
        return jnp.where(t < nctx, nctx - 1 - t, nb - 1 + nctx - t)

    wl, ang, bre, bim = _s5_params(lam_re[0], lam_im[0], log_dt[0], b_re, b_im)
    y_f = pl.pallas_call(
        functools.partial(_s5_kernel, reverse=False, ntiles=tb // SUBLANES),
        out_shape=jax.ShapeDtypeStruct((b, s, S5_CH), F32),
        grid=(b, nb),
        in_specs=[pl.BlockSpec((None, tb, S5_CH), lambda bi, t: (bi, blk_f(t), ucol))] + weights,
        out_specs=pl.BlockSpec((None, tb, S5_CH), lambda bi, t: (bi, blk_f(t), 0)),
        scratch_shapes=scratch,
        compiler_params=_cparams(2),
        name="s5_fwd",
    )(proj3, bre, bim, cmat_re, cmat_im, _s5_tables(wl, ang, False))

    wl, ang, bre, bim = _s5_params(lam_re[1], lam_im[1], log_dt[1], b_re, b_im)
    return pl.pallas_call(
        functools.partial(_s5_kernel, reverse=True, ntiles=tb // SUBLANES),
        out_shape=jax.ShapeDtypeStruct((b, s, S5_CH), BF16),
        grid=(b, nb),
        in_specs=[pl.BlockSpec((None, tb, S5_CH), lambda bi, t: (bi, blk_b(t), ucol))] + weights + [
            pl.BlockSpec((None, tb, S5_CH), lambda bi, t: (bi, blk_b(t), 0)),
            full((1, S5_CH)), full((S5_CH, S5_CH))],
        out_specs=pl.BlockSpec((None, tb, S5_CH), lambda bi, t: (bi, blk_b(t), 0)),
        scratch_shapes=scratch,
        compiler_params=_cparams(2),
        name="s5_bwd",
    )(proj3, bre, bim, cmat_re, cmat_im, _s5_tables(wl, ang, True), y_f, d[None],
      w_glu.astype(BF16))


def _neg_inf_like(x):
    return jnp.full(x.shape, -jnp.inf, x.dtype)


def _out_kernel(x_ref, ret_ref, dif_ref, s5_ref, wo_ref, g1_ref, nw_ref, sc_ref, sh_ref,
                rw_ref, rb_ref, tri_ref,
                x1_ref, h2_ref, h2p_ref, te_ref, tw_ref, rk_ref, cnt_ref, cnt_s):
    i = pl.program_id(0)

    @pl.when(i == 0)
    def _():
        cnt_s[...] = jnp.zeros_like(cnt_s)

    d = x_ref.shape[1]
    tm = x_ref.shape[0]
    mix = (_dot(ret_ref[...], wo_ref[0:RET_W, :])
           + _dot(dif_ref[...], wo_ref[RET_W:RET_W + DIFF_W, :])
           + _dot(s5_ref[...], wo_ref[RET_W + DIFF_W:, :]))
    x1 = x_ref[...] + g1_ref[...] * mix
    x1_ref[...] = x1
    ms = jnp.mean(x1 * x1, axis=-1, keepdims=True)
    h2 = x1 * lax.rsqrt(ms + EPS) * nw_ref[...]
    h2 = h2 * (1.0 + sc_ref[...]) + sh_ref[...]
    h2_ref[...] = h2.astype(BF16)
    h2p_ref[...] = _pack_halves(h2[:, :d // 2], h2[:, d // 2:])

    ng = N_EXPERT_GROUPS
    pg = N_EXPERTS // N_EXPERT_GROUPS
    logits = _dot3_nt(rw_ref[...], h2)
    scores = jax.nn.sigmoid(logits)
    sel3 = (scores + rb_ref[...]).reshape(ng, pg, tm)
    scores3 = scores.reshape(ng, pg, tm)
    in_grp = lax.broadcasted_iota(I32, (ng, pg, tm), 1).astype(F32)
    grp = lax.broadcasted_iota(I32, (ng, pg, tm), 0).astype(F32)
    eidx = grp * pg + in_grp
    gidx = lax.broadcasted_iota(I32, (ng, 1, tm), 0).astype(F32)

    m1 = jnp.max(sel3, axis=1, keepdims=True)
    first = jnp.min(jnp.where(sel3 == m1, in_grp, float(pg)), axis=1, keepdims=True)
    m2 = jnp.max(jnp.where(in_grp == first, -jnp.inf, sel3), axis=1, keepdims=True)
    rem = m1 + m2
    gsel = jnp.zeros((ng, 1, tm), F32)
    for _ in range(TOPK_GROUPS):
        mx = jnp.max(rem, axis=0, keepdims=True)
        fi = jnp.min(jnp.where(rem == mx, gidx, float(ng)), axis=0, keepdims=True)
        pick = gidx == fi
        gsel = jnp.where(pick, 1.0, gsel)
        rem = jnp.where(pick, -jnp.inf, rem)
    masked = jnp.where(gsel > 0.0, sel3, -jnp.inf)

    def red2(fn, x):
        return fn(fn(x, axis=1, keepdims=True), axis=0, keepdims=True)

    chosen = jnp.zeros((ng, pg, tm), F32)
    picks = []
    weights = []
    for _ in range(TOP_K):
        mx = red2(jnp.max, masked)
        fi = red2(jnp.min, jnp.where(masked == mx, eidx, float(N_EXPERTS)))
        pick = eidx == fi
        picks.append(fi)
        weights.append(red2(jnp.sum, jnp.where(pick, scores3, 0.0)))
        chosen = jnp.where(pick, 1.0, chosen)
        masked = jnp.where(pick, -jnp.inf, masked)
    wsum = weights[0]
    for w in weights[1:]:
        wsum = wsum + w
    inv = ROUTE_SCALE / wsum

    chosen2 = chosen.reshape(N_EXPERTS, tm)
    cum = _dot(chosen2.astype(BF16), tri_ref[...]) + cnt_s[...]
    cum3 = cum.reshape(ng, pg, tm)
    for k in range(TOP_K):
        te_ref[k:k + 1, :] = picks[k].reshape(1, tm).astype(I32)
        tw_ref[k:k + 1, :] = (weights[k] * inv).reshape(1, tm)
        rk = red2(jnp.sum, jnp.where(eidx == picks[k], cum3, 0.0))
        rk_ref[k:k + 1, :] = rk.reshape(1, tm).astype(I32)
    cnt_new = cnt_s[...] + jnp.sum(chosen2, axis=-1, keepdims=True)
    cnt_s[...] = cnt_new
    cnt_ref[...] = jnp.broadcast_to(cnt_new, cnt_ref.shape)


def _out_proj_route(z, ret, dif, s5, w_out, g1, norm_w, sc, sh, router_w, router_bias, mod_row):
    t, d = z.shape
    tm = TOK_TILE
    tri = (jnp.arange(tm)[:, None] < jnp.arange(tm)[None, :]).astype(BF16)
    mod_spec = pl.BlockSpec((None, 1, d), lambda i: (mod_row(i), 0, 0))
    tok = lambda w: pl.BlockSpec((tm, w), lambda i: (i, 0))
    const = lambda shape: pl.BlockSpec(shape, lambda i: tuple(0 for _ in shape))
    lane_out = pl.BlockSpec((TOP_K, tm), lambda i: (0, i))
    return pl.pallas_call(
        _out_kernel,
        out_shape=(
            jax.ShapeDtypeStruct((t, d), F32),
            jax.ShapeDtypeStruct((t, d), BF16),
            jax.ShapeDtypeStruct((t, d // 2), U32),
            jax.ShapeDtypeStruct((TOP_K, t), I32),
            jax.ShapeDtypeStruct((TOP_K, t), F32),
            jax.ShapeDtypeStruct((TOP_K, t), I32),
            jax.ShapeDtypeStruct((N_EXPERTS, LANES), F32),
        ),
        grid=(t // tm,),
        in_specs=[tok(d), tok(RET_W), tok(DIFF_W), tok(S5_CH), const((d, d)),
                  mod_spec, const((1, d)), mod_spec, mod_spec,
                  const((N_EXPERTS, d)), const((N_EXPERTS, 1)), const((tm, tm))],
        out_specs=(tok(d), tok(d), tok(d // 2), lane_out, lane_out, lane_out,
                   const((N_EXPERTS, LANES))),
        scratch_shapes=[pltpu.VMEM((N_EXPERTS, 1), F32)],
        compiler_params=_cparams(1),
        name="out_proj_route",
    )(z, ret, dif, s5, w_out, g1, norm_w, sc, sh, router_w.T, router_bias[:, None], tri)


def _dispatch_kernel(pos_hbm, x_ref, xs_in, xs_out, pos_s, sem, psem):
    del xs_in
    i = pl.program_id(0)
    tm = x_ref.shape[0]
    n = tm * TOP_K
    cp = pltpu.make_async_copy(pos_hbm.at[pl.ds(pl.multiple_of(i * n, n), n)], pos_s, psem)
    cp.start()
    cp.wait()

    def row_copy(r, p):
        return pltpu.make_async_copy(x_ref.at[pl.ds(r, 1), :], xs_out.at[pl.ds(p, 1), :], sem)

    def issue(r, c):
        for k in range(TOP_K):
            row_copy(r, pos_s[r * TOP_K + k]).start()
        return c

    lax.fori_loop(0, tm, issue, 0)

    def drain(r, c):
        for k in range(TOP_K):
            row_copy(0, 0).wait()
        return c

    lax.fori_loop(0, tm, drain, 0)


def _dispatch(h2p, pos_flat, n_rows):
    t, w = h2p.shape
    tm = TOK_TILE
    xs0 = jnp.zeros((n_rows, w), U32)
    return pl.pallas_call(
        _dispatch_kernel,
        out_shape=jax.ShapeDtypeStruct((n_rows, w), U32),
        grid=(t // tm,),
        in_specs=[pl.BlockSpec(memory_space=pl.ANY),
                  pl.BlockSpec((tm, w), lambda i: (i, 0)),
                  pl.BlockSpec(memory_space=pl.ANY)],
        out_specs=pl.BlockSpec(memory_space=pl.ANY),
        scratch_shapes=[pltpu.SMEM((tm * TOP_K,), I32), pltpu.SemaphoreType.DMA(()),
                        pltpu.SemaphoreType.DMA(())],
        input_output_aliases={2: 0},
        compiler_params=_cparams(1),
        name="moe_dispatch",
    )(pos_flat, h2p, xs0)


def _expert_kernel(te_ref, tv_ref, xs_ref, wg_ref, wu_ref, wd_ref, ys_ref, wg_s, wu_s, wd_s):
    i = pl.program_id(0)
    half = xs_ref.shape[1]

    @pl.when((i == 0) | (te_ref[i] != te_ref[jnp.maximum(i - 1, 0)]))
    def _():
        wg_s[...] = wg_ref[...].astype(BF16)
        wu_s[...] = wu_ref[...].astype(BF16)
        wd_s[...] = wd_ref[...].astype(BF16)

    @pl.when(tv_ref[i] != 0)
    def _():
        lo, hi = _unpack_halves(xs_ref[...])
        lo = lo.astype(BF16)
        hi = hi.astype(BF16)
        g = _dot(lo, wg_s[0:half, :]) + _dot(hi, wg_s[half:, :])
        u = _dot(lo, wu_s[0:half, :]) + _dot(hi, wu_s[half:, :])
        y = _dot((_silu(g) * u).astype(BF16), wd_s[...])
        ys_ref[...] = _pack_halves(y[:, :half], y[:, half:])

    @pl.when(tv_ref[i] == 0)
    def _():
        ys_ref[...] = jnp.zeros_like(ys_ref)


def _experts(xs, tile_expert, tile_valid, w_gate, w_up, w_down, layer):
    n_rows, half = xs.shape
    _, e, d, f = w_gate.shape
    tm = EXP_TILE
    grid_spec = pltpu.PrefetchScalarGridSpec(
        num_scalar_prefetch=2,
        grid=(n_rows // tm,),
        in_specs=[
            pl.BlockSpec((tm, half), lambda i, te, tv: (i, 0)),
            pl.BlockSpec((None, None, d, f), lambda i, te, tv: (layer, te[i], 0, 0)),
            pl.BlockSpec((None, None, d, f), lambda i, te, tv: (layer, te[i], 0, 0)),
            pl.BlockSpec((None, None, f, d), lambda i, te, tv: (layer, te[i], 0, 0)),
        ],
        out_specs=pl.BlockSpec((tm, half), lambda i, te, tv: (i, 0)),
        scratch_shapes=[pltpu.VMEM((d, f), BF16), pltpu.VMEM((d, f), BF16), pltpu.VMEM((f, d), BF16)],
    )
    return pl.pallas_call(
        _expert_kernel,
        out_shape=jax.ShapeDtypeStruct((n_rows, half), U32),
        grid_spec=grid_spec,
        compiler_params=_cparams(1, 56),
        name="moe_experts",
    )(tile_expert, tile_valid, xs, w_gate, w_up, w_down)


def _combine_kernel(pos_hbm, ys_hbm, x1_ref, h2_ref, w_ref, g2_ref, sg_ref, su_ref, sd_ref,
                    o_ref, pos_s, buf, sem, psem):
    i = pl.program_id(0)
    tm = x1_ref.shape[0]
    half = buf.shape[2]
    n = tm * TOP_K
    cp = pltpu.make_async_copy(pos_hbm.at[pl.ds(pl.multiple_of(i * n, n), n)], pos_s, psem)
    cp.start()
    cp.wait()

    def row_copy(r, k, p):
        return pltpu.make_async_copy(ys_hbm.at[pl.ds(p, 1), :], buf.at[k, pl.ds(r, 1), :], sem)

    def issue(r, c):
        for k in range(TOP_K):
            row_copy(r, k, pos_s[r * TOP_K + k]).start()
        return c

    lax.fori_loop(0, tm, issue, 0)

    h2 = h2_ref[...]
    hid = _silu(_dot(h2, sg_ref[...])) * _dot(h2, su_ref[...])
    shared = _dot(hid.astype(BF16), sd_ref[...])

    def drain(r, c):
        for k in range(TOP_K):
            row_copy(0, k, 0).wait()
        return c

    lax.fori_loop(0, tm, drain, 0)

    w = w_ref[...]
    acc_lo = shared[:, :half]
    acc_hi = shared[:, half:]
    for k in range(TOP_K):
        lo, hi = _unpack_halves(buf[k])
        wk = w[:, k:k + 1]
        acc_lo = acc_lo + wk * lo
        acc_hi = acc_hi + wk * hi
    g2 = g2_ref[...]
    o_ref[:, :half] = x1_ref[:, :half] + g2[:, :half] * acc_lo
    o_ref[:, half:] = x1_ref[:, half:] + g2[:, half:] * acc_hi


def _combine(pos_flat, ys, x1, h2, tw, g2, s_gate, s_up, s_down, mod_row):
    t, d = x1.shape
    tm = TOK_TILE
    f = s_gate.shape[1]
    tok = lambda w: pl.BlockSpec((tm, w), lambda i: (i, 0))
    const = lambda shape: pl.BlockSpec(shape, lambda i: tuple(0 for _ in shape))
    return pl.pallas_call(
        _combine_kernel,
        out_shape=jax.ShapeDtypeStruct((t, d), F32),
        grid=(t // tm,),
        in_specs=[pl.BlockSpec(memory_space=pl.ANY), pl.BlockSpec(memory_space=pl.ANY),
                  tok(d), tok(d), tok(TOP_K),
                  pl.BlockSpec((None, 1, d), lambda i: (mod_row(i), 0, 0)),
                  const((d, f)), const((d, f)), const((f, d))],
        out_specs=tok(d),
        scratch_shapes=[pltpu.SMEM((tm * TOP_K,), I32), pltpu.VMEM((TOP_K, tm, d // 2), U32),
                        pltpu.SemaphoreType.DMA(()), pltpu.SemaphoreType.DMA(())],
        compiler_params=_cparams(1),
        name="moe_combine",
    )(pos_flat, ys, x1, h2, tw, g2, s_gate, s_up, s_down)


def _moe(x1, h2, h2p, te, tw, rk, cnt, g2, w_gate, w_up, w_down, layer, s_gate, s_up, s_down,
         mod_row):
    t = x1.shape[0]
    tm = EXP_TILE
    n_tiles = (t * TOP_K) // tm + N_EXPERTS
    counts = cnt[:, 0].astype(I32)
    padded = ((counts + tm - 1) // tm) * tm
    pad_end = jnp.cumsum(padded)
    pad_off = pad_end - padded
    experts = jnp.arange(N_EXPERTS, dtype=I32)
    off_of = jnp.sum(jnp.where(te[:, :, None] == experts, pad_off, 0), axis=-1)
    pos_flat = (off_of + rk).T.reshape(-1)
    tile_start = jnp.arange(n_tiles, dtype=I32) * tm
    tile_expert = jnp.minimum(jnp.sum((tile_start[:, None] >= pad_end[None, :]).astype(I32), axis=-1),
                              N_EXPERTS - 1)
    tile_valid = (tile_start < pad_end[-1]).astype(I32)
    xs = _dispatch(h2p, pos_flat, n_tiles * tm)
    ys = _experts(xs, tile_expert, tile_valid, w_gate, w_up, w_down, layer)
    return _combine(pos_flat, ys, x1, h2, tw.T, g2, s_gate, s_up, s_down, mod_row)


def _rope_angles(rows, head_dim):
    axis_dim = head_dim // 2
    inv_freq = ROPE_BASE ** (-jnp.arange(0, axis_dim, 2, dtype=F32) / axis_dim)
    row = jnp.repeat(jnp.arange(rows, dtype=F32), GRID_W)
    col = jnp.tile(jnp.arange(GRID_W, dtype=F32), rows)
    return jnp.concatenate([row[:, None] * inv_freq, col[:, None] * inv_freq], axis=-1)


def _rope_tables(n, ctx_len):
    rows = n // GRID_W
    ang = _rope_angles(rows, RET_DIM)
    cos, sin = jnp.cos(ang), jnp.sin(ang)
    ret_cos = jnp.concatenate([cos, cos], axis=-1)
    ret_sin = jnp.concatenate([-sin, sin], axis=-1)
    ang = _rope_angles(rows, DIFF_QK)
    cos, sin = jnp.cos(ang), jnp.sin(ang)
    zero = jnp.zeros_like(sin)
    dc = jnp.tile(jnp.concatenate([cos, cos], axis=-1), (1, 2))
    ds1 = jnp.tile(jnp.concatenate([-sin, zero], axis=-1), (1, 2))
    ds2 = jnp.tile(jnp.concatenate([zero, sin], axis=-1), (1, 2))

    def with_ctx(tab, fill):
        return jnp.concatenate([jnp.full((ctx_len, tab.shape[1]), fill, F32), tab], axis=0)

    return (with_ctx(ret_cos, 1.0), with_ctx(ret_sin, 0.0),
            with_ctx(dc, 1.0), with_ctx(ds1, 0.0), with_ctx(ds2, 0.0))


def kernel(x, c, ctx, c_ctx, w_ada, b_ada, norm1_w, norm2_w, w_in, ret_decay_logit, ret_norm_w,
           diff_qnorm_w, diff_knorm_w, diff_lambda, diff_subln_w, s5_lambda_re, s5_lambda_im,
           s5_log_dt, s5_b_re, s5_b_im, s5_c_re, s5_c_im, s5_d, s5_w_glu, w_out, router_w,
           router_bias, exp_w_gate, exp_w_up, exp_w_down, shared_w_gate, shared_w_up, shared_w_down):
    out_dtype = x.dtype
    batch, n, d = x.shape
    ctx_len = ctx.shape[1]
    depth = w_ada.shape[0]
    s = ctx_len + n
    t = batch * s
    assert ctx_len % TOK_TILE == 0 and n % TOK_TILE == 0 and n % GRID_W == 0
    assert batch + 1 <= SUBLANES
    tiles_per_seq = s // TOK_TILE
    ctx_tiles = ctx_len // TOK_TILE

    def mod_row(i):
        return jnp.where(i % tiles_per_seq < ctx_tiles, batch, i // tiles_per_seq)

    z = jnp.concatenate([ctx.astype(F32), x.astype(F32)], axis=1).reshape(t, d)
    cond = jnp.zeros((SUBLANES, d), F32).at[:batch].set(c.astype(F32)).at[batch].set(c_ctx.astype(F32))
    mod = _adaln(cond, w_ada.astype(F32), b_ada.astype(F32))
    ret_cos, ret_sin, dc, ds1, ds2 = _rope_tables(n, ctx_len)

    for l in range(depth):
        lambda_init = 0.8 - 0.6 * math.exp(-0.3 * l)
        m = mod[l].reshape(SUBLANES, N_MOD, 1, d)
        sh1, sc1, g1, sh2, sc2, g2 = (m[:, j] for j in range(N_MOD))
        proj = _in_proj(z, norm1_w[l][None], sc1, sh1, w_in[l].astype(BF16), mod_row)
        proj3 = proj.reshape(batch, s, proj.shape[1])
        log_gamma = jax.nn.log_sigmoid(ret_decay_logit[l].astype(F32))
        ret = _retention(proj3, log_gamma, ret_cos, ret_sin, ret_norm_w[l][None], ctx_len)
        q0t, q1t, kk, vt = _diff_prep(proj, diff_qnorm_w[l], diff_knorm_w[l], dc, ds1, ds2,
                                      batch, s)
        dif = _diff_attention(q0t, q1t, kk, vt, diff_lambda[l], diff_subln_w[l], ctx_len,
                              lambda_init)
        s5 = _s5_mixer(proj3, s5_lambda_re[l], s5_lambda_im[l], s5_log_dt[l], s5_b_re[l],
                       s5_b_im[l], s5_c_re[l], s5_c_im[l], s5_d[l], s5_w_glu[l], ctx_len)
        x1, h2, h2p, te, tw, rk, cnt = _out_proj_route(
            z, ret.reshape(t, RET_W), dif.reshape(t, DIFF_W), s5.reshape(t, S5_CH),
            w_out[l].astype(BF16), g1, norm2_w[l][None], sc2, sh2, router_w[l], router_bias[l],
            mod_row)
        z = _moe(x1, h2, h2p, te, tw, rk, cnt, g2, exp_w_gate.astype(F32),
                 exp_w_up.astype(F32), exp_w_down.astype(F32), l,
                 shared_w_gate[l].astype(BF16), shared_w_up[l].astype(BF16),
                 shared_w_down[l].astype(BF16), mod_row)
    return z.reshape(batch, s, d)[:, ctx_len:].astype(out_dtype)
```

```python
import functools
import math

import jax
import jax.numpy as jnp
from jax import lax
from jax.experimental import pallas as pl
from jax.experimental.pallas import tpu as pltpu

F32 = jnp.float32
BF16 = jnp.bfloat16
U32 = jnp.uint32
I32 = jnp.int32

GRID_W = 64
N_MOD = 6
EPS = 1e-6
ROPE_BASE = 10000.0
RET_HEADS = 4
RET_DIM = 128
RET_W = RET_HEADS * RET_DIM
DIFF_HEADS = 8
DIFF_QK = 64
DIFF_V = 2 * DIFF_QK
DIFF_QK_W = DIFF_HEADS * 2 * DIFF_QK
DIFF_W = DIFF_HEADS * DIFF_V
S5_GROUP = 16
S5_CH = 512
S5_GROUPS = S5_CH // S5_GROUP
S5_STATE = 64
S5_W = S5_GROUPS * S5_STATE
N_EXPERTS = 64
EXPERT_FF = 512
TOP_K = 8
N_EXPERT_GROUPS = 8
TOPK_GROUPS = 4
ROUTE_SCALE = 2.5

V7X_VMEM_BYTES = 64 * 1024 * 1024
LANES = 128
SUBLANES = 8
BF16_SUBLANES = 16
TOK_TILE = 256
RET_CHUNK = 256
ATT_TQ = 256
ATT_TK = 256
ATT_HEADS = 2
LOG2E = math.log2(math.e)
S5_BLOCK = 256
EXP_TILE = 256
ADA_TN = 1024
ROW_DMA_UNROLL = 4


def _cparams(n_axes, vmem_mb=48):
    return pltpu.CompilerParams(
        dimension_semantics=("arbitrary",) * n_axes,
        vmem_limit_bytes=vmem_mb * 1024 * 1024,
    )


def _dot(a, b):
    return jnp.dot(a, b, preferred_element_type=F32)


def _dot_nt(a, b):
    return lax.dot_general(a, b, (((1,), (1,)), ((), ())), preferred_element_type=F32)


def _dot_tn(a, b):
    return lax.dot_general(a, b, (((0,), (0,)), ((), ())), preferred_element_type=F32)


def _split(x):
    hi = x.astype(BF16)
    lo = (x - hi.astype(F32)).astype(BF16)
    return hi, lo


def _dot3(a, b):
    ah, al = _split(a)
    bh, bl = _split(b)
    return _dot(ah, bh) + _dot(al, bh) + _dot(ah, bl)


def _dot3_nt(a, b):
    ah, al = _split(a)
    bh, bl = _split(b)
    return _dot_nt(ah, bh) + _dot_nt(al, bh) + _dot_nt(ah, bl)


def _silu(x):
    return x * jax.nn.sigmoid(x)


def _pack_halves(lo, hi):
    lo_b = pltpu.bitcast(lo.astype(BF16).astype(F32), U32) >> 16
    hi_b = pltpu.bitcast(hi.astype(BF16).astype(F32), U32) & jnp.uint32(0xFFFF0000)
    return hi_b | lo_b


def _unpack_halves(p):
    lo = pltpu.bitcast(p << 16, F32)
    hi = pltpu.bitcast(p & jnp.uint32(0xFFFF0000), F32)
    return lo, hi


def _ada_kernel(cond_ref, w_ref, b_ref, o_ref):
    o_ref[...] = _dot3(_silu(cond_ref[...]), w_ref[...]) + b_ref[...]


def _adaln(cond, w_ada, b_ada):
    n_layers, d, n = w_ada.shape
    rows = cond.shape[0]
    return pl.pallas_call(
        _ada_kernel,
        out_shape=jax.ShapeDtypeStruct((n_layers, rows, n), F32),
        grid=(n_layers, n // ADA_TN),
        in_specs=[
            pl.BlockSpec((rows, d), lambda l, j: (0, 0)),
            pl.BlockSpec((None, d, ADA_TN), lambda l, j: (l, 0, j)),
            pl.BlockSpec((None, 1, ADA_TN), lambda l, j: (l, 0, j)),
        ],
        out_specs=pl.BlockSpec((None, rows, ADA_TN), lambda l, j: (l, 0, j)),
        compiler_params=_cparams(2),
        name="adaln",
    )(cond, w_ada, b_ada.reshape(n_layers, 1, n))


def _proj_kernel(x_ref, nw_ref, sc_ref, sh_ref, w_ref, o_ref):
    x = x_ref[...]
    ms = jnp.mean(x * x, axis=-1, keepdims=True)
    h = x * lax.rsqrt(ms + EPS) * nw_ref[...]
    h = h * (1.0 + sc_ref[...]) + sh_ref[...]
    o_ref[...] = _dot(h.astype(BF16), w_ref[...]).astype(BF16)


def _in_proj(z, norm_w, sc, sh, w_in, mod_row):
    t, d = z.shape
    n = w_in.shape[1]
    mod_spec = pl.BlockSpec((None, 1, d), lambda i: (mod_row(i), 0, 0))
    return pl.pallas_call(
        _proj_kernel,
        out_shape=jax.ShapeDtypeStruct((t, n), BF16),
        grid=(t // TOK_TILE,),
        in_specs=[
            pl.BlockSpec((TOK_TILE, d), lambda i: (i, 0)),
            pl.BlockSpec((1, d), lambda i: (0, 0)),
            mod_spec,
            mod_spec,
            pl.BlockSpec((d, n), lambda i: (0, 0), pipeline_mode=pl.Buffered(1)),
        ],
        out_specs=pl.BlockSpec((TOK_TILE, n), lambda i: (i, 0)),
        compiler_params=_cparams(1, 56),
        name="in_proj",
    )(z, norm_w, sc, sh, w_in)


def _ret_kernel(lg_ref, q_ref, k_ref, v_ref, cos_ref, sin_ref, *rest, reverse, chunk):
    if reverse:
        of_ref, g_ref, nw_ref, o_ref, state_ref = rest
    else:
        o_ref, state_ref = rest
    t = pl.program_id(1)

    @pl.when(t == 0)
    def _():
        state_ref[...] = jnp.zeros_like(state_ref)

    ii = lax.broadcasted_iota(I32, (chunk, chunk), 0)
    jj = lax.broadcasted_iota(I32, (chunk, chunk), 1)
    rel = ((jj - ii) if reverse else (ii - jj)).astype(F32)
    idx = lax.broadcasted_iota(I32, (chunk, 1), 0).astype(F32)
    cos2 = cos_ref[...]
    sin2 = sin_ref[...]
    scale = RET_DIM ** -0.5
    direction = 1 if reverse else 0

    def rope(x):
        return x * cos2 + pltpu.roll(x, RET_DIM // 2, 1) * sin2

    for h in range(RET_HEADS):
        sl = slice(h * RET_DIM, (h + 1) * RET_DIM)
        lg = lg_ref[direction, h]
        dmask = jnp.where(rel >= 0.0, jnp.exp(lg * jnp.maximum(rel, 0.0)), 0.0)
        if reverse:
            qdec = jnp.exp(lg * (chunk - idx))
            kdec = jnp.exp(lg * idx)
        else:
            qdec = jnp.exp(lg * (idx + 1.0))
            kdec = jnp.exp(lg * (chunk - 1.0 - idx))
        cdec = jnp.exp(jnp.full((1, 1), lg * chunk, F32))
        q = rope(q_ref[:, sl].astype(F32))
        k = rope(k_ref[:, sl].astype(F32)) * scale
        v = v_ref[:, sl]
        qb = q.astype(BF16)
        scores = _dot_nt(qb, k.astype(BF16)) * dmask
        inner = _dot(scores.astype(BF16), v)
        s_old = state_ref[h]
        cross = _dot(qb, s_old.astype(BF16)) * qdec
        state_ref[h] = s_old * cdec + _dot_tn((k * kdec).astype(BF16), v)
        o = inner + cross
        if reverse:
            o = o + of_ref[:, sl]
            mu = jnp.mean(o, axis=-1, keepdims=True)
            oc = o - mu
            var = jnp.mean(oc * oc, axis=-1, keepdims=True)
            on = oc * lax.rsqrt(var + EPS) * nw_ref[:, sl]
            o_ref[:, sl] = (_silu(g_ref[:, sl].astype(F32)) * on).astype(BF16)
        else:
            o_ref[:, sl] = o


def _retention(proj3, log_gamma, cos2, sin2, norm_w, ctx_len):
    b, s, _ = proj3.shape
    c = RET_CHUNK
    nch = s // c
    nctx = ctx_len // c

    def fwd_chunk(t):
        return t

    def bwd_chunk(t):
        return jnp.where(t < nctx, nctx - 1 - t, nch - 1 + nctx - t)

    def specs(chunk_of):
        col = lambda j: pl.BlockSpec((None, c, RET_W), lambda bi, t, j=j: (bi, chunk_of(t), j))
        tab = pl.BlockSpec((c, RET_DIM), lambda bi, t: (chunk_of(t), 0))
        return col, tab

    smem = pl.BlockSpec(memory_space=pltpu.SMEM)
    scratch = [pltpu.VMEM((RET_HEADS, RET_DIM, RET_DIM), F32)]

    col, tab = specs(fwd_chunk)
    o_f = pl.pallas_call(
        functools.partial(_ret_kernel, reverse=False, chunk=c),
        out_shape=jax.ShapeDtypeStruct((b, s, RET_W), F32),
        grid=(b, nch),
        in_specs=[smem, col(0), col(1), col(2), tab, tab],
        out_specs=col(0),
        scratch_shapes=scratch,
        compiler_params=_cparams(2),
        name="retention_fwd",
    )(log_gamma, proj3, proj3, proj3, cos2, sin2)

    col, tab = specs(bwd_chunk)
    return pl.pallas_call(
        functools.partial(_ret_kernel, reverse=True, chunk=c),
        out_shape=jax.ShapeDtypeStruct((b, s, RET_W), BF16),
        grid=(b, nch),
        in_specs=[smem, col(0), col(1), col(2), tab, tab, col(0), col(3),
                  pl.BlockSpec((1, RET_W), lambda bi, t: (0, 0))],
        out_specs=col(0),
        scratch_shapes=scratch,
        compiler_params=_cparams(2),
        name="retention_bwd",
    )(log_gamma, proj3, proj3, proj3, cos2, sin2, o_f, proj3, norm_w)


def _dprep_kernel(q_ref, k_ref, v_ref, qw_ref, kw_ref, c_ref, s1_ref, s2_ref, bd_ref,
                  q0t_ref, q1t_ref, kk_ref, vt_ref):
    cc = c_ref[...]
    s1 = s1_ref[...]
    s2 = s2_ref[...]
    bd = bd_ref[...]
    rows = q_ref.shape[0]
    first = lax.broadcasted_iota(I32, (LANES, rows), 0) < DIFF_QK
    quarter = DIFF_QK // 2

    def prep(x, w):
        x = x.astype(F32)
        hi, lo = _split(x * x)
        ss = _dot(hi, bd) + _dot(lo, bd)
        xn = x * lax.rsqrt(ss * (1.0 / DIFF_QK) + EPS) * w
        return xn * cc + pltpu.roll(xn, LANES - quarter, 1) * s1 + pltpu.roll(xn, quarter, 1) * s2

    for h in range(DIFF_HEADS):
        sl = slice(h * LANES, (h + 1) * LANES)
        qt = (prep(q_ref[:, sl], qw_ref[...]) * (DIFF_QK ** -0.5 * LOG2E)).T
        q0t_ref[sl, :] = jnp.where(first, qt, 0.0).astype(BF16)
        q1t_ref[sl, :] = jnp.where(first, 0.0, qt).astype(BF16)
        kk_ref[:, sl] = prep(k_ref[:, sl], kw_ref[...]).astype(BF16)
        vt_ref[sl, :] = v_ref[:, sl].astype(F32).T.astype(BF16)


def _diff_prep(proj, qnorm_w, knorm_w, dc, ds1, ds2, batch, seq):
    t, _ = proj.shape
    tiles_per_seq = seq // TOK_TILE
    qcol = (4 * RET_W) // DIFF_QK_W
    blockdiag = jnp.kron(jnp.eye(2, dtype=F32), jnp.ones((DIFF_QK, DIFF_QK), F32)).astype(BF16)
    tab = pl.BlockSpec((TOK_TILE, LANES), lambda i: (i % tiles_per_seq, 0))
    row = pl.BlockSpec((1, LANES), lambda i: (0, 0))
    col = lambda j: pl.BlockSpec((TOK_TILE, DIFF_QK_W), lambda i, j=j: (i, j))
    tr = pl.BlockSpec((None, DIFF_QK_W, TOK_TILE),
                      lambda i: (i // tiles_per_seq, 0, i % tiles_per_seq))
    shp_t = jax.ShapeDtypeStruct((batch, DIFF_QK_W, seq), BF16)
    return pl.pallas_call(
        _dprep_kernel,
        out_shape=(shp_t, shp_t, jax.ShapeDtypeStruct((t, DIFF_QK_W), BF16), shp_t),
        grid=(t // TOK_TILE,),
        in_specs=[col(qcol), col(qcol + 1), col(qcol + 2), row, row, tab, tab, tab,
                  pl.BlockSpec((LANES, LANES), lambda i: (0, 0))],
        out_specs=(tr, tr, col(0), tr),
        compiler_params=_cparams(1),
        name="diff_prep",
    )(proj, proj, proj, jnp.tile(qnorm_w, 2)[None], jnp.tile(knorm_w, 2)[None], dc, ds1, ds2,
      blockdiag)


def _flash_kernel(lam_ref, q0t_ref, q1t_ref, k_ref, vt_ref, sw_ref, o_ref, m_s, acc_s, s_s, *,
                  heads, nctx_tiles, ctx_kv, all_kv, lambda_init):
    i = pl.program_id(2)
    nkv = jnp.where(i < nctx_tiles, ctx_kv, all_kv)
    m_s[...] = jnp.full(m_s.shape, -jnp.inf, F32)
    acc_s[...] = jnp.zeros(acc_s.shape, F32)
    ones = jnp.ones((BF16_SUBLANES, ATT_TK), BF16)

    def scores(j, buf):
        off = pl.multiple_of(j * ATT_TK, ATT_TK)
        for h in range(heads):
            sl = slice(h * DIFF_V, (h + 1) * DIFF_V)
            kt = k_ref[pl.ds(off, ATT_TK), sl]
            for comp, qt_ref in enumerate((q0t_ref, q1t_ref)):
                s_s[buf, 2 * h + comp] = _dot(kt, qt_ref[sl, :])

    def softmax_pv(j, buf):
        off = pl.multiple_of(j * ATT_TK, ATT_TK)
        for h in range(heads):
            sl = slice(h * DIFF_V, (h + 1) * DIFF_V)
            vt = jnp.concatenate([vt_ref[sl, pl.ds(off, ATT_TK)], ones], axis=0)
            for comp in range(2):
                c = 2 * h + comp
                st = s_s[buf, c]
                m_old = m_s[c]
                m_new = jnp.maximum(m_old, jnp.max(st, axis=0, keepdims=True))
                alpha = jnp.exp2(m_old - m_new)
                pt = jnp.exp2(st - m_new)
                acc_s[c] = alpha * acc_s[c] + _dot(vt, pt.astype(BF16))
                m_s[c] = m_new

    scores(0, 0)

    def pairs(first, n_pairs):
        for p in range(n_pairs):
            scores(first + 2 * p + 1, 1)
            softmax_pv(first + 2 * p, 0)
            scores(first + 2 * p + 2, 0)
            softmax_pv(first + 2 * p + 1, 1)

    def quad_body(g, carry):
        pairs(4 * g, 2)
        return carry

    def pair_body(g, carry):
        pairs(nkv - 3 + 2 * g, 1)
        return carry

    lax.fori_loop(0, (nkv - 1) // 4, quad_body, 0)
    lax.fori_loop(0, ((nkv - 1) % 4) // 2, pair_body, 0)
    softmax_pv(nkv - 1, 0)
    lv = lam_ref[...]
    lam = (jnp.exp(jnp.sum(lv[0:1] * lv[1:2], axis=-1, keepdims=True))
           - jnp.exp(jnp.sum(lv[2:3] * lv[3:4], axis=-1, keepdims=True)) + lambda_init)
    for h in range(heads):
        sl = slice(h * DIFF_V, (h + 1) * DIFF_V)
        o0 = acc_s[2 * h, 0:DIFF_V] / acc_s[2 * h, DIFF_V:DIFF_V + 1]
        o1 = acc_s[2 * h + 1, 0:DIFF_V] / acc_s[2 * h + 1, DIFF_V:DIFF_V + 1]
        d = (o0 - lam * o1).T
        ms = jnp.mean(d * d, axis=-1, keepdims=True)
        o_ref[:, sl] = (d * lax.rsqrt(ms + EPS) * sw_ref[...] * (1.0 - lambda_init)).astype(BF16)


def _diff_attention(q0t, q1t, kk, vt, lam_vecs, subln_w, ctx_len, lambda_init):
    b, _, s = q0t.shape
    kk = kk.reshape(b, s, DIFF_QK_W)
    hb = ATT_HEADS
    w = hb * DIFF_V
    assert (s // ATT_TK) % 2 == 1 and (ctx_len // ATT_TK) % 2 == 1
    qspec = pl.BlockSpec((None, w, ATT_TQ), lambda bi, h, i: (bi, h, i))
    return pl.pallas_call(
        functools.partial(_flash_kernel, heads=hb, nctx_tiles=ctx_len // ATT_TQ,
                          ctx_kv=ctx_len // ATT_TK, all_kv=s // ATT_TK, lambda_init=lambda_init),
        out_shape=jax.ShapeDtypeStruct((b, s, DIFF_W), BF16),
        grid=(b, DIFF_HEADS // hb, s // ATT_TQ),
        in_specs=[
            pl.BlockSpec((4, DIFF_QK), lambda bi, h, i: (0, 0)),
            qspec, qspec,
            pl.BlockSpec((None, s, w), lambda bi, h, i: (bi, 0, h)),
            pl.BlockSpec((None, w, s), lambda bi, h, i: (bi, h, 0)),
            pl.BlockSpec((1, DIFF_V), lambda bi, h, i: (0, 0)),
        ],
        out_specs=pl.BlockSpec((None, ATT_TQ, w), lambda bi, h, i: (bi, i, h)),
        scratch_shapes=[pltpu.VMEM((2 * hb, 1, ATT_TQ), F32),
                        pltpu.VMEM((2 * hb, DIFF_V + BF16_SUBLANES, ATT_TQ), F32),
                        pltpu.VMEM((2, 2 * hb, ATT_TK, ATT_TQ), F32)],
        compiler_params=_cparams(3),
        name="diff_flash",
    )(lam_vecs, q0t, q1t, kk, vt, subln_w[None])


def _s5_kernel(u_ref, br_ref, bi_ref, cr_ref, ci_ref, tab_ref, *rest, reverse, ntiles):
    if reverse:
        yf_ref, d_ref, wg_ref, o_ref, xr_s, xi_s, car_s, cai_s = rest
    else:
        o_ref, xr_s, xi_s, car_s, cai_s = rest
    t = pl.program_id(1)

    @pl.when(t == 0)
    def _():
        car_s[...] = jnp.zeros_like(car_s)
        cai_s[...] = jnp.zeros_like(cai_s)

    u = u_ref[...]
    nq = S5_CH // LANES
    sw = S5_W // nq
    for q in range(nq):
        uq = u[:, q * LANES:(q + 1) * LANES]
        xr_s[:, q * sw:(q + 1) * sw] = _dot(uq, br_ref[q])
        xi_s[:, q * sw:(q + 1) * sw] = _dot(uq, bi_ref[q])

    def body(n, carry):
        cr, ci = carry
        tile = (ntiles - 1 - n) if reverse else n
        off = pl.multiple_of(tile * SUBLANES, SUBLANES)
        xr = xr_s[pl.ds(off, SUBLANES), :]
        xi = xi_s[pl.ds(off, SUBLANES), :]
        for lvl, k in enumerate((1, 2, 4)):
            pr = tab_ref[2 * lvl]
            pi = tab_ref[2 * lvl + 1]
            shift = (SUBLANES - k) if reverse else k
            sr = pltpu.roll(xr, shift, 0)
            si = pltpu.roll(xi, shift, 0)
            xr, xi = xr + pr * sr - pi * si, xi + pr * si + pi * sr
        ar = tab_ref[6]
        ai = tab_ref[7]
        xr, xi = xr + ar * cr - ai * ci, xi + ar * ci + ai * cr
        xr_s[pl.ds(off, SUBLANES), :] = xr
        xi_s[pl.ds(off, SUBLANES), :] = xi
        last = 0 if reverse else SUBLANES - 1
        return xr[last:last + 1], xi[last:last + 1]

    cr, ci = lax.fori_loop(0, ntiles, body, (car_s[...], cai_s[...]))
    car_s[...] = cr
    cai_s[...] = ci
    y = jnp.concatenate(
        [_dot(xr_s[:, q * sw:(q + 1) * sw].astype(BF16), cr_ref[q])
         + _dot(xi_s[:, q * sw:(q + 1) * sw].astype(BF16), ci_ref[q]) for q in range(nq)], axis=1)
    if reverse:
        y = y + yf_ref[...] + d_ref[...] * u.astype(F32)
        y = jax.nn.gelu(y, approximate=True)
        o_ref[...] = (y * jax.nn.sigmoid(_dot(y.astype(BF16), wg_ref[...]))).astype(BF16)
    else:
        o_ref[...] = y


def _s5_tables(a_re_log, a_im_ang, reverse):
    w = a_re_log.reshape(1, S5_W)
    th = a_im_ang.reshape(1, S5_W)
    row = jnp.arange(SUBLANES, dtype=F32)[:, None]
    tabs = []
    for k in (1, 2, 4):
        keep = (row <= SUBLANES - 1 - k) if reverse else (row >= k)
        mag = jnp.exp(w * k)
        tabs.append(jnp.where(keep, mag * jnp.cos(th * k), 0.0))
        tabs.append(jnp.where(keep, mag * jnp.sin(th * k), 0.0))
    e = (SUBLANES - row) if reverse else (row + 1.0)
    mag = jnp.exp(w * e)
    tabs.append(mag * jnp.cos(th * e))
    tabs.append(mag * jnp.sin(th * e))
    return jnp.stack(tabs)


def _s5_params(lam_re, lam_im, log_dt, b_re, b_im):
    dt = jnp.exp(log_dt)[:, None]
    wlog = lam_re * dt
    ang = lam_im * dt
    mag = jnp.exp(wlog)
    a_re = mag * jnp.cos(ang)
    a_im = mag * jnp.sin(ang)
    den = lam_re * lam_re + lam_im * lam_im
    nr = a_re - 1.0
    f_re = (nr * lam_re + a_im * lam_im) / den
    f_im = (a_im * lam_re - nr * lam_im) / den
    bb_re = f_re[..., None] * b_re - f_im[..., None] * b_im
    bb_im = f_re[..., None] * b_im + f_im[..., None] * b_re
    eye = jnp.eye(S5_GROUPS, dtype=F32)
    bmat_re = jnp.einsum('gph,gk->ghkp', bb_re, eye).reshape(S5_CH, S5_W)
    bmat_im = jnp.einsum('gph,gk->ghkp', bb_im, eye).reshape(S5_CH, S5_W)
    return wlog, ang, _diag_blocks(bmat_re).astype(BF16), _diag_blocks(bmat_im).astype(BF16)


def _diag_blocks(m):
    nq = S5_CH // LANES
    r, c = m.shape[0] // nq, m.shape[1] // nq
    return jnp.stack([m[q * r:(q + 1) * r, q * c:(q + 1) * c] for q in range(nq)])


def _s5_mixer(proj3, lam_re, lam_im, log_dt, b_re, b_im, c_re, c_im, d, w_glu, ctx_len):
    b, s, _ = proj3.shape
    tb = S5_BLOCK
    nb = s // tb
    nctx = ctx_len // tb
    ucol = (4 * RET_W + 2 * DIFF_QK_W + DIFF_W) // S5_CH
    eye = jnp.eye(S5_GROUPS, dtype=F32)
    cmat_re = _diag_blocks(jnp.einsum('ghp,gk->gpkh', c_re, eye).reshape(S5_W, S5_CH)).astype(BF16)
    cmat_im = _diag_blocks(-jnp.einsum('ghp,gk->gpkh', c_im, eye).reshape(S5_W, S5_CH)).astype(BF16)
    full = lambda shape: pl.BlockSpec(shape, lambda bi, t: tuple(0 for _ in shape))
    scratch = [pltpu.VMEM((tb, S5_W), F32), pltpu.VMEM((tb, S5_W), F32),
               pltpu.VMEM((1, S5_W), F32), pltpu.VMEM((1, S5_W), F32)]
    nq = S5_CH // LANES
    weights = [full((nq, LANES, S5_W // nq)), full((nq, LANES, S5_W // nq)),
               full((nq, S5_W // nq, LANES)), full((nq, S5_W // nq, LANES)),
               full((8, SUBLANES, S5_W))]

    def blk_f(t):
        return t

    def blk_b(t):
        return jnp.where(t < nctx, nctx - 1 - t, nb - 1 + nctx - t)

    wl, ang, bre, bim = _s5_params(lam_re[0], lam_im[0], log_dt[0], b_re, b_im)
    y_f = pl.pallas_call(
        functools.partial(_s5_kernel, reverse=False, ntiles=tb // SUBLANES),
        out_shape=jax.ShapeDtypeStruct((b, s, S5_CH), F32),
        grid=(b, nb),
        in_specs=[pl.BlockSpec((None, tb, S5_CH), lambda bi, t: (bi, blk_f(t), ucol))] + weights,
        out_specs=pl.BlockSpec((None, tb, S5_CH), lambda bi, t: (bi, blk_f(t), 0)),
        scratch_shapes=scratch,
        compiler_params=_cparams(2),
        name="s5_fwd",
    )(proj3, bre, bim, cmat_re, cmat_im, _s5_tables(wl, ang, False))

    wl, ang, bre, bim = _s5_params(lam_re[1], lam_im[1], log_dt[1], b_re, b_im)
    return pl.pallas_call(
        functools.partial(_s5_kernel, reverse=True, ntiles=tb // SUBLANES),
        out_shape=jax.ShapeDtypeStruct((b, s, S5_CH), BF16),
        grid=(b, nb),
        in_specs=[pl.BlockSpec((None, tb, S5_CH), lambda bi, t: (bi, blk_b(t), ucol))] + weights + [
            pl.BlockSpec((None, tb, S5_CH), lambda bi, t: (bi, blk_b(t), 0)),
            full((1, S5_CH)), full((S5_CH, S5_CH))],
        out_specs=pl.BlockSpec((None, tb, S5_CH), lambda bi, t: (bi, blk_b(t), 0)),
        scratch_shapes=scratch,
        compiler_params=_cparams(2),
        name="s5_bwd",
    )(proj3, bre, bim, cmat_re, cmat_im, _s5_tables(wl, ang, True), y_f, d[None],
      w_glu.astype(BF16))


def _neg_inf_like(x):
    return jnp.full(x.shape, -jnp.inf, x.dtype)


def _out_kernel(x_ref, ret_ref, dif_ref, s5_ref, wo_ref, g1_ref, nw_ref, sc_ref, sh_ref,
                rw_ref, rb_ref, tri_ref,
                x1_ref, h2_ref, h2p_ref, te_ref, tw_ref, rk_ref, cnt_ref, cnt_s):
    i = pl.program_id(0)

    @pl.when(i == 0)
    def _():
        cnt_s[...] = jnp.zeros_like(cnt_s)

    d = x_ref.shape[1]
    tm = x_ref.shape[0]
    mix = (_dot(ret_ref[...], wo_ref[0:RET_W, :])
           + _dot(dif_ref[...], wo_ref[RET_W:RET_W + DIFF_W, :])
           + _dot(s5_ref[...], wo_ref[RET_W + DIFF_W:, :]))
    x1 = x_ref[...] + g1_ref[...] * mix
    x1_ref[...] = x1
    ms = jnp.mean(x1 * x1, axis=-1, keepdims=True)
    h2 = x1 * lax.rsqrt(ms + EPS) * nw_ref[...]
    h2 = h2 * (1.0 + sc_ref[...]) + sh_ref[...]
    h2_ref[...] = h2.astype(BF16)
    h2p_ref[...] = _pack_halves(h2[:, :d // 2], h2[:, d // 2:])

    ng = N_EXPERT_GROUPS
    pg = N_EXPERTS // N_EXPERT_GROUPS
    logits = _dot3_nt(rw_ref[...], h2)
    scores = jax.nn.sigmoid(logits)
    sel3 = (scores + rb_ref[...]).reshape(ng, pg, tm)
    scores3 = scores.reshape(ng, pg, tm)
    in_grp = lax.broadcasted_iota(I32, (ng, pg, tm), 1).astype(F32)
    grp = lax.broadcasted_iota(I32, (ng, pg, tm), 0).astype(F32)
    eidx = grp * pg + in_grp
    gidx = lax.broadcasted_iota(I32, (ng, 1, tm), 0).astype(F32)

    m1 = jnp.max(sel3, axis=1, keepdims=True)
    first = jnp.min(jnp.where(sel3 == m1, in_grp, float(pg)), axis=1, keepdims=True)
    m2 = jnp.max(jnp.where(in_grp == first, -jnp.inf, sel3), axis=1, keepdims=True)
    rem = m1 + m2
    gsel = jnp.zeros((ng, 1, tm), F32)
    for _ in range(TOPK_GROUPS):
        mx = jnp.max(rem, axis=0, keepdims=True)
        fi = jnp.min(jnp.where(rem == mx, gidx, float(ng)), axis=0, keepdims=True)
        pick = gidx == fi
        gsel = jnp.where(pick, 1.0, gsel)
        rem = jnp.where(pick, -jnp.inf, rem)
    masked = jnp.where(gsel > 0.0, sel3, -jnp.inf)

    def red2(fn, x):
        return fn(fn(x, axis=1, keepdims=True), axis=0, keepdims=True)

    chosen = jnp.zeros((ng, pg, tm), F32)
    picks = []
    weights = []
    for _ in range(TOP_K):
        mx = red2(jnp.max, masked)
        fi = red2(jnp.min, jnp.where(masked == mx, eidx, float(N_EXPERTS)))
        pick = eidx == fi
        picks.append(fi)
        weights.append(red2(jnp.sum, jnp.where(pick, scores3, 0.0)))
        chosen = jnp.where(pick, 1.0, chosen)
        masked = jnp.where(pick, -jnp.inf, masked)
    wsum = weights[0]
    for w in weights[1:]:
        wsum = wsum + w
    inv = ROUTE_SCALE / wsum

    chosen2 = chosen.reshape(N_EXPERTS, tm)
    cum = _dot(chosen2.astype(BF16), tri_ref[...]) + cnt_s[...]
    cum3 = cum.reshape(ng, pg, tm)
    for k in range(TOP_K):
        te_ref[k:k + 1, :] = picks[k].reshape(1, tm).astype(I32)
        tw_ref[k:k + 1, :] = (weights[k] * inv).reshape(1, tm)
        rk = red2(jnp.sum, jnp.where(eidx == picks[k], cum3, 0.0))
        rk_ref[k:k + 1, :] = rk.reshape(1, tm).astype(I32)
    cnt_new = cnt_s[...] + jnp.sum(chosen2, axis=-1, keepdims=True)
    cnt_s[...] = cnt_new
    cnt_ref[...] = jnp.broadcast_to(cnt_new, cnt_ref.shape)


def _out_proj_route(z, ret, dif, s5, w_out, g1, norm_w, sc, sh, router_w, router_bias, mod_row):
    t, d = z.shape
    tm = TOK_TILE
    tri = (jnp.arange(tm)[:, None] < jnp.arange(tm)[None, :]).astype(BF16)
    mod_spec = pl.BlockSpec((None, 1, d), lambda i: (mod_row(i), 0, 0))
    tok = lambda w: pl.BlockSpec((tm, w), lambda i: (i, 0))
    const = lambda shape: pl.BlockSpec(shape, lambda i: tuple(0 for _ in shape))
    lane_out = pl.BlockSpec((TOP_K, tm), lambda i: (0, i))
    return pl.pallas_call(
        _out_kernel,
        out_shape=(
            jax.ShapeDtypeStruct((t, d), F32),
            jax.ShapeDtypeStruct((t, d), BF16),
            jax.ShapeDtypeStruct((t, d // 2), U32),
            jax.ShapeDtypeStruct((TOP_K, t), I32),
            jax.ShapeDtypeStruct((TOP_K, t), F32),
            jax.ShapeDtypeStruct((TOP_K, t), I32),
            jax.ShapeDtypeStruct((N_EXPERTS, LANES), F32),
        ),
        grid=(t // tm,),
        in_specs=[tok(d), tok(RET_W), tok(DIFF_W), tok(S5_CH), const((d, d)),
                  mod_spec, const((1, d)), mod_spec, mod_spec,
                  const((N_EXPERTS, d)), const((N_EXPERTS, 1)), const((tm, tm))],
        out_specs=(tok(d), tok(d), tok(d // 2), lane_out, lane_out, lane_out,
                   const((N_EXPERTS, LANES))),
        scratch_shapes=[pltpu.VMEM((N_EXPERTS, 1), F32)],
        compiler_params=_cparams(1),
        name="out_proj_route",
    )(z, ret, dif, s5, w_out, g1, norm_w, sc, sh, router_w.T, router_bias[:, None], tri)


def _dispatch_kernel(pos_hbm, x_ref, xs_in, xs_out, pos_s, sem, psem):
    del xs_in
    i = pl.program_id(0)
    tm = x_ref.shape[0]
    n = tm * TOP_K
    cp = pltpu.make_async_copy(pos_hbm.at[pl.ds(pl.multiple_of(i * n, n), n)], pos_s, psem)
    cp.start()
    cp.wait()

    def row_copy(r, p):
        return pltpu.make_async_copy(x_ref.at[pl.ds(r, 1), :], xs_out.at[pl.ds(p, 1), :], sem)

    def issue(r, c):
        for k in range(TOP_K):
            row_copy(r, pos_s[r * TOP_K + k]).start(priority=k % 2)
        return c

    lax.fori_loop(0, tm, issue, 0, unroll=ROW_DMA_UNROLL)

    def drain(r, c):
        for k in range(TOP_K):
            row_copy(0, 0).wait()
        return c

    lax.fori_loop(0, tm, drain, 0)


def _zero_tiles_kernel(tile_ref, o_ref):
    del tile_ref
    o_ref[...] = jnp.zeros_like(o_ref)


def _dispatch(h2p, pos_flat, pad_tiles, n_rows):
    t, w = h2p.shape
    tm = TOK_TILE
    xs0 = pl.pallas_call(
        _zero_tiles_kernel,
        out_shape=jax.ShapeDtypeStruct((n_rows, w), U32),
        grid_spec=pltpu.PrefetchScalarGridSpec(
            num_scalar_prefetch=1, grid=(pad_tiles.shape[0],), in_specs=[],
            out_specs=pl.BlockSpec((EXP_TILE, w), lambda e, tiles: (tiles[e], 0))),
        compiler_params=_cparams(1),
        name="moe_zero_pad_tiles",
    )(pad_tiles)
    return pl.pallas_call(
        _dispatch_kernel,
        out_shape=jax.ShapeDtypeStruct((n_rows, w), U32),
        grid=(t // tm,),
        in_specs=[pl.BlockSpec(memory_space=pl.ANY),
                  pl.BlockSpec((tm, w), lambda i: (i, 0)),
                  pl.BlockSpec(memory_space=pl.ANY)],
        out_specs=pl.BlockSpec(memory_space=pl.ANY),
        scratch_shapes=[pltpu.SMEM((tm * TOP_K,), I32), pltpu.SemaphoreType.DMA(()),
                        pltpu.SemaphoreType.DMA(())],
        input_output_aliases={2: 0},
        compiler_params=_cparams(1),
        name="moe_dispatch",
    )(pos_flat, h2p, xs0)


def _expert_kernel(te_ref, tv_ref, xs_ref, wg_ref, wu_ref, wd_ref, ys_ref, wg_s, wu_s, wd_s):
    i = pl.program_id(0)
    half = xs_ref.shape[1]

    @pl.when((i == 0) | (te_ref[i] != te_ref[jnp.maximum(i - 1, 0)]))
    def _():
        wg_s[...] = wg_ref[...].astype(BF16)
        wu_s[...] = wu_ref[...].astype(BF16)
        wd_s[...] = wd_ref[...].astype(BF16)

    @pl.when(tv_ref[i] != 0)
    def _():
        lo, hi = _unpack_halves(xs_ref[...])
        lo = lo.astype(BF16)
        hi = hi.astype(BF16)
        g = _dot(lo, wg_s[0:half, :]) + _dot(hi, wg_s[half:, :])
        u = _dot(lo, wu_s[0:half, :]) + _dot(hi, wu_s[half:, :])
        y = _dot((_silu(g) * u).astype(BF16), wd_s[...])
        ys_ref[...] = _pack_halves(y[:, :half], y[:, half:])

    @pl.when(tv_ref[i] == 0)
    def _():
        ys_ref[...] = jnp.zeros_like(ys_ref)


def _experts(xs, tile_expert, tile_valid, w_gate, w_up, w_down, layer):
    n_rows, half = xs.shape
    _, e, d, f = w_gate.shape
    tm = EXP_TILE
    grid_spec = pltpu.PrefetchScalarGridSpec(
        num_scalar_prefetch=2,
        grid=(n_rows // tm,),
        in_specs=[
            pl.BlockSpec((tm, half), lambda i, te, tv: (jnp.where(tv[i] != 0, i, 0), 0)),
            pl.BlockSpec((None, None, d, f), lambda i, te, tv: (layer, te[i], 0, 0)),
            pl.BlockSpec((None, None, d, f), lambda i, te, tv: (layer, te[i], 0, 0)),
            pl.BlockSpec((None, None, f, d), lambda i, te, tv: (layer, te[i], 0, 0)),
        ],
        out_specs=pl.BlockSpec((tm, half), lambda i, te, tv: (i, 0)),
        scratch_shapes=[pltpu.VMEM((d, f), BF16), pltpu.VMEM((d, f), BF16), pltpu.VMEM((f, d), BF16)],
    )
    return pl.pallas_call(
        _expert_kernel,
        out_shape=jax.ShapeDtypeStruct((n_rows, half), U32),
        grid_spec=grid_spec,
        compiler_params=_cparams(1, 56),
        name="moe_experts",
    )(tile_expert, tile_valid, xs, w_gate, w_up, w_down)


def _combine_kernel(pos_hbm, ys_hbm, x1_ref, h2_ref, w_ref, g2_ref, sg_ref, su_ref, sd_ref,
                    o_ref, pos_s, buf, sem, psem):
    i = pl.program_id(0)
    tm = x1_ref.shape[0]
    half = buf.shape[2]
    n = tm * TOP_K
    cp = pltpu.make_async_copy(pos_hbm.at[pl.ds(pl.multiple_of(i * n, n), n)], pos_s, psem)
    cp.start()
    cp.wait()

    def row_copy(r, k, p):
        return pltpu.make_async_copy(ys_hbm.at[pl.ds(p, 1), :], buf.at[k, pl.ds(r, 1), :], sem)

    def issue(r, c):
        for k in range(TOP_K):
            row_copy(r, k, pos_s[r * TOP_K + k]).start(priority=k % 2)
        return c

    lax.fori_loop(0, tm, issue, 0, unroll=ROW_DMA_UNROLL)

    h2 = h2_ref[...]
    hid = _silu(_dot(h2, sg_ref[...])) * _dot(h2, su_ref[...])
    shared = _dot(hid.astype(BF16), sd_ref[...])

    def drain(r, c):
        for k in range(TOP_K):
            row_copy(0, k, 0).wait()
        return c

    lax.fori_loop(0, tm, drain, 0)

    w = w_ref[...]
    acc_lo = shared[:, :half]
    acc_hi = shared[:, half:]
    for k in range(TOP_K):
        lo, hi = _unpack_halves(buf[k])
        wk = w[:, k:k + 1]
        acc_lo = acc_lo + wk * lo
        acc_hi = acc_hi + wk * hi
    g2 = g2_ref[...]
    o_ref[:, :half] = x1_ref[:, :half] + g2[:, :half] * acc_lo
    o_ref[:, half:] = x1_ref[:, half:] + g2[:, half:] * acc_hi


def _combine(pos_flat, ys, x1, h2, tw, g2, s_gate, s_up, s_down, mod_row):
    t, d = x1.shape
    tm = TOK_TILE
    f = s_gate.shape[1]
    tok = lambda w: pl.BlockSpec((tm, w), lambda i: (i, 0))
    const = lambda shape: pl.BlockSpec(shape, lambda i: tuple(0 for _ in shape))
    return pl.pallas_call(
        _combine_kernel,
        out_shape=jax.ShapeDtypeStruct((t, d), F32),
        grid=(t // tm,),
        in_specs=[pl.BlockSpec(memory_space=pl.ANY), pl.BlockSpec(memory_space=pl.ANY),
                  tok(d), tok(d), tok(TOP_K),
                  pl.BlockSpec((None, 1, d), lambda i: (mod_row(i), 0, 0)),
                  const((d, f)), const((d, f)), const((f, d))],
        out_specs=tok(d),
        scratch_shapes=[pltpu.SMEM((tm * TOP_K,), I32), pltpu.VMEM((TOP_K, tm, d // 2), U32),
                        pltpu.SemaphoreType.DMA(()), pltpu.SemaphoreType.DMA(())],
        compiler_params=_cparams(1),
        name="moe_combine",
    )(pos_flat, ys, x1, h2, tw, g2, s_gate, s_up, s_down)


def _moe(x1, h2, h2p, te, tw, rk, cnt, g2, w_gate, w_up, w_down, layer, s_gate, s_up, s_down,
         mod_row):
    t = x1.shape[0]
    tm = EXP_TILE
    n_tiles = (t * TOP_K) // tm + N_EXPERTS
    counts = cnt[:, 0].astype(I32)
    padded = ((counts + tm - 1) // tm) * tm
    pad_end = jnp.cumsum(padded)
    pad_off = pad_end - padded
    experts = jnp.arange(N_EXPERTS, dtype=I32)
    off_of = jnp.sum(jnp.where(te[:, :, None] == experts, pad_off, 0), axis=-1)
    pos_flat = (off_of + rk).T.reshape(-1)
    tile_start = jnp.arange(n_tiles, dtype=I32) * tm
    tile_expert = jnp.minimum(jnp.sum((tile_start[:, None] >= pad_end[None, :]).astype(I32), axis=-1),
                              N_EXPERTS - 1)
    tile_valid = (tile_start < pad_end[-1]).astype(I32)
    pad_tiles = jnp.maximum(pad_end // tm - 1, 0).astype(I32)
    xs = _dispatch(h2p, pos_flat, pad_tiles, n_tiles * tm)
    ys = _experts(xs, tile_expert, tile_valid, w_gate, w_up, w_down, layer)
    return _combine(pos_flat, ys, x1, h2, tw.T, g2, s_gate, s_up, s_down, mod_row)


def _rope_angles(rows, head_dim):
    axis_dim = head_dim // 2
    inv_freq = ROPE_BASE ** (-jnp.arange(0, axis_dim, 2, dtype=F32) / axis_dim)
    row = jnp.repeat(jnp.arange(rows, dtype=F32), GRID_W)
    col = jnp.tile(jnp.arange(GRID_W, dtype=F32), rows)
    return jnp.concatenate([row[:, None] * inv_freq, col[:, None] * inv_freq], axis=-1)


def _rope_tables(n, ctx_len):
    rows = n // GRID_W
    ang = _rope_angles(rows, RET_DIM)
    cos, sin = jnp.cos(ang), jnp.sin(ang)
    ret_cos = jnp.concatenate([cos, cos], axis=-1)
    ret_sin = jnp.concatenate([-sin, sin], axis=-1)
    ang = _rope_angles(rows, DIFF_QK)
    cos, sin = jnp.cos(ang), jnp.sin(ang)
    zero = jnp.zeros_like(sin)
    dc = jnp.tile(jnp.concatenate([cos, cos], axis=-1), (1, 2))
    ds1 = jnp.tile(jnp.concatenate([-sin, zero], axis=-1), (1, 2))
    ds2 = jnp.tile(jnp.concatenate([zero, sin], axis=-1), (1, 2))

    def with_ctx(tab, fill):
        return jnp.concatenate([jnp.full((ctx_len, tab.shape[1]), fill, F32), tab], axis=0)

    return (with_ctx(ret_cos, 1.0), with_ctx(ret_sin, 0.0),
            with_ctx(dc, 1.0), with_ctx(ds1, 0.0), with_ctx(ds2, 0.0))


def kernel(x, c, ctx, c_ctx, w_ada, b_ada, norm1_w, norm2_w, w_in, ret_decay_logit, ret_norm_w,
           diff_qnorm_w, diff_knorm_w, diff_lambda, diff_subln_w, s5_lambda_re, s5_lambda_im,
           s5_log_dt, s5_b_re, s5_b_im, s5_c_re, s5_c_im, s5_d, s5_w_glu, w_out, router_w,
           router_bias, exp_w_gate, exp_w_up, exp_w_down, shared_w_gate, shared_w_up, shared_w_down):
    out_dtype = x.dtype
    batch, n, d = x.shape
    ctx_len = ctx.shape[1]
    depth = w_ada.shape[0]
    s = ctx_len + n
    t = batch * s
    assert ctx_len % TOK_TILE == 0 and n % TOK_TILE == 0 and n % GRID_W == 0
    assert batch + 1 <= SUBLANES
    tiles_per_seq = s // TOK_TILE
    ctx_tiles = ctx_len // TOK_TILE

    def mod_row(i):
        return jnp.where(i % tiles_per_seq < ctx_tiles, batch, i // tiles_per_seq)

    z = jnp.concatenate([ctx.astype(F32), x.astype(F32)], axis=1).reshape(t, d)
    cond = jnp.zeros((SUBLANES, d), F32).at[:batch].set(c.astype(F32)).at[batch].set(c_ctx.astype(F32))
    mod = _adaln(cond, w_ada.astype(F32), b_ada.astype(F32))
    ret_cos, ret_sin, dc, ds1, ds2 = _rope_tables(n, ctx_len)

    for l in range(depth):
        lambda_init = 0.8 - 0.6 * math.exp(-0.3 * l)
        m = mod[l].reshape(SUBLANES, N_MOD, 1, d)
        sh1, sc1, g1, sh2, sc2, g2 = (m[:, j] for j in range(N_MOD))
        proj = _in_proj(z, norm1_w[l][None], sc1, sh1, w_in[l].astype(BF16), mod_row)
        proj3 = proj.reshape(batch, s, proj.shape[1])
        log_gamma = jax.nn.log_sigmoid(ret_decay_logit[l].astype(F32))
        ret = _retention(proj3, log_gamma, ret_cos, ret_sin, ret_norm_w[l][None], ctx_len)
        q0t, q1t, kk, vt = _diff_prep(proj, diff_qnorm_w[l], diff_knorm_w[l], dc, ds1, ds2,
                                      batch, s)
        dif = _diff_attention(q0t, q1t, kk, vt, diff_lambda[l], diff_subln_w[l], ctx_len,
                              lambda_init)
        s5 = _s5_mixer(proj3, s5_lambda_re[l], s5_lambda_im[l], s5_log_dt[l], s5_b_re[l],
                       s5_b_im[l], s5_c_re[l], s5_c_im[l], s5_d[l], s5_w_glu[l], ctx_len)
        x1, h2, h2p, te, tw, rk, cnt = _out_proj_route(
            z, ret.reshape(t, RET_W), dif.reshape(t, DIFF_W), s5.reshape(t, S5_CH),
            w_out[l].astype(BF16), g1, norm2_w[l][None], sc2, sh2, router_w[l], router_bias[l],
            mod_row)
        z = _moe(x1, h2, h2p, te, tw, rk, cnt, g2, exp_w_gate.astype(F32),
                 exp_w_up.astype(F32), exp_w_down.astype(F32), l,
                 shared_w_gate[l].astype(BF16), shared_w_up[l].astype(BF16),
                 shared_w_down[l].astype(BF16), mod_row)
    return z.reshape(batch, s, d)[:, ctx_len:].astype(out_dtype)
```

```python
import functools
import math

import jax
import jax.numpy as jnp
from jax import lax
from jax.experimental import pallas as pl
from jax.experimental.pallas import tpu as pltpu

F32 = jnp.float32
BF16 = jnp.bfloat16
U32 = jnp.uint32
I32 = jnp.int32

GRID_W = 64
N_MOD = 6
EPS = 1e-6
ROPE_BASE = 10000.0
RET_HEADS = 4
RET_DIM = 128
RET_W = RET_HEADS * RET_DIM
DIFF_HEADS = 8
DIFF_QK = 64
DIFF_V = 2 * DIFF_QK
DIFF_QK_W = DIFF_HEADS * 2 * DIFF_QK
DIFF_W = DIFF_HEADS * DIFF_V
S5_GROUP = 16
S5_CH = 512
S5_GROUPS = S5_CH // S5_GROUP
S5_STATE = 64
S5_W = S5_GROUPS * S5_STATE
N_EXPERTS = 64
EXPERT_FF = 512
TOP_K = 8
N_EXPERT_GROUPS = 8
TOPK_GROUPS = 4
ROUTE_SCALE = 2.5

V7X_VMEM_BYTES = 64 * 1024 * 1024
LANES = 128
SUBLANES = 8
BF16_SUBLANES = 16
TOK_TILE = 256
RET_CHUNK = 256
ATT_TQ = 256
ATT_TK = 256
ATT_HEADS = 2
ATT_UNROLL_PAIRS = 4
LOG2E = math.log2(math.e)
S5_BLOCK = 256
EXP_TILE = 256
ADA_TN = 1024
ROW_DMA_UNROLL = 4


def _cparams(n_axes, vmem_mb=48):
    return pltpu.CompilerParams(
        dimension_semantics=("arbitrary",) * n_axes,
        vmem_limit_bytes=vmem_mb * 1024 * 1024,
    )


def _dot(a, b):
    return jnp.dot(a, b, preferred_element_type=F32)


def _dot_nt(a, b):
    return lax.dot_general(a, b, (((1,), (1,)), ((), ())), preferred_element_type=F32)


def _dot_tn(a, b):
    return lax.dot_general(a, b, (((0,), (0,)), ((), ())), preferred_element_type=F32)


def _split(x):
    hi = x.astype(BF16)
    lo = (x - hi.astype(F32)).astype(BF16)
    return hi, lo


def _dot3(a, b):
    ah, al = _split(a)
    bh, bl = _split(b)
    return _dot(ah, bh) + _dot(al, bh) + _dot(ah, bl)


def _dot3_nt(a, b):
    ah, al = _split(a)
    bh, bl = _split(b)
    return _dot_nt(ah, bh) + _dot_nt(al, bh) + _dot_nt(ah, bl)


def _silu(x):
    return x * jax.nn.sigmoid(x)


def _pack_halves(lo, hi):
    lo_b = pltpu.bitcast(lo.astype(BF16).astype(F32), U32) >> 16
    hi_b = pltpu.bitcast(hi.astype(BF16).astype(F32), U32) & jnp.uint32(0xFFFF0000)
    return hi_b | lo_b


def _unpack_halves(p):
    lo = pltpu.bitcast(p << 16, F32)
    hi = pltpu.bitcast(p & jnp.uint32(0xFFFF0000), F32)
    return lo, hi


def _ada_kernel(cond_ref, w_ref, b_ref, o_ref):
    o_ref[...] = _dot3(_silu(cond_ref[...]), w_ref[...]) + b_ref[...]


def _adaln(cond, w_ada, b_ada):
    n_layers, d, n = w_ada.shape
    rows = cond.shape[0]
    return pl.pallas_call(
        _ada_kernel,
        out_shape=jax.ShapeDtypeStruct((n_layers, rows, n), F32),
        grid=(n_layers, n // ADA_TN),
        in_specs=[
            pl.BlockSpec((rows, d), lambda l, j: (0, 0)),
            pl.BlockSpec((None, d, ADA_TN), lambda l, j: (l, 0, j)),
            pl.BlockSpec((None, 1, ADA_TN), lambda l, j: (l, 0, j)),
        ],
        out_specs=pl.BlockSpec((None, rows, ADA_TN), lambda l, j: (l, 0, j)),
        compiler_params=_cparams(2),
        name="adaln",
    )(cond, w_ada, b_ada.reshape(n_layers, 1, n))


def _proj_kernel(x_ref, nw_ref, sc_ref, sh_ref, w_ref, o_ref):
    x = x_ref[...]
    ms = jnp.mean(x * x, axis=-1, keepdims=True)
    h = x * lax.rsqrt(ms + EPS) * nw_ref[...]
    h = h * (1.0 + sc_ref[...]) + sh_ref[...]
    o_ref[...] = _dot(h.astype(BF16), w_ref[...]).astype(BF16)


def _in_proj(z, norm_w, sc, sh, w_in, mod_row):
    t, d = z.shape
    n = w_in.shape[1]
    mod_spec = pl.BlockSpec((None, 1, d), lambda i: (mod_row(i), 0, 0))
    return pl.pallas_call(
        _proj_kernel,
        out_shape=jax.ShapeDtypeStruct((t, n), BF16),
        grid=(t // TOK_TILE,),
        in_specs=[
            pl.BlockSpec((TOK_TILE, d), lambda i: (i, 0)),
            pl.BlockSpec((1, d), lambda i: (0, 0)),
            mod_spec,
            mod_spec,
            pl.BlockSpec((d, n), lambda i: (0, 0), pipeline_mode=pl.Buffered(1)),
        ],
        out_specs=pl.BlockSpec((TOK_TILE, n), lambda i: (i, 0)),
        compiler_params=_cparams(1, 56),
        name="in_proj",
    )(z, norm_w, sc, sh, w_in)


def _ret_kernel(lg_ref, q_ref, k_ref, v_ref, cos_ref, sin_ref, *rest, reverse, chunk):
    if reverse:
        of_ref, g_ref, nw_ref, o_ref, state_ref = rest
    else:
        o_ref, state_ref = rest
    t = pl.program_id(1)

    @pl.when(t == 0)
    def _():
        state_ref[...] = jnp.zeros_like(state_ref)

    ii = lax.broadcasted_iota(I32, (chunk, chunk), 0)
    jj = lax.broadcasted_iota(I32, (chunk, chunk), 1)
    rel = ((jj - ii) if reverse else (ii - jj)).astype(F32)
    idx = lax.broadcasted_iota(I32, (chunk, 1), 0).astype(F32)
    cos2 = cos_ref[...]
    sin2 = sin_ref[...]
    scale = RET_DIM ** -0.5
    direction = 1 if reverse else 0

    def rope(x):
        return x * cos2 + pltpu.roll(x, RET_DIM // 2, 1) * sin2

    for h in range(RET_HEADS):
        sl = slice(h * RET_DIM, (h + 1) * RET_DIM)
        lg = lg_ref[direction, h]
        dmask = jnp.where(rel >= 0.0, jnp.exp(lg * jnp.maximum(rel, 0.0)), 0.0)
        if reverse:
            qdec = jnp.exp(lg * (chunk - idx))
            kdec = jnp.exp(lg * idx)
        else:
            qdec = jnp.exp(lg * (idx + 1.0))
            kdec = jnp.exp(lg * (chunk - 1.0 - idx))
        cdec = jnp.exp(jnp.full((1, 1), lg * chunk, F32))
        q = rope(q_ref[:, sl].astype(F32))
        k = rope(k_ref[:, sl].astype(F32)) * scale
        v = v_ref[:, sl]
        qb = q.astype(BF16)
        scores = _dot_nt(qb, k.astype(BF16)) * dmask
        inner = _dot(scores.astype(BF16), v)
        s_old = state_ref[h]
        cross = _dot(qb, s_old.astype(BF16)) * qdec
        state_ref[h] = s_old * cdec + _dot_tn((k * kdec).astype(BF16), v)
        o = inner + cross
        if reverse:
            o = o + of_ref[:, sl]
            mu = jnp.mean(o, axis=-1, keepdims=True)
            oc = o - mu
            var = jnp.mean(oc * oc, axis=-1, keepdims=True)
            on = oc * lax.rsqrt(var + EPS) * nw_ref[:, sl]
            o_ref[:, sl] = (_silu(g_ref[:, sl].astype(F32)) * on).astype(BF16)
        else:
            o_ref[:, sl] = o


def _retention(proj3, log_gamma, cos2, sin2, norm_w, ctx_len):
    b, s, _ = proj3.shape
    c = RET_CHUNK
    nch = s // c
    nctx = ctx_len // c

    def fwd_chunk(t):
        return t

    def bwd_chunk(t):
        return jnp.where(t < nctx, nctx - 1 - t, nch - 1 + nctx - t)

    def specs(chunk_of):
        col = lambda j: pl.BlockSpec((None, c, RET_W), lambda bi, t, j=j: (bi, chunk_of(t), j))
        tab = pl.BlockSpec((c, RET_DIM), lambda bi, t: (chunk_of(t), 0))
        return col, tab

    smem = pl.BlockSpec(memory_space=pltpu.SMEM)
    scratch = [pltpu.VMEM((RET_HEADS, RET_DIM, RET_DIM), F32)]

    col, tab = specs(fwd_chunk)
    o_f = pl.pallas_call(
        functools.partial(_ret_kernel, reverse=False, chunk=c),
        out_shape=jax.ShapeDtypeStruct((b, s, RET_W), F32),
        grid=(b, nch),
        in_specs=[smem, col(0), col(1), col(2), tab, tab],
        out_specs=col(0),
        scratch_shapes=scratch,
        compiler_params=_cparams(2),
        name="retention_fwd",
    )(log_gamma, proj3, proj3, proj3, cos2, sin2)

    col, tab = specs(bwd_chunk)
    return pl.pallas_call(
        functools.partial(_ret_kernel, reverse=True, chunk=c),
        out_shape=jax.ShapeDtypeStruct((b, s, RET_W), BF16),
        grid=(b, nch),
        in_specs=[smem, col(0), col(1), col(2), tab, tab, col(0), col(3),
                  pl.BlockSpec((1, RET_W), lambda bi, t: (0, 0))],
        out_specs=col(0),
        scratch_shapes=scratch,
        compiler_params=_cparams(2),
        name="retention_bwd",
    )(log_gamma, proj3, proj3, proj3, cos2, sin2, o_f, proj3, norm_w)


def _dprep_kernel(q_ref, k_ref, v_ref, qw_ref, kw_ref, c_ref, s1_ref, s2_ref, bd_ref,
                  q0t_ref, q1t_ref, kk_ref, vt_ref):
    cc = c_ref[...]
    s1 = s1_ref[...]
    s2 = s2_ref[...]
    bd = bd_ref[...]
    rows = q_ref.shape[0]
    first = lax.broadcasted_iota(I32, (LANES, rows), 0) < DIFF_QK
    quarter = DIFF_QK // 2

    def prep(x, w):
        x = x.astype(F32)
        hi, lo = _split(x * x)
        ss = _dot(hi, bd) + _dot(lo, bd)
        xn = x * lax.rsqrt(ss * (1.0 / DIFF_QK) + EPS) * w
        return xn * cc + pltpu.roll(xn, LANES - quarter, 1) * s1 + pltpu.roll(xn, quarter, 1) * s2

    for h in range(DIFF_HEADS):
        sl = slice(h * LANES, (h + 1) * LANES)
        qt = (prep(q_ref[:, sl], qw_ref[...]) * (DIFF_QK ** -0.5 * LOG2E)).T
        q0t_ref[sl, :] = jnp.where(first, qt, 0.0).astype(BF16)
        q1t_ref[sl, :] = jnp.where(first, 0.0, qt).astype(BF16)
        kk_ref[:, sl] = prep(k_ref[:, sl], kw_ref[...]).astype(BF16)
        vt_ref[sl, :] = v_ref[:, sl].astype(F32).T.astype(BF16)


def _diff_prep(proj, qnorm_w, knorm_w, dc, ds1, ds2, batch, seq):
    t, _ = proj.shape
    tiles_per_seq = seq // TOK_TILE
    qcol = (4 * RET_W) // DIFF_QK_W
    blockdiag = jnp.kron(jnp.eye(2, dtype=F32), jnp.ones((DIFF_QK, DIFF_QK), F32)).astype(BF16)
    tab = pl.BlockSpec((TOK_TILE, LANES), lambda i: (i % tiles_per_seq, 0))
    row = pl.BlockSpec((1, LANES), lambda i: (0, 0))
    col = lambda j: pl.BlockSpec((TOK_TILE, DIFF_QK_W), lambda i, j=j: (i, j))
    tr = pl.BlockSpec((None, DIFF_QK_W, TOK_TILE),
                      lambda i: (i // tiles_per_seq, 0, i % tiles_per_seq))
    shp_t = jax.ShapeDtypeStruct((batch, DIFF_QK_W, seq), BF16)
    return pl.pallas_call(
        _dprep_kernel,
        out_shape=(shp_t, shp_t, jax.ShapeDtypeStruct((t, DIFF_QK_W), BF16), shp_t),
        grid=(t // TOK_TILE,),
        in_specs=[col(qcol), col(qcol + 1), col(qcol + 2), row, row, tab, tab, tab,
                  pl.BlockSpec((LANES, LANES), lambda i: (0, 0))],
        out_specs=(tr, tr, col(0), tr),
        compiler_params=_cparams(1),
        name="diff_prep",
    )(proj, proj, proj, jnp.tile(qnorm_w, 2)[None], jnp.tile(knorm_w, 2)[None], dc, ds1, ds2,
      blockdiag)


def _flash_kernel(lam_ref, q0t_ref, q1t_ref, k_ref, vt_ref, sw_ref, o_ref, m_s, acc_s, s_s, *,
                  heads, nctx_tiles, ctx_kv, all_kv, lambda_init):
    i = pl.program_id(2)
    nkv = jnp.where(i < nctx_tiles, ctx_kv, all_kv)
    m_s[...] = jnp.full(m_s.shape, -jnp.inf, F32)
    acc_s[...] = jnp.zeros(acc_s.shape, F32)
    ones = jnp.ones((BF16_SUBLANES, ATT_TK), BF16)

    def scores(j, buf):
        off = pl.multiple_of(j * ATT_TK, ATT_TK)
        for h in range(heads):
            sl = slice(h * DIFF_V, (h + 1) * DIFF_V)
            kt = k_ref[pl.ds(off, ATT_TK), sl]
            for comp, qt_ref in enumerate((q0t_ref, q1t_ref)):
                s_s[buf, 2 * h + comp] = _dot(kt, qt_ref[sl, :])

    def softmax_pv(j, buf):
        off = pl.multiple_of(j * ATT_TK, ATT_TK)
        for h in range(heads):
            sl = slice(h * DIFF_V, (h + 1) * DIFF_V)
            vt = jnp.concatenate([vt_ref[sl, pl.ds(off, ATT_TK)], ones], axis=0)
            for comp in range(2):
                c = 2 * h + comp
                st = s_s[buf, c]
                m_old = m_s[c]
                m_new = jnp.maximum(m_old, jnp.max(st, axis=0, keepdims=True))
                alpha = jnp.exp2(m_old - m_new)
                pt = jnp.exp2(st - m_new)
                acc_s[c] = alpha * acc_s[c] + _dot(vt, pt.astype(BF16))
                m_s[c] = m_new

    scores(0, 0)

    def pairs(first, n_pairs):
        for p in range(n_pairs):
            scores(first + 2 * p + 1, 1)
            softmax_pv(first + 2 * p, 0)
            scores(first + 2 * p + 2, 0)
            softmax_pv(first + 2 * p + 1, 1)

    group = 2 * ATT_UNROLL_PAIRS
    n_groups = (nkv - 1) // group

    def group_body(g, carry):
        pairs(group * g, ATT_UNROLL_PAIRS)
        return carry

    def pair_body(p, carry):
        pairs(group * n_groups + 2 * p, 1)
        return carry

    lax.fori_loop(0, n_groups, group_body, 0)
    lax.fori_loop(0, ((nkv - 1) % group) // 2, pair_body, 0)
    softmax_pv(nkv - 1, 0)
    lv = lam_ref[...]
    lam = (jnp.exp(jnp.sum(lv[0:1] * lv[1:2], axis=-1, keepdims=True))
           - jnp.exp(jnp.sum(lv[2:3] * lv[3:4], axis=-1, keepdims=True)) + lambda_init)
    for h in range(heads):
        sl = slice(h * DIFF_V, (h + 1) * DIFF_V)
        o0 = acc_s[2 * h, 0:DIFF_V] / acc_s[2 * h, DIFF_V:DIFF_V + 1]
        o1 = acc_s[2 * h + 1, 0:DIFF_V] / acc_s[2 * h + 1, DIFF_V:DIFF_V + 1]
        d = (o0 - lam * o1).T
        ms = jnp.mean(d * d, axis=-1, keepdims=True)
        o_ref[:, sl] = (d * lax.rsqrt(ms + EPS) * sw_ref[...] * (1.0 - lambda_init)).astype(BF16)


def _diff_attention(q0t, q1t, kk, vt, lam_vecs, subln_w, ctx_len, lambda_init):
    b, _, s = q0t.shape
    kk = kk.reshape(b, s, DIFF_QK_W)
    hb = ATT_HEADS
    w = hb * DIFF_V
    assert (s // ATT_TK) % 2 == 1 and (ctx_len // ATT_TK) % 2 == 1
    qspec = pl.BlockSpec((None, w, ATT_TQ), lambda bi, h, i: (bi, h, i))
    return pl.pallas_call(
        functools.partial(_flash_kernel, heads=hb, nctx_tiles=ctx_len // ATT_TQ,
                          ctx_kv=ctx_len // ATT_TK, all_kv=s // ATT_TK, lambda_init=lambda_init),
        out_shape=jax.ShapeDtypeStruct((b, s, DIFF_W), BF16),
        grid=(b, DIFF_HEADS // hb, s // ATT_TQ),
        in_specs=[
            pl.BlockSpec((4, DIFF_QK), lambda bi, h, i: (0, 0)),
            qspec, qspec,
            pl.BlockSpec((None, s, w), lambda bi, h, i: (bi, 0, h)),
            pl.BlockSpec((None, w, s), lambda bi, h, i: (bi, h, 0)),
            pl.BlockSpec((1, DIFF_V), lambda bi, h, i: (0, 0)),
        ],
        out_specs=pl.BlockSpec((None, ATT_TQ, w), lambda bi, h, i: (bi, i, h)),
        scratch_shapes=[pltpu.VMEM((2 * hb, 1, ATT_TQ), F32),
                        pltpu.VMEM((2 * hb, DIFF_V + BF16_SUBLANES, ATT_TQ), F32),
                        pltpu.VMEM((2, 2 * hb, ATT_TK, ATT_TQ), F32)],
        compiler_params=_cparams(3),
        name="diff_flash",
    )(lam_vecs, q0t, q1t, kk, vt, subln_w[None])


def _s5_kernel(u_ref, br_ref, bi_ref, cr_ref, ci_ref, tab_ref, *rest, reverse, ntiles):
    if reverse:
        yf_ref, d_ref, wg_ref, o_ref, xr_s, xi_s, car_s, cai_s = rest
    else:
        o_ref, xr_s, xi_s, car_s, cai_s = rest
    t = pl.program_id(1)

    @pl.when(t == 0)
    def _():
        car_s[...] = jnp.zeros_like(car_s)
        cai_s[...] = jnp.zeros_like(cai_s)

    u = u_ref[...]
    nq = S5_CH // LANES
    sw = S5_W // nq
    for q in range(nq):
        uq = u[:, q * LANES:(q + 1) * LANES]
        xr_s[:, q * sw:(q + 1) * sw] = _dot(uq, br_ref[q])
        xi_s[:, q * sw:(q + 1) * sw] = _dot(uq, bi_ref[q])

    def body(n, carry):
        cr, ci = carry
        tile = (ntiles - 1 - n) if reverse else n
        off = pl.multiple_of(tile * SUBLANES, SUBLANES)
        xr = xr_s[pl.ds(off, SUBLANES), :]
        xi = xi_s[pl.ds(off, SUBLANES), :]
        for lvl, k in enumerate((1, 2, 4)):
            pr = tab_ref[2 * lvl]
            pi = tab_ref[2 * lvl + 1]
            shift = (SUBLANES - k) if reverse else k
            sr = pltpu.roll(xr, shift, 0)
            si = pltpu.roll(xi, shift, 0)
            xr, xi = xr + pr * sr - pi * si, xi + pr * si + pi * sr
        ar = tab_ref[6]
        ai = tab_ref[7]
        xr, xi = xr + ar * cr - ai * ci, xi + ar * ci + ai * cr
        xr_s[pl.ds(off, SUBLANES), :] = xr
        xi_s[pl.ds(off, SUBLANES), :] = xi
        last = 0 if reverse else SUBLANES - 1
        return xr[last:last + 1], xi[last:last + 1]

    cr, ci = lax.fori_loop(0, ntiles, body, (car_s[...], cai_s[...]))
    car_s[...] = cr
    cai_s[...] = ci
    y = jnp.concatenate(
        [_dot(xr_s[:, q * sw:(q + 1) * sw].astype(BF16), cr_ref[q])
         + _dot(xi_s[:, q * sw:(q + 1) * sw].astype(BF16), ci_ref[q]) for q in range(nq)], axis=1)
    if reverse:
        y = y + yf_ref[...] + d_ref[...] * u.astype(F32)
        y = jax.nn.gelu(y, approximate=True)
        o_ref[...] = (y * jax.nn.sigmoid(_dot(y.astype(BF16), wg_ref[...]))).astype(BF16)
    else:
        o_ref[...] = y


def _s5_tables(a_re_log, a_im_ang, reverse):
    w = a_re_log.reshape(1, S5_W)
    th = a_im_ang.reshape(1, S5_W)
    row = jnp.arange(SUBLANES, dtype=F32)[:, None]
    tabs = []
    for k in (1, 2, 4):
        keep = (row <= SUBLANES - 1 - k) if reverse else (row >= k)
        mag = jnp.exp(w * k)
        tabs.append(jnp.where(keep, mag * jnp.cos(th * k), 0.0))
        tabs.append(jnp.where(keep, mag * jnp.sin(th * k), 0.0))
    e = (SUBLANES - row) if reverse else (row + 1.0)
    mag = jnp.exp(w * e)
    tabs.append(mag * jnp.cos(th * e))
    tabs.append(mag * jnp.sin(th * e))
    return jnp.stack(tabs)


def _s5_params(lam_re, lam_im, log_dt, b_re, b_im):
    dt = jnp.exp(log_dt)[:, None]
    wlog = lam_re * dt
    ang = lam_im * dt
    mag = jnp.exp(wlog)
    a_re = mag * jnp.cos(ang)
    a_im = mag * jnp.sin(ang)
    den = lam_re * lam_re + lam_im * lam_im
    nr = a_re - 1.0
    f_re = (nr * lam_re + a_im * lam_im) / den
    f_im = (a_im * lam_re - nr * lam_im) / den
    bb_re = f_re[..., None] * b_re - f_im[..., None] * b_im
    bb_im = f_re[..., None] * b_im + f_im[..., None] * b_re
    eye = jnp.eye(S5_GROUPS, dtype=F32)
    bmat_re = jnp.einsum('gph,gk->ghkp', bb_re, eye).reshape(S5_CH, S5_W)
    bmat_im = jnp.einsum('gph,gk->ghkp', bb_im, eye).reshape(S5_CH, S5_W)
    return wlog, ang, _diag_blocks(bmat_re).astype(BF16), _diag_blocks(bmat_im).astype(BF16)


def _diag_blocks(m):
    nq = S5_CH // LANES
    r, c = m.shape[0] // nq, m.shape[1] // nq
    return jnp.stack([m[q * r:(q + 1) * r, q * c:(q + 1) * c] for q in range(nq)])


def _s5_mixer(proj3, lam_re, lam_im, log_dt, b_re, b_im, c_re, c_im, d, w_glu, ctx_len):
    b, s, _ = proj3.shape
    tb = S5_BLOCK
    nb = s // tb
    nctx = ctx_len // tb
    ucol = (4 * RET_W + 2 * DIFF_QK_W + DIFF_W) // S5_CH
    eye = jnp.eye(S5_GROUPS, dtype=F32)
    cmat_re = _diag_blocks(jnp.einsum('ghp,gk->gpkh', c_re, eye).reshape(S5_W, S5_CH)).astype(BF16)
    cmat_im = _diag_blocks(-jnp.einsum('ghp,gk->gpkh', c_im, eye).reshape(S5_W, S5_CH)).astype(BF16)
    full = lambda shape: pl.BlockSpec(shape, lambda bi, t: tuple(0 for _ in shape))
    scratch = [pltpu.VMEM((tb, S5_W), F32), pltpu.VMEM((tb, S5_W), F32),
               pltpu.VMEM((1, S5_W), F32), pltpu.VMEM((1, S5_W), F32)]
    nq = S5_CH // LANES
    weights = [full((nq, LANES, S5_W // nq)), full((nq, LANES, S5_W // nq)),
               full((nq, S5_W // nq, LANES)), full((nq, S5_W // nq, LANES)),
               full((8, SUBLANES, S5_W))]

    def blk_f(t):
        return t

    def blk_b(t):
        return jnp.where(t < nctx, nctx - 1 - t, nb - 1 + nctx - t)

    wl, ang, bre, bim = _s5_params(lam_re[0], lam_im[0], log_dt[0], b_re, b_im)
    y_f = pl.pallas_call(
        functools.partial(_s5_kernel, reverse=False, ntiles=tb // SUBLANES),
        out_shape=jax.ShapeDtypeStruct((b, s, S5_CH), F32),
        grid=(b, nb),
        in_specs=[pl.BlockSpec((None, tb, S5_CH), lambda bi, t: (bi, blk_f(t), ucol))] + weights,
        out_specs=pl.BlockSpec((None, tb, S5_CH), lambda bi, t: (bi, blk_f(t), 0)),
        scratch_shapes=scratch,
        compiler_params=_cparams(2),
        name="s5_fwd",
    )(proj3, bre, bim, cmat_re, cmat_im, _s5_tables(wl, ang, False))

    wl, ang, bre, bim = _s5_params(lam_re[1], lam_im[1], log_dt[1], b_re, b_im)
    return pl.pallas_call(
        functools.partial(_s5_kernel, reverse=True, ntiles=tb // SUBLANES),
        out_shape=jax.ShapeDtypeStruct((b, s, S5_CH), BF16),
        grid=(b, nb),
        in_specs=[pl.BlockSpec((None, tb, S5_CH), lambda bi, t: (bi, blk_b(t), ucol))] + weights + [
            pl.BlockSpec((None, tb, S5_CH), lambda bi, t: (bi, blk_b(t), 0)),
            full((1, S5_CH)), full((S5_CH, S5_CH))],
        out_specs=pl.BlockSpec((None, tb, S5_CH), lambda bi, t: (bi, blk_b(t), 0)),
        scratch_shapes=scratch,
        compiler_params=_cparams(2),
        name="s5_bwd",
    )(proj3, bre, bim, cmat_re, cmat_im, _s5_tables(wl, ang, True), y_f, d[None],
      w_glu.astype(BF16))


def _neg_inf_like(x):
    return jnp.full(x.shape, -jnp.inf, x.dtype)


def _out_kernel(x_ref, ret_ref, dif_ref, s5_ref, wo_ref, g1_ref, nw_ref, sc_ref, sh_ref,
                rw_ref, rb_ref, tri_ref,
                x1_ref, h2_ref, h2p_ref, te_ref, tw_ref, rk_ref, cnt_ref, cnt_s):
    i = pl.program_id(0)

    @pl.when(i == 0)
    def _():
        cnt_s[...] = jnp.zeros_like(cnt_s)

    d = x_ref.shape[1]
    tm = x_ref.shape[0]
    mix = (_dot(ret_ref[...], wo_ref[0:RET_W, :])
           + _dot(dif_ref[...], wo_ref[RET_W:RET_W + DIFF_W, :])
           + _dot(s5_ref[...], wo_ref[RET_W + DIFF_W:, :]))
    x1 = x_ref[...] + g1_ref[...] * mix
    x1_ref[...] = x1
    ms = jnp.mean(x1 * x1, axis=-1, keepdims=True)
    h2 = x1 * lax.rsqrt(ms + EPS) * nw_ref[...]
    h2 = h2 * (1.0 + sc_ref[...]) + sh_ref[...]
    h2_ref[...] = h2.astype(BF16)
    h2p_ref[...] = _pack_halves(h2[:, :d // 2], h2[:, d // 2:])

    ng = N_EXPERT_GROUPS
    pg = N_EXPERTS // N_EXPERT_GROUPS
    logits = _dot3_nt(rw_ref[...], h2)
    scores = jax.nn.sigmoid(logits)
    sel3 = (scores + rb_ref[...]).reshape(ng, pg, tm)
    scores3 = scores.reshape(ng, pg, tm)
    in_grp = lax.broadcasted_iota(I32, (ng, pg, tm), 1).astype(F32)
    grp = lax.broadcasted_iota(I32, (ng, pg, tm), 0).astype(F32)
    eidx = grp * pg + in_grp
    gidx = lax.broadcasted_iota(I32, (ng, 1, tm), 0).astype(F32)

    m1 = jnp.max(sel3, axis=1, keepdims=True)
    first = jnp.min(jnp.where(sel3 == m1, in_grp, float(pg)), axis=1, keepdims=True)
    m2 = jnp.max(jnp.where(in_grp == first, -jnp.inf, sel3), axis=1, keepdims=True)
    rem = m1 + m2
    gsel = jnp.zeros((ng, 1, tm), F32)
    for _ in range(TOPK_GROUPS):
        mx = jnp.max(rem, axis=0, keepdims=True)
        fi = jnp.min(jnp.where(rem == mx, gidx, float(ng)), axis=0, keepdims=True)
        pick = gidx == fi
        gsel = jnp.where(pick, 1.0, gsel)
        rem = jnp.where(pick, -jnp.inf, rem)
    masked = jnp.where(gsel > 0.0, sel3, -jnp.inf)

    def red2(fn, x):
        return fn(fn(x, axis=1, keepdims=True), axis=0, keepdims=True)

    chosen = jnp.zeros((ng, pg, tm), F32)
    picks = []
    weights = []
    for _ in range(TOP_K):
        mx = red2(jnp.max, masked)
        fi = red2(jnp.min, jnp.where(masked == mx, eidx, float(N_EXPERTS)))
        pick = eidx == fi
        picks.append(fi)
        weights.append(red2(jnp.sum, jnp.where(pick, scores3, 0.0)))
        chosen = jnp.where(pick, 1.0, chosen)
        masked = jnp.where(pick, -jnp.inf, masked)
    wsum = weights[0]
    for w in weights[1:]:
        wsum = wsum + w
    inv = ROUTE_SCALE / wsum

    chosen2 = chosen.reshape(N_EXPERTS, tm)
    cum = _dot(chosen2.astype(BF16), tri_ref[...]) + cnt_s[...]
    cum3 = cum.reshape(ng, pg, tm)
    for k in range(TOP_K):
        te_ref[k:k + 1, :] = picks[k].reshape(1, tm).astype(I32)
        tw_ref[k:k + 1, :] = (weights[k] * inv).reshape(1, tm)
        rk = red2(jnp.sum, jnp.where(eidx == picks[k], cum3, 0.0))
        rk_ref[k:k + 1, :] = rk.reshape(1, tm).astype(I32)
    cnt_new = cnt_s[...] + jnp.sum(chosen2, axis=-1, keepdims=True)
    cnt_s[...] = cnt_new
    cnt_ref[...] = jnp.broadcast_to(cnt_new, cnt_ref.shape)


def _out_proj_route(z, ret, dif, s5, w_out, g1, norm_w, sc, sh, router_w, router_bias, mod_row):
    t, d = z.shape
    tm = TOK_TILE
    tri = (jnp.arange(tm)[:, None] < jnp.arange(tm)[None, :]).astype(BF16)
    mod_spec = pl.BlockSpec((None, 1, d), lambda i: (mod_row(i), 0, 0))
    tok = lambda w: pl.BlockSpec((tm, w), lambda i: (i, 0))
    const = lambda shape: pl.BlockSpec(shape, lambda i: tuple(0 for _ in shape))
    lane_out = pl.BlockSpec((TOP_K, tm), lambda i: (0, i))
    return pl.pallas_call(
        _out_kernel,
        out_shape=(
            jax.ShapeDtypeStruct((t, d), F32),
            jax.ShapeDtypeStruct((t, d), BF16),
            jax.ShapeDtypeStruct((t, d // 2), U32),
            jax.ShapeDtypeStruct((TOP_K, t), I32),
            jax.ShapeDtypeStruct((TOP_K, t), F32),
            jax.ShapeDtypeStruct((TOP_K, t), I32),
            jax.ShapeDtypeStruct((N_EXPERTS, LANES), F32),
        ),
        grid=(t // tm,),
        in_specs=[tok(d), tok(RET_W), tok(DIFF_W), tok(S5_CH), const((d, d)),
                  mod_spec, const((1, d)), mod_spec, mod_spec,
                  const((N_EXPERTS, d)), const((N_EXPERTS, 1)), const((tm, tm))],
        out_specs=(tok(d), tok(d), tok(d // 2), lane_out, lane_out, lane_out,
                   const((N_EXPERTS, LANES))),
        scratch_shapes=[pltpu.VMEM((N_EXPERTS, 1), F32)],
        compiler_params=_cparams(1),
        name="out_proj_route",
    )(z, ret, dif, s5, w_out, g1, norm_w, sc, sh, router_w.T, router_bias[:, None], tri)


def _dispatch_kernel(pos_hbm, x_ref, xs_in, xs_out, pos_s, sem, psem):
    del xs_in
    i = pl.program_id(0)
    tm = x_ref.shape[0]
    n = tm * TOP_K
    cp = pltpu.make_async_copy(pos_hbm.at[pl.ds(pl.multiple_of(i * n, n), n)], pos_s, psem)
    cp.start()
    cp.wait()

    def row_copy(r, p):
        return pltpu.make_async_copy(x_ref.at[pl.ds(r, 1), :], xs_out.at[pl.ds(p, 1), :], sem)

    def issue(r, c):
        for k in range(TOP_K):
            row_copy(r, pos_s[r * TOP_K + k]).start(priority=k % 2)
        return c

    lax.fori_loop(0, tm, issue, 0, unroll=ROW_DMA_UNROLL)

    def drain(r, c):
        for k in range(TOP_K):
            row_copy(0, 0).wait()
        return c

    lax.fori_loop(0, tm, drain, 0)


def _dispatch(h2p, pos_flat, n_rows):
    t, w = h2p.shape
    tm = TOK_TILE
    xs0 = jnp.zeros((n_rows, w), U32)
    return pl.pallas_call(
        _dispatch_kernel,
        out_shape=jax.ShapeDtypeStruct((n_rows, w), U32),
        grid=(t // tm,),
        in_specs=[pl.BlockSpec(memory_space=pl.ANY),
                  pl.BlockSpec((tm, w), lambda i: (i, 0)),
                  pl.BlockSpec(memory_space=pl.ANY)],
        out_specs=pl.BlockSpec(memory_space=pl.ANY),
        scratch_shapes=[pltpu.SMEM((tm * TOP_K,), I32), pltpu.SemaphoreType.DMA(()),
                        pltpu.SemaphoreType.DMA(())],
        input_output_aliases={2: 0},
        compiler_params=_cparams(1),
        name="moe_dispatch",
    )(pos_flat, h2p, xs0)


def _expert_kernel(te_ref, tv_ref, xs_ref, wg_ref, wu_ref, wd_ref, ys_ref, wg_s, wu_s, wd_s):
    i = pl.program_id(0)
    half = xs_ref.shape[1]

    @pl.when((i == 0) | (te_ref[i] != te_ref[jnp.maximum(i - 1, 0)]))
    def _():
        wg_s[...] = wg_ref[...].astype(BF16)
        wu_s[...] = wu_ref[...].astype(BF16)
        wd_s[...] = wd_ref[...].astype(BF16)

    @pl.when(tv_ref[i] != 0)
    def _():
        lo, hi = _unpack_halves(xs_ref[...])
        lo = lo.astype(BF16)
        hi = hi.astype(BF16)
        g = _dot(lo, wg_s[0:half, :]) + _dot(hi, wg_s[half:, :])
        u = _dot(lo, wu_s[0:half, :]) + _dot(hi, wu_s[half:, :])
        y = _dot((_silu(g) * u).astype(BF16), wd_s[...])
        ys_ref[...] = _pack_halves(y[:, :half], y[:, half:])

    @pl.when(tv_ref[i] == 0)
    def _():
        ys_ref[...] = jnp.zeros_like(ys_ref)


def _experts(xs, tile_expert, tile_valid, w_gate, w_up, w_down, layer):
    n_rows, half = xs.shape
    _, e, d, f = w_gate.shape
    tm = EXP_TILE
    grid_spec = pltpu.PrefetchScalarGridSpec(
        num_scalar_prefetch=2,
        grid=(n_rows // tm,),
        in_specs=[
            pl.BlockSpec((tm, half), lambda i, te, tv: (jnp.where(tv[i] != 0, i, 0), 0)),
            pl.BlockSpec((None, None, d, f), lambda i, te, tv: (layer, te[i], 0, 0)),
            pl.BlockSpec((None, None, d, f), lambda i, te, tv: (layer, te[i], 0, 0)),
            pl.BlockSpec((None, None, f, d), lambda i, te, tv: (layer, te[i], 0, 0)),
        ],
        out_specs=pl.BlockSpec((tm, half), lambda i, te, tv: (i, 0)),
        scratch_shapes=[pltpu.VMEM((d, f), BF16), pltpu.VMEM((d, f), BF16), pltpu.VMEM((f, d), BF16)],
    )
    return pl.pallas_call(
        _expert_kernel,
        out_shape=jax.ShapeDtypeStruct((n_rows, half), U32),
        grid_spec=grid_spec,
        compiler_params=_cparams(1, 56),
        name="moe_experts",
    )(tile_expert, tile_valid, xs, w_gate, w_up, w_down)


def _combine_kernel(pos_hbm, ys_hbm, x1_ref, h2_ref, w_ref, g2_ref, sg_ref, su_ref, sd_ref,
                    o_ref, pos_s, buf, sem, psem):
    i = pl.program_id(0)
    tm = x1_ref.shape[0]
    half = buf.shape[2]
    n = tm * TOP_K
    cp = pltpu.make_async_copy(pos_hbm.at[pl.ds(pl.multiple_of(i * n, n), n)], pos_s, psem)
    cp.start()
    cp.wait()

    def row_copy(r, k, p):
        return pltpu.make_async_copy(ys_hbm.at[pl.ds(p, 1), :], buf.at[k, pl.ds(r, 1), :], sem)

    def issue(r, c):
        for k in range(TOP_K):
            row_copy(r, k, pos_s[r * TOP_K + k]).start(priority=k % 2)
        return c

    lax.fori_loop(0, tm, issue, 0, unroll=ROW_DMA_UNROLL)

    def drain(r, c):
        for k in range(TOP_K):
            row_copy(0, k, 0).wait()
        return c

    lax.fori_loop(0, tm, drain, 0)

    h2 = h2_ref[...]
    hid = _silu(_dot(h2, sg_ref[...])) * _dot(h2, su_ref[...])
    shared = _dot(hid.astype(BF16), sd_ref[...])

    w = w_ref[...]
    acc_lo = None
    acc_hi = None
    for k in range(TOP_K):
        lo, hi = _unpack_halves(buf[k])
        wk = w[:, k:k + 1]
        acc_lo = wk * lo if acc_lo is None else acc_lo + wk * lo
        acc_hi = wk * hi if acc_hi is None else acc_hi + wk * hi
    g2 = g2_ref[...]
    o_ref[:, :half] = x1_ref[:, :half] + g2[:, :half] * (acc_lo + shared[:, :half])
    o_ref[:, half:] = x1_ref[:, half:] + g2[:, half:] * (acc_hi + shared[:, half:])


def _combine(pos_flat, ys, x1, h2, tw, g2, s_gate, s_up, s_down, mod_row):
    t, d = x1.shape
    tm = TOK_TILE
    f = s_gate.shape[1]
    tok = lambda w: pl.BlockSpec((tm, w), lambda i: (i, 0))
    const = lambda shape: pl.BlockSpec(shape, lambda i: tuple(0 for _ in shape))
    return pl.pallas_call(
        _combine_kernel,
        out_shape=jax.ShapeDtypeStruct((t, d), F32),
        grid=(t // tm,),
        in_specs=[pl.BlockSpec(memory_space=pl.ANY), pl.BlockSpec(memory_space=pl.ANY),
                  tok(d), tok(d), tok(TOP_K),
                  pl.BlockSpec((None, 1, d), lambda i: (mod_row(i), 0, 0)),
                  const((d, f)), const((d, f)), const((f, d))],
        out_specs=tok(d),
        scratch_shapes=[pltpu.SMEM((tm * TOP_K,), I32), pltpu.VMEM((TOP_K, tm, d // 2), U32),
                        pltpu.SemaphoreType.DMA(()), pltpu.SemaphoreType.DMA(())],
        compiler_params=_cparams(1),
        name="moe_combine",
    )(pos_flat, ys, x1, h2, tw, g2, s_gate, s_up, s_down)


def _moe(x1, h2, h2p, te, tw, rk, cnt, g2, w_gate, w_up, w_down, layer, s_gate, s_up, s_down,
         mod_row):
    t = x1.shape[0]
    tm = EXP_TILE
    n_tiles = (t * TOP_K) // tm + N_EXPERTS
    counts = cnt[:, 0].astype(I32)
    padded = ((counts + tm - 1) // tm) * tm
    pad_end = jnp.cumsum(padded)
    pad_off = pad_end - padded
    experts = jnp.arange(N_EXPERTS, dtype=I32)
    off_of = jnp.sum(jnp.where(te[:, :, None] == experts, pad_off, 0), axis=-1)
    pos_flat = (off_of + rk).T.reshape(-1)
    tile_start = jnp.arange(n_tiles, dtype=I32) * tm
    tile_expert = jnp.minimum(jnp.sum((tile_start[:, None] >= pad_end[None, :]).astype(I32), axis=-1),
                              N_EXPERTS - 1)
    tile_valid = (tile_start < pad_end[-1]).astype(I32)
    xs = _dispatch(h2p, pos_flat, n_tiles * tm)
    ys = _experts(xs, tile_expert, tile_valid, w_gate, w_up, w_down, layer)
    return _combine(pos_flat, ys, x1, h2, tw.T, g2, s_gate, s_up, s_down, mod_row)


def _rope_angles(rows, head_dim):
    axis_dim = head_dim // 2
    inv_freq = ROPE_BASE ** (-jnp.arange(0, axis_dim, 2, dtype=F32) / axis_dim)
    row = jnp.repeat(jnp.arange(rows, dtype=F32), GRID_W)
    col = jnp.tile(jnp.arange(GRID_W, dtype=F32), rows)
    return jnp.concatenate([row[:, None] * inv_freq, col[:, None] * inv_freq], axis=-1)


def _rope_tables(n, ctx_len):
    rows = n // GRID_W
    ang = _rope_angles(rows, RET_DIM)
    cos, sin = jnp.cos(ang), jnp.sin(ang)
    ret_cos = jnp.concatenate([cos, cos], axis=-1)
    ret_sin = jnp.concatenate([-sin, sin], axis=-1)
    ang = _rope_angles(rows, DIFF_QK)
    cos, sin = jnp.cos(ang), jnp.sin(ang)
    zero = jnp.zeros_like(sin)
    dc = jnp.tile(jnp.concatenate([cos, cos], axis=-1), (1, 2))
    ds1 = jnp.tile(jnp.concatenate([-sin, zero], axis=-1), (1, 2))
    ds2 = jnp.tile(jnp.concatenate([zero, sin], axis=-1), (1, 2))

    def with_ctx(tab, fill):
        return jnp.concatenate([jnp.full((ctx_len, tab.shape[1]), fill, F32), tab], axis=0)

    return (with_ctx(ret_cos, 1.0), with_ctx(ret_sin, 0.0),
            with_ctx(dc, 1.0), with_ctx(ds1, 0.0), with_ctx(ds2, 0.0))


def kernel(x, c, ctx, c_ctx, w_ada, b_ada, norm1_w, norm2_w, w_in, ret_decay_logit, ret_norm_w,
           diff_qnorm_w, diff_knorm_w, diff_lambda, diff_subln_w, s5_lambda_re, s5_lambda_im,
           s5_log_dt, s5_b_re, s5_b_im, s5_c_re, s5_c_im, s5_d, s5_w_glu, w_out, router_w,
           router_bias, exp_w_gate, exp_w_up, exp_w_down, shared_w_gate, shared_w_up, shared_w_down):
    out_dtype = x.dtype
    batch, n, d = x.shape
    ctx_len = ctx.shape[1]
    depth = w_ada.shape[0]
    s = ctx_len + n
    t = batch * s
    assert ctx_len % TOK_TILE == 0 and n % TOK_TILE == 0 and n % GRID_W == 0
    assert batch + 1 <= SUBLANES
    tiles_per_seq = s // TOK_TILE
    ctx_tiles = ctx_len // TOK_TILE

    def mod_row(i):
        return jnp.where(i % tiles_per_seq < ctx_tiles, batch, i // tiles_per_seq)

    z = jnp.concatenate([ctx.astype(F32), x.astype(F32)], axis=1).reshape(t, d)
    cond = jnp.zeros((SUBLANES, d), F32).at[:batch].set(c.astype(F32)).at[batch].set(c_ctx.astype(F32))
    mod = _adaln(cond, w_ada.astype(F32), b_ada.astype(F32))
    ret_cos, ret_sin, dc, ds1, ds2 = _rope_tables(n, ctx_len)

    for l in range(depth):
        lambda_init = 0.8 - 0.6 * math.exp(-0.3 * l)
        m = mod[l].reshape(SUBLANES, N_MOD, 1, d)
        sh1, sc1, g1, sh2, sc2, g2 = (m[:, j] for j in range(N_MOD))
        proj = _in_proj(z, norm1_w[l][None], sc1, sh1, w_in[l].astype(BF16), mod_row)
        proj3 = proj.reshape(batch, s, proj.shape[1])
        log_gamma = jax.nn.log_sigmoid(ret_decay_logit[l].astype(F32))
        ret = _retention(proj3, log_gamma, ret_cos, ret_sin, ret_norm_w[l][None], ctx_len)
        q0t, q1t, kk, vt = _diff_prep(proj, diff_qnorm_w[l], diff_knorm_w[l], dc, ds1, ds2,
                                      batch, s)
        dif = _diff_attention(q0t, q1t, kk, vt, diff_lambda[l], diff_subln_w[l], ctx_len,
                              lambda_init)
        s5 = _s5_mixer(proj3, s5_lambda_re[l], s5_lambda_im[l], s5_log_dt[l], s5_b_re[l],
                       s5_b_im[l], s5_c_re[l], s5_c_im[l], s5_d[l], s5_w_glu[l], ctx_len)
        x1, h2, h2p, te, tw, rk, cnt = _out_proj_route(
            z, ret.reshape(t, RET_W), dif.reshape(t, DIFF_W), s5.reshape(t, S5_CH),
            w_out[l].astype(BF16), g1, norm2_w[l][None], sc2, sh2, router_w[l], router_bias[l],
            mod_row)
        z = _moe(x1, h2, h2p, te, tw, rk, cnt, g2, exp_w_gate.astype(F32),
                 exp_w_up.astype(F32), exp_w_down.astype(F32), l,
                 shared_w_gate[l].astype(BF16), shared_w_up[l].astype(BF16),
                 shared_w_down[l].astype(BF16), mod_row)
    return z.reshape(batch, s, d)[:, ctx_len:].astype(out_dtype)
```

```python
import functools
import math

import jax
import jax.numpy as jnp
from jax import lax
from jax.experimental import pallas as pl
from jax.experimental.pallas import tpu as pltpu

F32 = jnp.float32
BF16 = jnp.bfloat16
U32 = jnp.uint32
I32 = jnp.int32

GRID_W = 64
N_MOD = 6
EPS = 1e-6
ROPE_BASE = 10000.0
RET_HEADS = 4
RET_DIM = 128
RET_W = RET_HEADS * RET_DIM
DIFF_HEADS = 8
DIFF_QK = 64
DIFF_V = 2 * DIFF_QK
DIFF_QK_W = DIFF_HEADS * 2 * DIFF_QK
DIFF_W = DIFF_HEADS * DIFF_V
S5_GROUP = 16
S5_CH = 512
S5_GROUPS = S5_CH // S5_GROUP
S5_STATE = 64
S5_W = S5_GROUPS * S5_STATE
N_EXPERTS = 64
EXPERT_FF = 512
TOP_K = 8
N_EXPERT_GROUPS = 8
TOPK_GROUPS = 4
ROUTE_SCALE = 2.5

V7X_VMEM_BYTES = 64 * 1024 * 1024
LANES = 128
SUBLANES = 8
BF16_SUBLANES = 16
TOK_TILE = 256
RET_CHUNK = 256
ATT_TQ = 256
ATT_TK = 256
ATT_HEADS = 4
ATT_UNROLL_PAIRS = 2
LOG2E = math.log2(math.e)
S5_BLOCK = 256
EXP_TILE = 256
ADA_TN = 1024
ROW_DMA_UNROLL = 4


def _cparams(n_axes, vmem_mb=48):
    return pltpu.CompilerParams(
        dimension_semantics=("arbitrary",) * n_axes,
        vmem_limit_bytes=vmem_mb * 1024 * 1024,
    )


def _dot(a, b):
    return jnp.dot(a, b, preferred_element_type=F32)


def _dot_nt(a, b):
    return lax.dot_general(a, b, (((1,), (1,)), ((), ())), preferred_element_type=F32)


def _dot_tn(a, b):
    return lax.dot_general(a, b, (((0,), (0,)), ((), ())), preferred_element_type=F32)


def _split(x):
    hi = x.astype(BF16)
    lo = (x - hi.astype(F32)).astype(BF16)
    return hi, lo


def _dot3(a, b):
    ah, al = _split(a)
    bh, bl = _split(b)
    return _dot(ah, bh) + _dot(al, bh) + _dot(ah, bl)


def _dot3_nt(a, b):
    ah, al = _split(a)
    bh, bl = _split(b)
    return _dot_nt(ah, bh) + _dot_nt(al, bh) + _dot_nt(ah, bl)


def _silu(x):
    return x * jax.nn.sigmoid(x)


def _pack_halves(lo, hi):
    lo_b = pltpu.bitcast(lo.astype(BF16).astype(F32), U32) >> 16
    hi_b = pltpu.bitcast(hi.astype(BF16).astype(F32), U32) & jnp.uint32(0xFFFF0000)
    return hi_b | lo_b


def _unpack_halves(p):
    lo = pltpu.bitcast(p << 16, F32)
    hi = pltpu.bitcast(p & jnp.uint32(0xFFFF0000), F32)
    return lo, hi


def _ada_kernel(cond_ref, w_ref, b_ref, o_ref):
    o_ref[...] = _dot3(_silu(cond_ref[...]), w_ref[...]) + b_ref[...]


def _adaln(cond, w_ada, b_ada):
    n_layers, d, n = w_ada.shape
    rows = cond.shape[0]
    return pl.pallas_call(
        _ada_kernel,
        out_shape=jax.ShapeDtypeStruct((n_layers, rows, n), F32),
        grid=(n_layers, n // ADA_TN),
        in_specs=[
            pl.BlockSpec((rows, d), lambda l, j: (0, 0)),
            pl.BlockSpec((None, d, ADA_TN), lambda l, j: (l, 0, j)),
            pl.BlockSpec((None, 1, ADA_TN), lambda l, j: (l, 0, j)),
        ],
        out_specs=pl.BlockSpec((None, rows, ADA_TN), lambda l, j: (l, 0, j)),
        compiler_params=_cparams(2),
        name="adaln",
    )(cond, w_ada, b_ada.reshape(n_layers, 1, n))


def _proj_kernel(x_ref, nw_ref, sc_ref, sh_ref, w_ref, o_ref):
    x = x_ref[...]
    ms = jnp.mean(x * x, axis=-1, keepdims=True)
    h = x * lax.rsqrt(ms + EPS) * nw_ref[...]
    h = h * (1.0 + sc_ref[...]) + sh_ref[...]
    o_ref[...] = _dot(h.astype(BF16), w_ref[...]).astype(BF16)


def _in_proj(z, norm_w, sc, sh, w_in, mod_row):
    t, d = z.shape
    n = w_in.shape[1]
    mod_spec = pl.BlockSpec((None, 1, d), lambda i: (mod_row(i), 0, 0))
    return pl.pallas_call(
        _proj_kernel,
        out_shape=jax.ShapeDtypeStruct((t, n), BF16),
        grid=(t // TOK_TILE,),
        in_specs=[
            pl.BlockSpec((TOK_TILE, d), lambda i: (i, 0)),
            pl.BlockSpec((1, d), lambda i: (0, 0)),
            mod_spec,
            mod_spec,
            pl.BlockSpec((d, n), lambda i: (0, 0), pipeline_mode=pl.Buffered(1)),
        ],
        out_specs=pl.BlockSpec((TOK_TILE, n), lambda i: (i, 0)),
        compiler_params=_cparams(1, 56),
        name="in_proj",
    )(z, norm_w, sc, sh, w_in)


def _ret_kernel(lg_ref, q_ref, k_ref, v_ref, cos_ref, sin_ref, *rest, reverse, chunk):
    if reverse:
        of_ref, g_ref, nw_ref, o_ref, state_ref = rest
    else:
        o_ref, state_ref = rest
    t = pl.program_id(1)

    @pl.when(t == 0)
    def _():
        state_ref[...] = jnp.zeros_like(state_ref)

    ii = lax.broadcasted_iota(I32, (chunk, chunk), 0)
    jj = lax.broadcasted_iota(I32, (chunk, chunk), 1)
    rel = ((jj - ii) if reverse else (ii - jj)).astype(F32)
    idx = lax.broadcasted_iota(I32, (chunk, 1), 0).astype(F32)
    cos2 = cos_ref[...]
    sin2 = sin_ref[...]
    scale = RET_DIM ** -0.5
    direction = 1 if reverse else 0

    def rope(x):
        return x * cos2 + pltpu.roll(x, RET_DIM // 2, 1) * sin2

    for h in range(RET_HEADS):
        sl = slice(h * RET_DIM, (h + 1) * RET_DIM)
        lg = lg_ref[direction, h]
        dmask = jnp.where(rel >= 0.0, jnp.exp(lg * jnp.maximum(rel, 0.0)), 0.0)
        if reverse:
            qdec = jnp.exp(lg * (chunk - idx))
            kdec = jnp.exp(lg * idx)
        else:
            qdec = jnp.exp(lg * (idx + 1.0))
            kdec = jnp.exp(lg * (chunk - 1.0 - idx))
        cdec = jnp.exp(jnp.full((1, 1), lg * chunk, F32))
        q = rope(q_ref[:, sl].astype(F32))
        k = rope(k_ref[:, sl].astype(F32)) * scale
        v = v_ref[:, sl]
        qb = q.astype(BF16)
        scores = _dot_nt(qb, k.astype(BF16)) * dmask
        inner = _dot(scores.astype(BF16), v)
        s_old = state_ref[h]
        cross = _dot(qb, s_old.astype(BF16)) * qdec
        state_ref[h] = s_old * cdec + _dot_tn((k * kdec).astype(BF16), v)
        o = inner + cross
        if reverse:
            o = o + of_ref[:, sl]
            mu = jnp.mean(o, axis=-1, keepdims=True)
            oc = o - mu
            var = jnp.mean(oc * oc, axis=-1, keepdims=True)
            on = oc * lax.rsqrt(var + EPS) * nw_ref[:, sl]
            o_ref[:, sl] = (_silu(g_ref[:, sl].astype(F32)) * on).astype(BF16)
        else:
            o_ref[:, sl] = o


def _retention(proj3, log_gamma, cos2, sin2, norm_w, ctx_len):
    b, s, _ = proj3.shape
    c = RET_CHUNK
    nch = s // c
    nctx = ctx_len // c

    def fwd_chunk(t):
        return t

    def bwd_chunk(t):
        return jnp.where(t < nctx, nctx - 1 - t, nch - 1 + nctx - t)

    def specs(chunk_of):
        col = lambda j: pl.BlockSpec((None, c, RET_W), lambda bi, t, j=j: (bi, chunk_of(t), j))
        tab = pl.BlockSpec((c, RET_DIM), lambda bi, t: (chunk_of(t), 0))
        return col, tab

    smem = pl.BlockSpec(memory_space=pltpu.SMEM)
    scratch = [pltpu.VMEM((RET_HEADS, RET_DIM, RET_DIM), F32)]

    col, tab = specs(fwd_chunk)
    o_f = pl.pallas_call(
        functools.partial(_ret_kernel, reverse=False, chunk=c),
        out_shape=jax.ShapeDtypeStruct((b, s, RET_W), F32),
        grid=(b, nch),
        in_specs=[smem, col(0), col(1), col(2), tab, tab],
        out_specs=col(0),
        scratch_shapes=scratch,
        compiler_params=_cparams(2),
        name="retention_fwd",
    )(log_gamma, proj3, proj3, proj3, cos2, sin2)

    col, tab = specs(bwd_chunk)
    return pl.pallas_call(
        functools.partial(_ret_kernel, reverse=True, chunk=c),
        out_shape=jax.ShapeDtypeStruct((b, s, RET_W), BF16),
        grid=(b, nch),
        in_specs=[smem, col(0), col(1), col(2), tab, tab, col(0), col(3),
                  pl.BlockSpec((1, RET_W), lambda bi, t: (0, 0))],
        out_specs=col(0),
        scratch_shapes=scratch,
        compiler_params=_cparams(2),
        name="retention_bwd",
    )(log_gamma, proj3, proj3, proj3, cos2, sin2, o_f, proj3, norm_w)


def _dprep_kernel(q_ref, k_ref, v_ref, qw_ref, kw_ref, c_ref, s1_ref, s2_ref, bd_ref,
                  q0t_ref, q1t_ref, kk_ref, vt_ref):
    cc = c_ref[...]
    s1 = s1_ref[...]
    s2 = s2_ref[...]
    bd = bd_ref[...]
    rows = q_ref.shape[0]
    first = lax.broadcasted_iota(I32, (LANES, rows), 0) < DIFF_QK
    quarter = DIFF_QK // 2

    def prep(x, w):
        x = x.astype(F32)
        hi, lo = _split(x * x)
        ss = _dot(hi, bd) + _dot(lo, bd)
        xn = x * lax.rsqrt(ss * (1.0 / DIFF_QK) + EPS) * w
        return xn * cc + pltpu.roll(xn, LANES - quarter, 1) * s1 + pltpu.roll(xn, quarter, 1) * s2

    for h in range(DIFF_HEADS):
        sl = slice(h * LANES, (h + 1) * LANES)
        qt = (prep(q_ref[:, sl], qw_ref[...]) * (DIFF_QK ** -0.5 * LOG2E)).T
        q0t_ref[sl, :] = jnp.where(first, qt, 0.0).astype(BF16)
        q1t_ref[sl, :] = jnp.where(first, 0.0, qt).astype(BF16)
        kk_ref[:, sl] = prep(k_ref[:, sl], kw_ref[...]).astype(BF16)
        vt_ref[sl, :] = v_ref[:, sl].astype(F32).T.astype(BF16)


def _diff_prep(proj, qnorm_w, knorm_w, dc, ds1, ds2, batch, seq):
    t, _ = proj.shape
    tiles_per_seq = seq // TOK_TILE
    qcol = (4 * RET_W) // DIFF_QK_W
    blockdiag = jnp.kron(jnp.eye(2, dtype=F32), jnp.ones((DIFF_QK, DIFF_QK), F32)).astype(BF16)
    tab = pl.BlockSpec((TOK_TILE, LANES), lambda i: (i % tiles_per_seq, 0))
    row = pl.BlockSpec((1, LANES), lambda i: (0, 0))
    col = lambda j: pl.BlockSpec((TOK_TILE, DIFF_QK_W), lambda i, j=j: (i, j))
    tr = pl.BlockSpec((None, DIFF_QK_W, TOK_TILE),
                      lambda i: (i // tiles_per_seq, 0, i % tiles_per_seq))
    shp_t = jax.ShapeDtypeStruct((batch, DIFF_QK_W, seq), BF16)
    return pl.pallas_call(
        _dprep_kernel,
        out_shape=(shp_t, shp_t, jax.ShapeDtypeStruct((t, DIFF_QK_W), BF16), shp_t),
        grid=(t // TOK_TILE,),
        in_specs=[col(qcol), col(qcol + 1), col(qcol + 2), row, row, tab, tab, tab,
                  pl.BlockSpec((LANES, LANES), lambda i: (0, 0))],
        out_specs=(tr, tr, col(0), tr),
        compiler_params=_cparams(1),
        name="diff_prep",
    )(proj, proj, proj, jnp.tile(qnorm_w, 2)[None], jnp.tile(knorm_w, 2)[None], dc, ds1, ds2,
      blockdiag)


def _flash_kernel(lam_ref, q0t_ref, q1t_ref, k_ref, vt_ref, sw_ref, o_ref, m_s, acc_s, s_s, *,
                  heads, nctx_tiles, ctx_kv, all_kv, lambda_init):
    i = pl.program_id(2)
    nkv = jnp.where(i < nctx_tiles, ctx_kv, all_kv)
    m_s[...] = jnp.full(m_s.shape, -jnp.inf, F32)
    acc_s[...] = jnp.zeros(acc_s.shape, F32)
    ones = jnp.ones((BF16_SUBLANES, ATT_TK), BF16)

    def scores(j, buf):
        off = pl.multiple_of(j * ATT_TK, ATT_TK)
        for h in range(heads):
            sl = slice(h * DIFF_V, (h + 1) * DIFF_V)
            kt = k_ref[pl.ds(off, ATT_TK), sl]
            for comp, qt_ref in enumerate((q0t_ref, q1t_ref)):
                s_s[buf, 2 * h + comp] = _dot(kt, qt_ref[sl, :])

    def softmax_pv(j, buf):
        off = pl.multiple_of(j * ATT_TK, ATT_TK)
        for h in range(heads):
            sl = slice(h * DIFF_V, (h + 1) * DIFF_V)
            vt = jnp.concatenate([vt_ref[sl, pl.ds(off, ATT_TK)], ones], axis=0)
            for comp in range(2):
                c = 2 * h + comp
                st = s_s[buf, c]
                m_old = m_s[c]
                m_new = jnp.maximum(m_old, jnp.max(st, axis=0, keepdims=True))
                alpha = jnp.exp2(m_old - m_new)
                pt = jnp.exp2(st - m_new)
                acc_s[c] = alpha * acc_s[c] + _dot(vt, pt.astype(BF16))
                m_s[c] = m_new

    scores(0, 0)

    def pairs(first, n_pairs):
        for p in range(n_pairs):
            scores(first + 2 * p + 1, 1)
            softmax_pv(first + 2 * p, 0)
            scores(first + 2 * p + 2, 0)
            softmax_pv(first + 2 * p + 1, 1)

    group = 2 * ATT_UNROLL_PAIRS
    n_groups = (nkv - 1) // group

    def group_body(g, carry):
        pairs(group * g, ATT_UNROLL_PAIRS)
        return carry

    def pair_body(p, carry):
        pairs(group * n_groups + 2 * p, 1)
        return carry

    lax.fori_loop(0, n_groups, group_body, 0)
    lax.fori_loop(0, ((nkv - 1) % group) // 2, pair_body, 0)
    softmax_pv(nkv - 1, 0)
    lv = lam_ref[...]
    lam = (jnp.exp(jnp.sum(lv[0:1] * lv[1:2], axis=-1, keepdims=True))
           - jnp.exp(jnp.sum(lv[2:3] * lv[3:4], axis=-1, keepdims=True)) + lambda_init)
    for h in range(heads):
        sl = slice(h * DIFF_V, (h + 1) * DIFF_V)
        o0 = acc_s[2 * h, 0:DIFF_V] / acc_s[2 * h, DIFF_V:DIFF_V + 1]
        o1 = acc_s[2 * h + 1, 0:DIFF_V] / acc_s[2 * h + 1, DIFF_V:DIFF_V + 1]
        d = (o0 - lam * o1).T
        ms = jnp.mean(d * d, axis=-1, keepdims=True)
        o_ref[:, sl] = (d * lax.rsqrt(ms + EPS) * sw_ref[...] * (1.0 - lambda_init)).astype(BF16)


def _diff_attention(q0t, q1t, kk, vt, lam_vecs, subln_w, ctx_len, lambda_init):
    b, _, s = q0t.shape
    kk = kk.reshape(b, s, DIFF_QK_W)
    hb = ATT_HEADS
    w = hb * DIFF_V
    assert (s // ATT_TK) % 2 == 1 and (ctx_len // ATT_TK) % 2 == 1
    qspec = pl.BlockSpec((None, w, ATT_TQ), lambda bi, h, i: (bi, h, i))
    return pl.pallas_call(
        functools.partial(_flash_kernel, heads=hb, nctx_tiles=ctx_len // ATT_TQ,
                          ctx_kv=ctx_len // ATT_TK, all_kv=s // ATT_TK, lambda_init=lambda_init),
        out_shape=jax.ShapeDtypeStruct((b, s, DIFF_W), BF16),
        grid=(b, DIFF_HEADS // hb, s // ATT_TQ),
        in_specs=[
            pl.BlockSpec((4, DIFF_QK), lambda bi, h, i: (0, 0)),
            qspec, qspec,
            pl.BlockSpec((None, s, w), lambda bi, h, i: (bi, 0, h)),
            pl.BlockSpec((None, w, s), lambda bi, h, i: (bi, h, 0)),
            pl.BlockSpec((1, DIFF_V), lambda bi, h, i: (0, 0)),
        ],
        out_specs=pl.BlockSpec((None, ATT_TQ, w), lambda bi, h, i: (bi, i, h)),
        scratch_shapes=[pltpu.VMEM((2 * hb, 1, ATT_TQ), F32),
                        pltpu.VMEM((2 * hb, DIFF_V + BF16_SUBLANES, ATT_TQ), F32),
                        pltpu.VMEM((2, 2 * hb, ATT_TK, ATT_TQ), F32)],
        compiler_params=_cparams(3),
        name="diff_flash",
    )(lam_vecs, q0t, q1t, kk, vt, subln_w[None])


def _s5_kernel(u_ref, br_ref, bi_ref, cr_ref, ci_ref, tab_ref, *rest, reverse, ntiles):
    if reverse:
        yf_ref, d_ref, wg_ref, o_ref, xr_s, xi_s, car_s, cai_s = rest
    else:
        o_ref, xr_s, xi_s, car_s, cai_s = rest
    t = pl.program_id(1)

    @pl.when(t == 0)
    def _():
        car_s[...] = jnp.zeros_like(car_s)
        cai_s[...] = jnp.zeros_like(cai_s)

    u = u_ref[...]
    nq = S5_CH // LANES
    sw = S5_W // nq
    for q in range(nq):
        uq = u[:, q * LANES:(q + 1) * LANES]
        xr_s[:, q * sw:(q + 1) * sw] = _dot(uq, br_ref[q])
        xi_s[:, q * sw:(q + 1) * sw] = _dot(uq, bi_ref[q])

    def body(n, carry):
        cr, ci = carry
        tile = (ntiles - 1 - n) if reverse else n
        off = pl.multiple_of(tile * SUBLANES, SUBLANES)
        xr = xr_s[pl.ds(off, SUBLANES), :]
        xi = xi_s[pl.ds(off, SUBLANES), :]
        for lvl, k in enumerate((1, 2, 4)):
            pr = tab_ref[2 * lvl]
            pi = tab_ref[2 * lvl + 1]
            shift = (SUBLANES - k) if reverse else k
            sr = pltpu.roll(xr, shift, 0)
            si = pltpu.roll(xi, shift, 0)
            xr, xi = xr + pr * sr - pi * si, xi + pr * si + pi * sr
        ar = tab_ref[6]
        ai = tab_ref[7]
        xr, xi = xr + ar * cr - ai * ci, xi + ar * ci + ai * cr
        xr_s[pl.ds(off, SUBLANES), :] = xr
        xi_s[pl.ds(off, SUBLANES), :] = xi
        last = 0 if reverse else SUBLANES - 1
        return xr[last:last + 1], xi[last:last + 1]

    cr, ci = lax.fori_loop(0, ntiles, body, (car_s[...], cai_s[...]))
    car_s[...] = cr
    cai_s[...] = ci
    y = jnp.concatenate(
        [_dot(xr_s[:, q * sw:(q + 1) * sw].astype(BF16), cr_ref[q])
         + _dot(xi_s[:, q * sw:(q + 1) * sw].astype(BF16), ci_ref[q]) for q in range(nq)], axis=1)
    if reverse:
        y = y + yf_ref[...] + d_ref[...] * u.astype(F32)
        y = jax.nn.gelu(y, approximate=True)
        o_ref[...] = (y * jax.nn.sigmoid(_dot(y.astype(BF16), wg_ref[...]))).astype(BF16)
    else:
        o_ref[...] = y


def _s5_tables(a_re_log, a_im_ang, reverse):
    w = a_re_log.reshape(1, S5_W)
    th = a_im_ang.reshape(1, S5_W)
    row = jnp.arange(SUBLANES, dtype=F32)[:, None]
    tabs = []
    for k in (1, 2, 4):
        keep = (row <= SUBLANES - 1 - k) if reverse else (row >= k)
        mag = jnp.exp(w * k)
        tabs.append(jnp.where(keep, mag * jnp.cos(th * k), 0.0))
        tabs.append(jnp.where(keep, mag * jnp.sin(th * k), 0.0))
    e = (SUBLANES - row) if reverse else (row + 1.0)
    mag = jnp.exp(w * e)
    tabs.append(mag * jnp.cos(th * e))
    tabs.append(mag * jnp.sin(th * e))
    return jnp.stack(tabs)


def _s5_params(lam_re, lam_im, log_dt, b_re, b_im):
    dt = jnp.exp(log_dt)[:, None]
    wlog = lam_re * dt
    ang = lam_im * dt
    mag = jnp.exp(wlog)
    a_re = mag * jnp.cos(ang)
    a_im = mag * jnp.sin(ang)
    den = lam_re * lam_re + lam_im * lam_im
    nr = a_re - 1.0
    f_re = (nr * lam_re + a_im * lam_im) / den
    f_im = (a_im * lam_re - nr * lam_im) / den
    bb_re = f_re[..., None] * b_re - f_im[..., None] * b_im
    bb_im = f_re[..., None] * b_im + f_im[..., None] * b_re
    eye = jnp.eye(S5_GROUPS, dtype=F32)
    bmat_re = jnp.einsum('gph,gk->ghkp', bb_re, eye).reshape(S5_CH, S5_W)
    bmat_im = jnp.einsum('gph,gk->ghkp', bb_im, eye).reshape(S5_CH, S5_W)
    return wlog, ang, _diag_blocks(bmat_re).astype(BF16), _diag_blocks(bmat_im).astype(BF16)


def _diag_blocks(m):
    nq = S5_CH // LANES
    r, c = m.shape[0] // nq, m.shape[1] // nq
    return jnp.stack([m[q * r:(q + 1) * r, q * c:(q + 1) * c] for q in range(nq)])


def _s5_mixer(proj3, lam_re, lam_im, log_dt, b_re, b_im, c_re, c_im, d, w_glu, ctx_len):
    b, s, _ = proj3.shape
    tb = S5_BLOCK
    nb = s // tb
    nctx = ctx_len // tb
    ucol = (4 * RET_W + 2 * DIFF_QK_W + DIFF_W) // S5_CH
    eye = jnp.eye(S5_GROUPS, dtype=F32)
    cmat_re = _diag_blocks(jnp.einsum('ghp,gk->gpkh', c_re, eye).reshape(S5_W, S5_CH)).astype(BF16)
    cmat_im = _diag_blocks(-jnp.einsum('ghp,gk->gpkh', c_im, eye).reshape(S5_W, S5_CH)).astype(BF16)
    full = lambda shape: pl.BlockSpec(shape, lambda bi, t: tuple(0 for _ in shape))
    scratch = [pltpu.VMEM((tb, S5_W), F32), pltpu.VMEM((tb, S5_W), F32),
               pltpu.VMEM((1, S5_W), F32), pltpu.VMEM((1, S5_W), F32)]
    nq = S5_CH // LANES
    weights = [full((nq, LANES, S5_W // nq)), full((nq, LANES, S5_W // nq)),
               full((nq, S5_W // nq, LANES)), full((nq, S5_W // nq, LANES)),
               full((8, SUBLANES, S5_W))]

    def blk_f(t):
        return t

    def blk_b(t):
        return jnp.where(t < nctx, nctx - 1 - t, nb - 1 + nctx - t)

    wl, ang, bre, bim = _s5_params(lam_re[0], lam_im[0], log_dt[0], b_re, b_im)
    y_f = pl.pallas_call(
        functools.partial(_s5_kernel, reverse=False, ntiles=tb // SUBLANES),
        out_shape=jax.ShapeDtypeStruct((b, s, S5_CH), F32),
        grid=(b, nb),
        in_specs=[pl.BlockSpec((None, tb, S5_CH), lambda bi, t: (bi, blk_f(t), ucol))] + weights,
        out_specs=pl.BlockSpec((None, tb, S5_CH), lambda bi, t: (bi, blk_f(t), 0)),
        scratch_shapes=scratch,
        compiler_params=_cparams(2),
        name="s5_fwd",
    )(proj3, bre, bim, cmat_re, cmat_im, _s5_tables(wl, ang, False))

    wl, ang, bre, bim = _s5_params(lam_re[1], lam_im[1], log_dt[1], b_re, b_im)
    return pl.pallas_call(
        functools.partial(_s5_kernel, reverse=True, ntiles=tb // SUBLANES),
        out_shape=jax.ShapeDtypeStruct((b, s, S5_CH), BF16),
        grid=(b, nb),
        in_specs=[pl.BlockSpec((None, tb, S5_CH), lambda bi, t: (bi, blk_b(t), ucol))] + weights + [
            pl.BlockSpec((None, tb, S5_CH), lambda bi, t: (bi, blk_b(t), 0)),
            full((1, S5_CH)), full((S5_CH, S5_CH))],
        out_specs=pl.BlockSpec((None, tb, S5_CH), lambda bi, t: (bi, blk_b(t), 0)),
        scratch_shapes=scratch,
        compiler_params=_cparams(2),
        name="s5_bwd",
    )(proj3, bre, bim, cmat_re, cmat_im, _s5_tables(wl, ang, True), y_f, d[None],
      w_glu.astype(BF16))


def _neg_inf_like(x):
    return jnp.full(x.shape, -jnp.inf, x.dtype)


def _out_kernel(x_ref, ret_ref, dif_ref, s5_ref, wo_ref, g1_ref, nw_ref, sc_ref, sh_ref,
                rw_ref, rb_ref, tri_ref,
                x1_ref, h2_ref, h2p_ref, te_ref, tw_ref, rk_ref, cnt_ref, cnt_s):
    i = pl.program_id(0)

    @pl.when(i == 0)
    def _():
        cnt_s[...] = jnp.zeros_like(cnt_s)

    d = x_ref.shape[1]
    tm = x_ref.shape[0]
    mix = (_dot(ret_ref[...], wo_ref[0:RET_W, :])
           + _dot(dif_ref[...], wo_ref[RET_W:RET_W + DIFF_W, :])
           + _dot(s5_ref[...], wo_ref[RET_W + DIFF_W:, :]))
    x1 = x_ref[...] + g1_ref[...] * mix
    x1_ref[...] = x1
    ms = jnp.mean(x1 * x1, axis=-1, keepdims=True)
    h2 = x1 * lax.rsqrt(ms + EPS) * nw_ref[...]
    h2 = h2 * (1.0 + sc_ref[...]) + sh_ref[...]
    h2_ref[...] = h2.astype(BF16)
    h2p_ref[...] = _pack_halves(h2[:, :d // 2], h2[:, d // 2:])

    ng = N_EXPERT_GROUPS
    pg = N_EXPERTS // N_EXPERT_GROUPS
    logits = _dot3_nt(rw_ref[...], h2)
    scores = jax.nn.sigmoid(logits)
    sel3 = (scores + rb_ref[...]).reshape(ng, pg, tm)
    scores3 = scores.reshape(ng, pg, tm)
    in_grp = lax.broadcasted_iota(I32, (ng, pg, tm), 1).astype(F32)
    grp = lax.broadcasted_iota(I32, (ng, pg, tm), 0).astype(F32)
    eidx = grp * pg + in_grp
    gidx = lax.broadcasted_iota(I32, (ng, 1, tm), 0).astype(F32)

    m1 = jnp.max(sel3, axis=1, keepdims=True)
    first = jnp.min(jnp.where(sel3 == m1, in_grp, float(pg)), axis=1, keepdims=True)
    m2 = jnp.max(jnp.where(in_grp == first, -jnp.inf, sel3), axis=1, keepdims=True)
    rem = m1 + m2
    gsel = jnp.zeros((ng, 1, tm), F32)
    for _ in range(TOPK_GROUPS):
        mx = jnp.max(rem, axis=0, keepdims=True)
        fi = jnp.min(jnp.where(rem == mx, gidx, float(ng)), axis=0, keepdims=True)
        pick = gidx == fi
        gsel = jnp.where(pick, 1.0, gsel)
        rem = jnp.where(pick, -jnp.inf, rem)
    masked = jnp.where(gsel > 0.0, sel3, -jnp.inf)

    def red2(fn, x):
        return fn(fn(x, axis=1, keepdims=True), axis=0, keepdims=True)

    chosen = jnp.zeros((ng, pg, tm), F32)
    picks = []
    weights = []
    for _ in range(TOP_K):
        mx = red2(jnp.max, masked)
        fi = red2(jnp.min, jnp.where(masked == mx, eidx, float(N_EXPERTS)))
        pick = eidx == fi
        picks.append(fi)
        weights.append(red2(jnp.sum, jnp.where(pick, scores3, 0.0)))
        chosen = jnp.where(pick, 1.0, chosen)
        masked = jnp.where(pick, -jnp.inf, masked)
    wsum = weights[0]
    for w in weights[1:]:
        wsum = wsum + w
    inv = ROUTE_SCALE / wsum

    chosen2 = chosen.reshape(N_EXPERTS, tm)
    cum = _dot(chosen2.astype(BF16), tri_ref[...]) + cnt_s[...]
    cum3 = cum.reshape(ng, pg, tm)
    for k in range(TOP_K):
        te_ref[k:k + 1, :] = picks[k].reshape(1, tm).astype(I32)
        tw_ref[k:k + 1, :] = (weights[k] * inv).reshape(1, tm)
        rk = red2(jnp.sum, jnp.where(eidx == picks[k], cum3, 0.0))
        rk_ref[k:k + 1, :] = rk.reshape(1, tm).astype(I32)
    cnt_new = cnt_s[...] + jnp.sum(chosen2, axis=-1, keepdims=True)
    cnt_s[...] = cnt_new
    cnt_ref[...] = jnp.broadcast_to(cnt_new, cnt_ref.shape)


def _out_proj_route(z, ret, dif, s5, w_out, g1, norm_w, sc, sh, router_w, router_bias, mod_row,
                    src_tile, n_out_tiles):
    d = z.shape[1]
    tm = TOK_TILE
    t = n_out_tiles * tm
    tri = (jnp.arange(tm)[:, None] < jnp.arange(tm)[None, :]).astype(BF16)
    mod_spec = pl.BlockSpec((None, 1, d), lambda i: (mod_row(src_tile(i)), 0, 0))
    src = lambda w: pl.BlockSpec((tm, w), lambda i: (src_tile(i), 0))
    tok = lambda w: pl.BlockSpec((tm, w), lambda i: (i, 0))
    const = lambda shape: pl.BlockSpec(shape, lambda i: tuple(0 for _ in shape))
    lane_out = pl.BlockSpec((TOP_K, tm), lambda i: (0, i))
    return pl.pallas_call(
        _out_kernel,
        out_shape=(
            jax.ShapeDtypeStruct((t, d), F32),
            jax.ShapeDtypeStruct((t, d), BF16),
            jax.ShapeDtypeStruct((t, d // 2), U32),
            jax.ShapeDtypeStruct((TOP_K, t), I32),
            jax.ShapeDtypeStruct((TOP_K, t), F32),
            jax.ShapeDtypeStruct((TOP_K, t), I32),
            jax.ShapeDtypeStruct((N_EXPERTS, LANES), F32),
        ),
        grid=(t // tm,),
        in_specs=[src(d), src(RET_W), src(DIFF_W), src(S5_CH), const((d, d)),
                  mod_spec, const((1, d)), mod_spec, mod_spec,
                  const((N_EXPERTS, d)), const((N_EXPERTS, 1)), const((tm, tm))],
        out_specs=(tok(d), tok(d), tok(d // 2), lane_out, lane_out, lane_out,
                   const((N_EXPERTS, LANES))),
        scratch_shapes=[pltpu.VMEM((N_EXPERTS, 1), F32)],
        compiler_params=_cparams(1),
        name="out_proj_route",
    )(z, ret, dif, s5, w_out, g1, norm_w, sc, sh, router_w.T, router_bias[:, None], tri)


def _dispatch_kernel(pos_hbm, x_ref, xs_in, xs_out, pos_s, sem, psem):
    del xs_in
    i = pl.program_id(0)
    tm = x_ref.shape[0]
    n = tm * TOP_K
    cp = pltpu.make_async_copy(pos_hbm.at[pl.ds(pl.multiple_of(i * n, n), n)], pos_s, psem)
    cp.start()
    cp.wait()

    def row_copy(r, p):
        return pltpu.make_async_copy(x_ref.at[pl.ds(r, 1), :], xs_out.at[pl.ds(p, 1), :], sem)

    def issue(r, c):
        for k in range(TOP_K):
            row_copy(r, pos_s[r * TOP_K + k]).start(priority=k % 2)
        return c

    lax.fori_loop(0, tm, issue, 0, unroll=ROW_DMA_UNROLL)

    def drain(r, c):
        for k in range(TOP_K):
            row_copy(0, 0).wait()
        return c

    lax.fori_loop(0, tm, drain, 0)


def _dispatch(h2p, pos_flat, n_rows):
    t, w = h2p.shape
    tm = TOK_TILE
    xs0 = jnp.zeros((n_rows, w), U32)
    return pl.pallas_call(
        _dispatch_kernel,
        out_shape=jax.ShapeDtypeStruct((n_rows, w), U32),
        grid=(t // tm,),
        in_specs=[pl.BlockSpec(memory_space=pl.ANY),
                  pl.BlockSpec((tm, w), lambda i: (i, 0)),
                  pl.BlockSpec(memory_space=pl.ANY)],
        out_specs=pl.BlockSpec(memory_space=pl.ANY),
        scratch_shapes=[pltpu.SMEM((tm * TOP_K,), I32), pltpu.SemaphoreType.DMA(()),
                        pltpu.SemaphoreType.DMA(())],
        input_output_aliases={2: 0},
        compiler_params=_cparams(1),
        name="moe_dispatch",
    )(pos_flat, h2p, xs0)


def _expert_kernel(te_ref, tv_ref, xs_ref, wg_ref, wu_ref, wd_ref, ys_ref, wg_s, wu_s, wd_s):
    i = pl.program_id(0)
    half = xs_ref.shape[1]

    @pl.when((i == 0) | (te_ref[i] != te_ref[jnp.maximum(i - 1, 0)]))
    def _():
        wg_s[...] = wg_ref[...].astype(BF16)
        wu_s[...] = wu_ref[...].astype(BF16)
        wd_s[...] = wd_ref[...].astype(BF16)

    @pl.when(tv_ref[i] != 0)
    def _():
        lo, hi = _unpack_halves(xs_ref[...])
        lo = lo.astype(BF16)
        hi = hi.astype(BF16)
        g = _dot(lo, wg_s[0:half, :]) + _dot(hi, wg_s[half:, :])
        u = _dot(lo, wu_s[0:half, :]) + _dot(hi, wu_s[half:, :])
        y = _dot((_silu(g) * u).astype(BF16), wd_s[...])
        ys_ref[...] = _pack_halves(y[:, :half], y[:, half:])

    @pl.when(tv_ref[i] == 0)
    def _():
        ys_ref[...] = jnp.zeros_like(ys_ref)


def _experts(xs, tile_expert, tile_valid, w_gate, w_up, w_down, layer):
    n_rows, half = xs.shape
    _, e, d, f = w_gate.shape
    tm = EXP_TILE
    grid_spec = pltpu.PrefetchScalarGridSpec(
        num_scalar_prefetch=2,
        grid=(n_rows // tm,),
        in_specs=[
            pl.BlockSpec((tm, half), lambda i, te, tv: (jnp.where(tv[i] != 0, i, 0), 0)),
            pl.BlockSpec((None, None, d, f), lambda i, te, tv: (layer, te[i], 0, 0)),
            pl.BlockSpec((None, None, d, f), lambda i, te, tv: (layer, te[i], 0, 0)),
            pl.BlockSpec((None, None, f, d), lambda i, te, tv: (layer, te[i], 0, 0)),
        ],
        out_specs=pl.BlockSpec((tm, half), lambda i, te, tv: (i, 0)),
        scratch_shapes=[pltpu.VMEM((d, f), BF16), pltpu.VMEM((d, f), BF16), pltpu.VMEM((f, d), BF16)],
    )
    return pl.pallas_call(
        _expert_kernel,
        out_shape=jax.ShapeDtypeStruct((n_rows, half), U32),
        grid_spec=grid_spec,
        compiler_params=_cparams(1, 56),
        name="moe_experts",
    )(tile_expert, tile_valid, xs, w_gate, w_up, w_down)


def _combine_kernel(pos_hbm, ys_hbm, x1_ref, h2_ref, w_ref, g2_ref, sg_ref, su_ref, sd_ref,
                    o_ref, pos_s, buf, sem, psem):
    i = pl.program_id(0)
    tm = x1_ref.shape[0]
    half = buf.shape[2]
    n = tm * TOP_K
    cp = pltpu.make_async_copy(pos_hbm.at[pl.ds(pl.multiple_of(i * n, n), n)], pos_s, psem)
    cp.start()
    cp.wait()

    def row_copy(r, k, p):
        return pltpu.make_async_copy(ys_hbm.at[pl.ds(p, 1), :], buf.at[k, pl.ds(r, 1), :], sem)

    def issue(r, c):
        for k in range(TOP_K):
            row_copy(r, k, pos_s[r * TOP_K + k]).start(priority=k % 2)
        return c

    lax.fori_loop(0, tm, issue, 0, unroll=ROW_DMA_UNROLL)

    def drain(r, c):
        for k in range(TOP_K):
            row_copy(0, k, 0).wait()
        return c

    lax.fori_loop(0, tm, drain, 0)

    h2 = h2_ref[...]
    hid = _silu(_dot(h2, sg_ref[...])) * _dot(h2, su_ref[...])
    shared = _dot(hid.astype(BF16), sd_ref[...])

    w = w_ref[...]
    acc_lo = None
    acc_hi = None
    for k in range(TOP_K):
        lo, hi = _unpack_halves(buf[k])
        wk = w[:, k:k + 1]
        acc_lo = wk * lo if acc_lo is None else acc_lo + wk * lo
        acc_hi = wk * hi if acc_hi is None else acc_hi + wk * hi
    g2 = g2_ref[...]
    o_ref[:, :half] = x1_ref[:, :half] + g2[:, :half] * (acc_lo + shared[:, :half])
    o_ref[:, half:] = x1_ref[:, half:] + g2[:, half:] * (acc_hi + shared[:, half:])


def _combine(pos_flat, ys, x1, h2, tw, g2, s_gate, s_up, s_down, mod_row):
    t, d = x1.shape
    tm = TOK_TILE
    f = s_gate.shape[1]
    tok = lambda w: pl.BlockSpec((tm, w), lambda i: (i, 0))
    const = lambda shape: pl.BlockSpec(shape, lambda i: tuple(0 for _ in shape))
    return pl.pallas_call(
        _combine_kernel,
        out_shape=jax.ShapeDtypeStruct((t, d), F32),
        grid=(t // tm,),
        in_specs=[pl.BlockSpec(memory_space=pl.ANY), pl.BlockSpec(memory_space=pl.ANY),
                  tok(d), tok(d), tok(TOP_K),
                  pl.BlockSpec((None, 1, d), lambda i: (mod_row(i), 0, 0)),
                  const((d, f)), const((d, f)), const((f, d))],
        out_specs=tok(d),
        scratch_shapes=[pltpu.SMEM((tm * TOP_K,), I32), pltpu.VMEM((TOP_K, tm, d // 2), U32),
                        pltpu.SemaphoreType.DMA(()), pltpu.SemaphoreType.DMA(())],
        compiler_params=_cparams(1),
        name="moe_combine",
    )(pos_flat, ys, x1, h2, tw, g2, s_gate, s_up, s_down)


def _moe(x1, h2, h2p, te, tw, rk, cnt, g2, w_gate, w_up, w_down, layer, s_gate, s_up, s_down,
         mod_row):
    t = x1.shape[0]
    tm = EXP_TILE
    n_tiles = (t * TOP_K) // tm + N_EXPERTS
    counts = cnt[:, 0].astype(I32)
    padded = ((counts + tm - 1) // tm) * tm
    pad_end = jnp.cumsum(padded)
    pad_off = pad_end - padded
    experts = jnp.arange(N_EXPERTS, dtype=I32)
    off_of = jnp.sum(jnp.where(te[:, :, None] == experts, pad_off, 0), axis=-1)
    pos_flat = (off_of + rk).T.reshape(-1)
    tile_start = jnp.arange(n_tiles, dtype=I32) * tm
    tile_expert = jnp.minimum(jnp.sum((tile_start[:, None] >= pad_end[None, :]).astype(I32), axis=-1),
                              N_EXPERTS - 1)
    tile_valid = (tile_start < pad_end[-1]).astype(I32)
    xs = _dispatch(h2p, pos_flat, n_tiles * tm)
    ys = _experts(xs, tile_expert, tile_valid, w_gate, w_up, w_down, layer)
    return _combine(pos_flat, ys, x1, h2, tw.T, g2, s_gate, s_up, s_down, mod_row)


def _rope_angles(rows, head_dim):
    axis_dim = head_dim // 2
    inv_freq = ROPE_BASE ** (-jnp.arange(0, axis_dim, 2, dtype=F32) / axis_dim)
    row = jnp.repeat(jnp.arange(rows, dtype=F32), GRID_W)
    col = jnp.tile(jnp.arange(GRID_W, dtype=F32), rows)
    return jnp.concatenate([row[:, None] * inv_freq, col[:, None] * inv_freq], axis=-1)


def _rope_tables(n, ctx_len):
    rows = n // GRID_W
    ang = _rope_angles(rows, RET_DIM)
    cos, sin = jnp.cos(ang), jnp.sin(ang)
    ret_cos = jnp.concatenate([cos, cos], axis=-1)
    ret_sin = jnp.concatenate([-sin, sin], axis=-1)
    ang = _rope_angles(rows, DIFF_QK)
    cos, sin = jnp.cos(ang), jnp.sin(ang)
    zero = jnp.zeros_like(sin)
    dc = jnp.tile(jnp.concatenate([cos, cos], axis=-1), (1, 2))
    ds1 = jnp.tile(jnp.concatenate([-sin, zero], axis=-1), (1, 2))
    ds2 = jnp.tile(jnp.concatenate([zero, sin], axis=-1), (1, 2))

    def with_ctx(tab, fill):
        return jnp.concatenate([jnp.full((ctx_len, tab.shape[1]), fill, F32), tab], axis=0)

    return (with_ctx(ret_cos, 1.0), with_ctx(ret_sin, 0.0),
            with_ctx(dc, 1.0), with_ctx(ds1, 0.0), with_ctx(ds2, 0.0))


def kernel(x, c, ctx, c_ctx, w_ada, b_ada, norm1_w, norm2_w, w_in, ret_decay_logit, ret_norm_w,
           diff_qnorm_w, diff_knorm_w, diff_lambda, diff_subln_w, s5_lambda_re, s5_lambda_im,
           s5_log_dt, s5_b_re, s5_b_im, s5_c_re, s5_c_im, s5_d, s5_w_glu, w_out, router_w,
           router_bias, exp_w_gate, exp_w_up, exp_w_down, shared_w_gate, shared_w_up, shared_w_down):
    out_dtype = x.dtype
    batch, n, d = x.shape
    ctx_len = ctx.shape[1]
    depth = w_ada.shape[0]
    s = ctx_len + n
    t = batch * s
    assert ctx_len % TOK_TILE == 0 and n % TOK_TILE == 0 and n % GRID_W == 0
    assert batch + 1 <= SUBLANES
    tiles_per_seq = s // TOK_TILE
    ctx_tiles = ctx_len // TOK_TILE

    def mod_row(i):
        return jnp.where(i % tiles_per_seq < ctx_tiles, batch, i // tiles_per_seq)

    z = jnp.concatenate([ctx.astype(F32), x.astype(F32)], axis=1).reshape(t, d)
    cond = jnp.zeros((SUBLANES, d), F32).at[:batch].set(c.astype(F32)).at[batch].set(c_ctx.astype(F32))
    mod = _adaln(cond, w_ada.astype(F32), b_ada.astype(F32))
    ret_cos, ret_sin, dc, ds1, ds2 = _rope_tables(n, ctx_len)

    for l in range(depth):
        lambda_init = 0.8 - 0.6 * math.exp(-0.3 * l)
        m = mod[l].reshape(SUBLANES, N_MOD, 1, d)
        sh1, sc1, g1, sh2, sc2, g2 = (m[:, j] for j in range(N_MOD))
        proj = _in_proj(z, norm1_w[l][None], sc1, sh1, w_in[l].astype(BF16), mod_row)
        proj3 = proj.reshape(batch, s, proj.shape[1])
        log_gamma = jax.nn.log_sigmoid(ret_decay_logit[l].astype(F32))
        ret = _retention(proj3, log_gamma, ret_cos, ret_sin, ret_norm_w[l][None], ctx_len)
        q0t, q1t, kk, vt = _diff_prep(proj, diff_qnorm_w[l], diff_knorm_w[l], dc, ds1, ds2,
                                      batch, s)
        dif = _diff_attention(q0t, q1t, kk, vt, diff_lambda[l], diff_subln_w[l], ctx_len,
                              lambda_init)
        s5 = _s5_mixer(proj3, s5_lambda_re[l], s5_lambda_im[l], s5_log_dt[l], s5_b_re[l],
                       s5_b_im[l], s5_c_re[l], s5_c_im[l], s5_d[l], s5_w_glu[l], ctx_len)
        if l < depth - 1:
            src_tile = lambda i: i
            n_out_tiles = t // TOK_TILE
        else:
            lat_tiles = tiles_per_seq - ctx_tiles
            src_tile = lambda i: i + (i // lat_tiles + 1) * ctx_tiles
            n_out_tiles = batch * lat_tiles
        x1, h2, h2p, te, tw, rk, cnt = _out_proj_route(
            z, ret.reshape(t, RET_W), dif.reshape(t, DIFF_W), s5.reshape(t, S5_CH),
            w_out[l].astype(BF16), g1, norm2_w[l][None], sc2, sh2, router_w[l], router_bias[l],
            mod_row, src_tile, n_out_tiles)
        z = _moe(x1, h2, h2p, te, tw, rk, cnt, g2, exp_w_gate.astype(F32),
                 exp_w_up.astype(F32), exp_w_down.astype(F32), l,
                 shared_w_gate[l].astype(BF16), shared_w_up[l].astype(BF16),
                 shared_w_down[l].astype(BF16), lambda i, f=src_tile: mod_row(f(i)))
    return z.reshape(batch, n, d).astype(out_dtype)
```

```python
import functools
import math

import jax
import jax.numpy as jnp
from jax import lax
from jax.experimental import pallas as pl
from jax.experimental.pallas import tpu as pltpu

F32 = jnp.float32
BF16 = jnp.bfloat16
U32 = jnp.uint32
I32 = jnp.int32

GRID_W = 64
N_MOD = 6
EPS = 1e-6
ROPE_BASE = 10000.0
RET_HEADS = 4
RET_DIM = 128
RET_W = RET_HEADS * RET_DIM
DIFF_HEADS = 8
DIFF_QK = 64
DIFF_V = 2 * DIFF_QK
DIFF_QK_W = DIFF_HEADS * 2 * DIFF_QK
DIFF_W = DIFF_HEADS * DIFF_V
S5_GROUP = 16
S5_CH = 512
S5_GROUPS = S5_CH // S5_GROUP
S5_STATE = 64
S5_W = S5_GROUPS * S5_STATE
N_EXPERTS = 64
EXPERT_FF = 512
TOP_K = 8
N_EXPERT_GROUPS = 8
TOPK_GROUPS = 4
ROUTE_SCALE = 2.5

V7X_VMEM_BYTES = 64 * 1024 * 1024
LANES = 128
SUBLANES = 8
BF16_SUBLANES = 16
TOK_TILE = 256
RET_CHUNK = 256
ATT_TQ = 256
ATT_TK = 256
ATT_HEADS = 4
ATT_UNROLL_PAIRS = 2
LOG2E = math.log2(math.e)
S5_BLOCK = 256
EXP_TILE = 256
ADA_TN = 1024
EXP_BUFS = 3
WIN = 2048
WIN_ALIGN = 1024


def _cparams(n_axes, vmem_mb=48):
    return pltpu.CompilerParams(
        dimension_semantics=("arbitrary",) * n_axes,
        vmem_limit_bytes=vmem_mb * 1024 * 1024,
    )


def _dot(a, b):
    return jnp.dot(a, b, preferred_element_type=F32)


def _dot_nt(a, b):
    return lax.dot_general(a, b, (((1,), (1,)), ((), ())), preferred_element_type=F32)


def _dot_tn(a, b):
    return lax.dot_general(a, b, (((0,), (0,)), ((), ())), preferred_element_type=F32)


def _split(x):
    hi = x.astype(BF16)
    lo = (x - hi.astype(F32)).astype(BF16)
    return hi, lo


def _dot3(a, b):
    ah, al = _split(a)
    bh, bl = _split(b)
    return _dot(ah, bh) + _dot(al, bh) + _dot(ah, bl)


def _dot3_nt(a, b):
    ah, al = _split(a)
    bh, bl = _split(b)
    return _dot_nt(ah, bh) + _dot_nt(al, bh) + _dot_nt(ah, bl)


def _silu(x):
    return x * jax.nn.sigmoid(x)


def _pack_halves(lo, hi):
    lo_b = pltpu.bitcast(lo.astype(BF16).astype(F32), U32) >> 16
    hi_b = pltpu.bitcast(hi.astype(BF16).astype(F32), U32) & jnp.uint32(0xFFFF0000)
    return hi_b | lo_b


def _unpack_halves(p):
    lo = pltpu.bitcast(p << 16, F32)
    hi = pltpu.bitcast(p & jnp.uint32(0xFFFF0000), F32)
    return lo, hi


def _ada_kernel(cond_ref, w_ref, b_ref, o_ref):
    o_ref[...] = _dot3(_silu(cond_ref[...]), w_ref[...]) + b_ref[...]


def _adaln(cond, w_ada, b_ada):
    n_layers, d, n = w_ada.shape
    rows = cond.shape[0]
    return pl.pallas_call(
        _ada_kernel,
        out_shape=jax.ShapeDtypeStruct((n_layers, rows, n), F32),
        grid=(n_layers, n // ADA_TN),
        in_specs=[
            pl.BlockSpec((rows, d), lambda l, j: (0, 0)),
            pl.BlockSpec((None, d, ADA_TN), lambda l, j: (l, 0, j)),
            pl.BlockSpec((None, 1, ADA_TN), lambda l, j: (l, 0, j)),
        ],
        out_specs=pl.BlockSpec((None, rows, ADA_TN), lambda l, j: (l, 0, j)),
        compiler_params=_cparams(2),
        name="adaln",
    )(cond, w_ada, b_ada.reshape(n_layers, 1, n))


def _proj_kernel(x_ref, nw_ref, sc_ref, sh_ref, w_ref, o_ref):
    x = x_ref[...]
    ms = jnp.mean(x * x, axis=-1, keepdims=True)
    h = x * lax.rsqrt(ms + EPS) * nw_ref[...]
    h = h * (1.0 + sc_ref[...]) + sh_ref[...]
    o_ref[...] = _dot(h.astype(BF16), w_ref[...]).astype(BF16)


def _in_proj(z, norm_w, sc, sh, w_in, mod_row):
    t, d = z.shape
    n = w_in.shape[1]
    mod_spec = pl.BlockSpec((None, 1, d), lambda i: (mod_row(i), 0, 0))
    return pl.pallas_call(
        _proj_kernel,
        out_shape=jax.ShapeDtypeStruct((t, n), BF16),
        grid=(t // TOK_TILE,),
        in_specs=[
            pl.BlockSpec((TOK_TILE, d), lambda i: (i, 0)),
            pl.BlockSpec((1, d), lambda i: (0, 0)),
            mod_spec,
            mod_spec,
            pl.BlockSpec((d, n), lambda i: (0, 0), pipeline_mode=pl.Buffered(1)),
        ],
        out_specs=pl.BlockSpec((TOK_TILE, n), lambda i: (i, 0)),
        compiler_params=_cparams(1, 56),
        name="in_proj",
    )(z, norm_w, sc, sh, w_in)


def _ret_kernel(lg_ref, q_ref, k_ref, v_ref, cos_ref, sin_ref, *rest, reverse, chunk):
    if reverse:
        of_ref, g_ref, nw_ref, o_ref, state_ref = rest
    else:
        o_ref, state_ref = rest
    t = pl.program_id(1)

    @pl.when(t == 0)
    def _():
        state_ref[...] = jnp.zeros_like(state_ref)

    ii = lax.broadcasted_iota(I32, (chunk, chunk), 0)
    jj = lax.broadcasted_iota(I32, (chunk, chunk), 1)
    rel = ((jj - ii) if reverse else (ii - jj)).astype(F32)
    idx = lax.broadcasted_iota(I32, (chunk, 1), 0).astype(F32)
    cos2 = cos_ref[...]
    sin2 = sin_ref[...]
    scale = RET_DIM ** -0.5
    direction = 1 if reverse else 0

    def rope(x):
        return x * cos2 + pltpu.roll(x, RET_DIM // 2, 1) * sin2

    for h in range(RET_HEADS):
        sl = slice(h * RET_DIM, (h + 1) * RET_DIM)
        lg = lg_ref[direction, h]
        dmask = jnp.where(rel >= 0.0, jnp.exp(lg * jnp.maximum(rel, 0.0)), 0.0)
        if reverse:
            qdec = jnp.exp(lg * (chunk - idx))
            kdec = jnp.exp(lg * idx)
        else:
            qdec = jnp.exp(lg * (idx + 1.0))
            kdec = jnp.exp(lg * (chunk - 1.0 - idx))
        cdec = jnp.exp(jnp.full((1, 1), lg * chunk, F32))
        q = rope(q_ref[:, sl].astype(F32))
        k = rope(k_ref[:, sl].astype(F32)) * scale
        v = v_ref[:, sl]
        qb = q.astype(BF16)
        scores = _dot_nt(qb, k.astype(BF16)) * dmask
        inner = _dot(scores.astype(BF16), v)
        s_old = state_ref[h]
        cross = _dot(qb, s_old.astype(BF16)) * qdec
        state_ref[h] = s_old * cdec + _dot_tn((k * kdec).astype(BF16), v)
        o = inner + cross
        if reverse:
            o = o + of_ref[:, sl]
            mu = jnp.mean(o, axis=-1, keepdims=True)
            oc = o - mu
            var = jnp.mean(oc * oc, axis=-1, keepdims=True)
            on = oc * lax.rsqrt(var + EPS) * nw_ref[:, sl]
            o_ref[:, sl] = (_silu(g_ref[:, sl].astype(F32)) * on).astype(BF16)
        else:
            o_ref[:, sl] = o


def _retention(proj3, log_gamma, cos2, sin2, norm_w, ctx_len):
    b, s, _ = proj3.shape
    c = RET_CHUNK
    nch = s // c
    nctx = ctx_len // c

    def fwd_chunk(t):
        return t

    def bwd_chunk(t):
        return jnp.where(t < nctx, nctx - 1 - t, nch - 1 + nctx - t)

    def specs(chunk_of):
        col = lambda j: pl.BlockSpec((None, c, RET_W), lambda bi, t, j=j: (bi, chunk_of(t), j))
        tab = pl.BlockSpec((c, RET_DIM), lambda bi, t: (chunk_of(t), 0))
        return col, tab

    smem = pl.BlockSpec(memory_space=pltpu.SMEM)
    scratch = [pltpu.VMEM((RET_HEADS, RET_DIM, RET_DIM), F32)]

    col, tab = specs(fwd_chunk)
    o_f = pl.pallas_call(
        functools.partial(_ret_kernel, reverse=False, chunk=c),
        out_shape=jax.ShapeDtypeStruct((b, s, RET_W), F32),
        grid=(b, nch),
        in_specs=[smem, col(0), col(1), col(2), tab, tab],
        out_specs=col(0),
        scratch_shapes=scratch,
        compiler_params=_cparams(2),
        name="retention_fwd",
    )(log_gamma, proj3, proj3, proj3, cos2, sin2)

    col, tab = specs(bwd_chunk)
    return pl.pallas_call(
        functools.partial(_ret_kernel, reverse=True, chunk=c),
        out_shape=jax.ShapeDtypeStruct((b, s, RET_W), BF16),
        grid=(b, nch),
        in_specs=[smem, col(0), col(1), col(2), tab, tab, col(0), col(3),
                  pl.BlockSpec((1, RET_W), lambda bi, t: (0, 0))],
        out_specs=col(0),
        scratch_shapes=scratch,
        compiler_params=_cparams(2),
        name="retention_bwd",
    )(log_gamma, proj3, proj3, proj3, cos2, sin2, o_f, proj3, norm_w)


def _dprep_kernel(q_ref, k_ref, v_ref, qw_ref, kw_ref, c_ref, s1_ref, s2_ref, bd_ref,
                  q0t_ref, q1t_ref, kk_ref, vt_ref):
    cc = c_ref[...]
    s1 = s1_ref[...]
    s2 = s2_ref[...]
    bd = bd_ref[...]
    rows = q_ref.shape[0]
    first = lax.broadcasted_iota(I32, (LANES, rows), 0) < DIFF_QK
    quarter = DIFF_QK // 2

    def prep(x, w):
        x = x.astype(F32)
        hi, lo = _split(x * x)
        ss = _dot(hi, bd) + _dot(lo, bd)
        xn = x * lax.rsqrt(ss * (1.0 / DIFF_QK) + EPS) * w
        return xn * cc + pltpu.roll(xn, LANES - quarter, 1) * s1 + pltpu.roll(xn, quarter, 1) * s2

    for h in range(DIFF_HEADS):
        sl = slice(h * LANES, (h + 1) * LANES)
        qt = (prep(q_ref[:, sl], qw_ref[...]) * (DIFF_QK ** -0.5 * LOG2E)).T
        q0t_ref[sl, :] = jnp.where(first, qt, 0.0).astype(BF16)
        q1t_ref[sl, :] = jnp.where(first, 0.0, qt).astype(BF16)
        kk_ref[:, sl] = prep(k_ref[:, sl], kw_ref[...]).astype(BF16)
        vt_ref[sl, :] = v_ref[:, sl].astype(F32).T.astype(BF16)


def _diff_prep(proj, qnorm_w, knorm_w, dc, ds1, ds2, batch, seq):
    t, _ = proj.shape
    tiles_per_seq = seq // TOK_TILE
    qcol = (4 * RET_W) // DIFF_QK_W
    blockdiag = jnp.kron(jnp.eye(2, dtype=F32), jnp.ones((DIFF_QK, DIFF_QK), F32)).astype(BF16)
    tab = pl.BlockSpec((TOK_TILE, LANES), lambda i: (i % tiles_per_seq, 0))
    row = pl.BlockSpec((1, LANES), lambda i: (0, 0))
    col = lambda j: pl.BlockSpec((TOK_TILE, DIFF_QK_W), lambda i, j=j: (i, j))
    tr = pl.BlockSpec((None, DIFF_QK_W, TOK_TILE),
                      lambda i: (i // tiles_per_seq, 0, i % tiles_per_seq))
    shp_t = jax.ShapeDtypeStruct((batch, DIFF_QK_W, seq), BF16)
    return pl.pallas_call(
        _dprep_kernel,
        out_shape=(shp_t, shp_t, jax.ShapeDtypeStruct((t, DIFF_QK_W), BF16), shp_t),
        grid=(t // TOK_TILE,),
        in_specs=[col(qcol), col(qcol + 1), col(qcol + 2), row, row, tab, tab, tab,
                  pl.BlockSpec((LANES, LANES), lambda i: (0, 0))],
        out_specs=(tr, tr, col(0), tr),
        compiler_params=_cparams(1),
        name="diff_prep",
    )(proj, proj, proj, jnp.tile(qnorm_w, 2)[None], jnp.tile(knorm_w, 2)[None], dc, ds1, ds2,
      blockdiag)


def _flash_kernel(lam_ref, q0t_ref, q1t_ref, k_ref, vt_ref, sw_ref, o_ref, m_s, acc_s, s_s, *,
                  heads, nctx_tiles, ctx_kv, all_kv, lambda_init):
    i = pl.program_id(2)
    nkv = jnp.where(i < nctx_tiles, ctx_kv, all_kv)
    m_s[...] = jnp.full(m_s.shape, -jnp.inf, F32)
    acc_s[...] = jnp.zeros(acc_s.shape, F32)
    ones = jnp.ones((BF16_SUBLANES, ATT_TK), BF16)

    def scores(j, buf):
        off = pl.multiple_of(j * ATT_TK, ATT_TK)
        for h in range(heads):
            sl = slice(h * DIFF_V, (h + 1) * DIFF_V)
            kt = k_ref[pl.ds(off, ATT_TK), sl]
            for comp, qt_ref in enumerate((q0t_ref, q1t_ref)):
                s_s[buf, 2 * h + comp] = _dot(kt, qt_ref[sl, :])

    def softmax_pv(j, buf):
        off = pl.multiple_of(j * ATT_TK, ATT_TK)
        for h in range(heads):
            sl = slice(h * DIFF_V, (h + 1) * DIFF_V)
            vt = jnp.concatenate([vt_ref[sl, pl.ds(off, ATT_TK)], ones], axis=0)
            for comp in range(2):
                c = 2 * h + comp
                st = s_s[buf, c]
                m_old = m_s[c]
                m_new = jnp.maximum(m_old, jnp.max(st, axis=0, keepdims=True))
                alpha = jnp.exp2(m_old - m_new)
                pt = jnp.exp2(st - m_new)
                acc_s[c] = alpha * acc_s[c] + _dot(vt, pt.astype(BF16))
                m_s[c] = m_new

    scores(0, 0)

    def pairs(first, n_pairs):
        for p in range(n_pairs):
            scores(first + 2 * p + 1, 1)
            softmax_pv(first + 2 * p, 0)
            scores(first + 2 * p + 2, 0)
            softmax_pv(first + 2 * p + 1, 1)

    group = 2 * ATT_UNROLL_PAIRS
    n_groups = (nkv - 1) // group

    def group_body(g, carry):
        pairs(group * g, ATT_UNROLL_PAIRS)
        return carry

    def pair_body(p, carry):
        pairs(group * n_groups + 2 * p, 1)
        return carry

    lax.fori_loop(0, n_groups, group_body, 0)
    lax.fori_loop(0, ((nkv - 1) % group) // 2, pair_body, 0)
    softmax_pv(nkv - 1, 0)
    lv = lam_ref[...]
    lam = (jnp.exp(jnp.sum(lv[0:1] * lv[1:2], axis=-1, keepdims=True))
           - jnp.exp(jnp.sum(lv[2:3] * lv[3:4], axis=-1, keepdims=True)) + lambda_init)
    for h in range(heads):
        sl = slice(h * DIFF_V, (h + 1) * DIFF_V)
        o0 = acc_s[2 * h, 0:DIFF_V] / acc_s[2 * h, DIFF_V:DIFF_V + 1]
        o1 = acc_s[2 * h + 1, 0:DIFF_V] / acc_s[2 * h + 1, DIFF_V:DIFF_V + 1]
        d = (o0 - lam * o1).T
        ms = jnp.mean(d * d, axis=-1, keepdims=True)
        o_ref[:, sl] = (d * lax.rsqrt(ms + EPS) * sw_ref[...] * (1.0 - lambda_init)).astype(BF16)


def _diff_attention(q0t, q1t, kk, vt, lam_vecs, subln_w, ctx_len, lambda_init):
    b, _, s = q0t.shape
    kk = kk.reshape(b, s, DIFF_QK_W)
    hb = ATT_HEADS
    w = hb * DIFF_V
    assert (s // ATT_TK) % 2 == 1 and (ctx_len // ATT_TK) % 2 == 1
    qspec = pl.BlockSpec((None, w, ATT_TQ), lambda bi, h, i: (bi, h, i))
    return pl.pallas_call(
        functools.partial(_flash_kernel, heads=hb, nctx_tiles=ctx_len // ATT_TQ,
                          ctx_kv=ctx_len // ATT_TK, all_kv=s // ATT_TK, lambda_init=lambda_init),
        out_shape=jax.ShapeDtypeStruct((b, s, DIFF_W), BF16),
        grid=(b, DIFF_HEADS // hb, s // ATT_TQ),
        in_specs=[
            pl.BlockSpec((4, DIFF_QK), lambda bi, h, i: (0, 0)),
            qspec, qspec,
            pl.BlockSpec((None, s, w), lambda bi, h, i: (bi, 0, h)),
            pl.BlockSpec((None, w, s), lambda bi, h, i: (bi, h, 0)),
            pl.BlockSpec((1, DIFF_V), lambda bi, h, i: (0, 0)),
        ],
        out_specs=pl.BlockSpec((None, ATT_TQ, w), lambda bi, h, i: (bi, i, h)),
        scratch_shapes=[pltpu.VMEM((2 * hb, 1, ATT_TQ), F32),
                        pltpu.VMEM((2 * hb, DIFF_V + BF16_SUBLANES, ATT_TQ), F32),
                        pltpu.VMEM((2, 2 * hb, ATT_TK, ATT_TQ), F32)],
        compiler_params=_cparams(3),
        name="diff_flash",
    )(lam_vecs, q0t, q1t, kk, vt, subln_w[None])


def _s5_kernel(u_ref, br_ref, bi_ref, cr_ref, ci_ref, tab_ref, *rest, reverse, ntiles):
    if reverse:
        yf_ref, d_ref, wg_ref, o_ref, xr_s, xi_s, car_s, cai_s = rest
    else:
        o_ref, xr_s, xi_s, car_s, cai_s = rest
    t = pl.program_id(1)

    @pl.when(t == 0)
    def _():
        car_s[...] = jnp.zeros_like(car_s)
        cai_s[...] = jnp.zeros_like(cai_s)

    u = u_ref[...]
    nq = S5_CH // LANES
    sw = S5_W // nq
    for q in range(nq):
        uq = u[:, q * LANES:(q + 1) * LANES]
        xr_s[:, q * sw:(q + 1) * sw] = _dot(uq, br_ref[q])
        xi_s[:, q * sw:(q + 1) * sw] = _dot(uq, bi_ref[q])

    def body(n, carry):
        cr, ci = carry
        tile = (ntiles - 1 - n) if reverse else n
        off = pl.multiple_of(tile * SUBLANES, SUBLANES)
        xr = xr_s[pl.ds(off, SUBLANES), :]
        xi = xi_s[pl.ds(off, SUBLANES), :]
        for lvl, k in enumerate((1, 2, 4)):
            pr = tab_ref[2 * lvl]
            pi = tab_ref[2 * lvl + 1]
            shift = (SUBLANES - k) if reverse else k
            sr = pltpu.roll(xr, shift, 0)
            si = pltpu.roll(xi, shift, 0)
            xr, xi = xr + pr * sr - pi * si, xi + pr * si + pi * sr
        ar = tab_ref[6]
        ai = tab_ref[7]
        xr, xi = xr + ar * cr - ai * ci, xi + ar * ci + ai * cr
        xr_s[pl.ds(off, SUBLANES), :] = xr
        xi_s[pl.ds(off, SUBLANES), :] = xi
        last = 0 if reverse else SUBLANES - 1
        return xr[last:last + 1], xi[last:last + 1]

    cr, ci = lax.fori_loop(0, ntiles, body, (car_s[...], cai_s[...]))
    car_s[...] = cr
    cai_s[...] = ci
    y = jnp.concatenate(
        [_dot(xr_s[:, q * sw:(q + 1) * sw].astype(BF16), cr_ref[q])
         + _dot(xi_s[:, q * sw:(q + 1) * sw].astype(BF16), ci_ref[q]) for q in range(nq)], axis=1)
    if reverse:
        y = y + yf_ref[...] + d_ref[...] * u.astype(F32)
        y = jax.nn.gelu(y, approximate=True)
        o_ref[...] = (y * jax.nn.sigmoid(_dot(y.astype(BF16), wg_ref[...]))).astype(BF16)
    else:
        o_ref[...] = y


def _s5_tables(a_re_log, a_im_ang, reverse):
    w = a_re_log.reshape(1, S5_W)
    th = a_im_ang.reshape(1, S5_W)
    row = jnp.arange(SUBLANES, dtype=F32)[:, None]
    tabs = []
    for k in (1, 2, 4):
        keep = (row <= SUBLANES - 1 - k) if reverse else (row >= k)
        mag = jnp.exp(w * k)
        tabs.append(jnp.where(keep, mag * jnp.cos(th * k), 0.0))
        tabs.append(jnp.where(keep, mag * jnp.sin(th * k), 0.0))
    e = (SUBLANES - row) if reverse else (row + 1.0)
    mag = jnp.exp(w * e)
    tabs.append(mag * jnp.cos(th * e))
    tabs.append(mag * jnp.sin(th * e))
    return jnp.stack(tabs)


def _s5_params(lam_re, lam_im, log_dt, b_re, b_im):
    dt = jnp.exp(log_dt)[:, None]
    wlog = lam_re * dt
    ang = lam_im * dt
    mag = jnp.exp(wlog)
    a_re = mag * jnp.cos(ang)
    a_im = mag * jnp.sin(ang)
    den = lam_re * lam_re + lam_im * lam_im
    nr = a_re - 1.0
    f_re = (nr * lam_re + a_im * lam_im) / den
    f_im = (a_im * lam_re - nr * lam_im) / den
    bb_re = f_re[..., None] * b_re - f_im[..., None] * b_im
    bb_im = f_re[..., None] * b_im + f_im[..., None] * b_re
    eye = jnp.eye(S5_GROUPS, dtype=F32)
    bmat_re = jnp.einsum('gph,gk->ghkp', bb_re, eye).reshape(S5_CH, S5_W)
    bmat_im = jnp.einsum('gph,gk->ghkp', bb_im, eye).reshape(S5_CH, S5_W)
    return wlog, ang, _diag_blocks(bmat_re).astype(BF16), _diag_blocks(bmat_im).astype(BF16)


def _diag_blocks(m):
    nq = S5_CH // LANES
    r, c = m.shape[0] // nq, m.shape[1] // nq
    return jnp.stack([m[q * r:(q + 1) * r, q * c:(q + 1) * c] for q in range(nq)])


def _s5_mixer(proj3, lam_re, lam_im, log_dt, b_re, b_im, c_re, c_im, d, w_glu, ctx_len):
    b, s, _ = proj3.shape
    tb = S5_BLOCK
    nb = s // tb
    nctx = ctx_len // tb
    ucol = (4 * RET_W + 2 * DIFF_QK_W + DIFF_W) // S5_CH
    eye = jnp.eye(S5_GROUPS, dtype=F32)
    cmat_re = _diag_blocks(jnp.einsum('ghp,gk->gpkh', c_re, eye).reshape(S5_W, S5_CH)).astype(BF16)
    cmat_im = _diag_blocks(-jnp.einsum('ghp,gk->gpkh', c_im, eye).reshape(S5_W, S5_CH)).astype(BF16)
    full = lambda shape: pl.BlockSpec(shape, lambda bi, t: tuple(0 for _ in shape))
    scratch = [pltpu.VMEM((tb, S5_W), F32), pltpu.VMEM((tb, S5_W), F32),
               pltpu.VMEM((1, S5_W), F32), pltpu.VMEM((1, S5_W), F32)]
    nq = S5_CH // LANES
    weights = [full((nq, LANES, S5_W // nq)), full((nq, LANES, S5_W // nq)),
               full((nq, S5_W // nq, LANES)), full((nq, S5_W // nq, LANES)),
               full((8, SUBLANES, S5_W))]

    def blk_f(t):
        return t

    def blk_b(t):
        return jnp.where(t < nctx, nctx - 1 - t, nb - 1 + nctx - t)

    wl, ang, bre, bim = _s5_params(lam_re[0], lam_im[0], log_dt[0], b_re, b_im)
    y_f = pl.pallas_call(
        functools.partial(_s5_kernel, reverse=False, ntiles=tb // SUBLANES),
        out_shape=jax.ShapeDtypeStruct((b, s, S5_CH), F32),
        grid=(b, nb),
        in_specs=[pl.BlockSpec((None, tb, S5_CH), lambda bi, t: (bi, blk_f(t), ucol))] + weights,
        out_specs=pl.BlockSpec((None, tb, S5_CH), lambda bi, t: (bi, blk_f(t), 0)),
        scratch_shapes=scratch,
        compiler_params=_cparams(2),
        name="s5_fwd",
    )(proj3, bre, bim, cmat_re, cmat_im, _s5_tables(wl, ang, False))

    wl, ang, bre, bim = _s5_params(lam_re[1], lam_im[1], log_dt[1], b_re, b_im)
    return pl.pallas_call(
        functools.partial(_s5_kernel, reverse=True, ntiles=tb // SUBLANES),
        out_shape=jax.ShapeDtypeStruct((b, s, S5_CH), BF16),
        grid=(b, nb),
        in_specs=[pl.BlockSpec((None, tb, S5_CH), lambda bi, t: (bi, blk_b(t), ucol))] + weights + [
            pl.BlockSpec((None, tb, S5_CH), lambda bi, t: (bi, blk_b(t), 0)),
            full((1, S5_CH)), full((S5_CH, S5_CH))],
        out_specs=pl.BlockSpec((None, tb, S5_CH), lambda bi, t: (bi, blk_b(t), 0)),
        scratch_shapes=scratch,
        compiler_params=_cparams(2),
        name="s5_bwd",
    )(proj3, bre, bim, cmat_re, cmat_im, _s5_tables(wl, ang, True), y_f, d[None],
      w_glu.astype(BF16))


def _neg_inf_like(x):
    return jnp.full(x.shape, -jnp.inf, x.dtype)


def _out_kernel(x_ref, ret_ref, dif_ref, s5_ref, wo_ref, g1_ref, nw_ref, sc_ref, sh_ref,
                rw_ref, rb_ref, tri_ref,
                x1_ref, h2_ref, h2p_ref, te_ref, tw_ref, rk_ref, cnt_ref, cnt_s):
    i = pl.program_id(0)

    @pl.when(i == 0)
    def _():
        cnt_s[...] = jnp.zeros_like(cnt_s)

    d = x_ref.shape[1]
    tm = x_ref.shape[0]
    mix = (_dot(ret_ref[...], wo_ref[0:RET_W, :])
           + _dot(dif_ref[...], wo_ref[RET_W:RET_W + DIFF_W, :])
           + _dot(s5_ref[...], wo_ref[RET_W + DIFF_W:, :]))
    x1 = x_ref[...] + g1_ref[...] * mix
    x1_ref[...] = x1
    ms = jnp.mean(x1 * x1, axis=-1, keepdims=True)
    h2 = x1 * lax.rsqrt(ms + EPS) * nw_ref[...]
    h2 = h2 * (1.0 + sc_ref[...]) + sh_ref[...]
    h2_ref[...] = h2.astype(BF16)
    h2p_ref[...] = _pack_halves(h2[:, :d // 2], h2[:, d // 2:])

    ng = N_EXPERT_GROUPS
    pg = N_EXPERTS // N_EXPERT_GROUPS
    logits = _dot3_nt(rw_ref[...], h2)
    scores = jax.nn.sigmoid(logits)
    sel3 = (scores + rb_ref[...]).reshape(ng, pg, tm)
    scores3 = scores.reshape(ng, pg, tm)
    in_grp = lax.broadcasted_iota(I32, (ng, pg, tm), 1).astype(F32)
    grp = lax.broadcasted_iota(I32, (ng, pg, tm), 0).astype(F32)
    eidx = grp * pg + in_grp
    gidx = lax.broadcasted_iota(I32, (ng, 1, tm), 0).astype(F32)

    m1 = jnp.max(sel3, axis=1, keepdims=True)
    first = jnp.min(jnp.where(sel3 == m1, in_grp, float(pg)), axis=1, keepdims=True)
    m2 = jnp.max(jnp.where(in_grp == first, -jnp.inf, sel3), axis=1, keepdims=True)
    rem = m1 + m2
    gsel = jnp.zeros((ng, 1, tm), F32)
    for _ in range(TOPK_GROUPS):
        mx = jnp.max(rem, axis=0, keepdims=True)
        fi = jnp.min(jnp.where(rem == mx, gidx, float(ng)), axis=0, keepdims=True)
        pick = gidx == fi
        gsel = jnp.where(pick, 1.0, gsel)
        rem = jnp.where(pick, -jnp.inf, rem)
    masked = jnp.where(gsel > 0.0, sel3, -jnp.inf)

    def red2(fn, x):
        return fn(fn(x, axis=1, keepdims=True), axis=0, keepdims=True)

    chosen = jnp.zeros((ng, pg, tm), F32)
    picks = []
    weights = []
    for _ in range(TOP_K):
        mx = red2(jnp.max, masked)
        fi = red2(jnp.min, jnp.where(masked == mx, eidx, float(N_EXPERTS)))
        pick = eidx == fi
        picks.append(fi)
        weights.append(red2(jnp.sum, jnp.where(pick, scores3, 0.0)))
        chosen = jnp.where(pick, 1.0, chosen)
        masked = jnp.where(pick, -jnp.inf, masked)
    wsum = weights[0]
    for w in weights[1:]:
        wsum = wsum + w
    inv = ROUTE_SCALE / wsum

    chosen2 = chosen.reshape(N_EXPERTS, tm)
    cum = _dot(chosen2.astype(BF16), tri_ref[...]) + cnt_s[...]
    cum3 = cum.reshape(ng, pg, tm)
    for k in range(TOP_K):
        te_ref[k:k + 1, :] = picks[k].reshape(1, tm).astype(I32)
        tw_ref[k:k + 1, :] = (weights[k] * inv).reshape(1, tm)
        rk = red2(jnp.sum, jnp.where(eidx == picks[k], cum3, 0.0))
        rk_ref[k:k + 1, :] = rk.reshape(1, tm).astype(I32)
    cnt_new = cnt_s[...] + jnp.sum(chosen2, axis=-1, keepdims=True)
    cnt_s[...] = cnt_new
    cnt_ref[...] = jnp.broadcast_to(cnt_new, cnt_ref.shape)


def _out_proj_route(z, ret, dif, s5, w_out, g1, norm_w, sc, sh, router_w, router_bias, mod_row,
                    src_tile, n_out_tiles):
    d = z.shape[1]
    tm = TOK_TILE
    t = n_out_tiles * tm
    tri = (jnp.arange(tm)[:, None] < jnp.arange(tm)[None, :]).astype(BF16)
    mod_spec = pl.BlockSpec((None, 1, d), lambda i: (mod_row(src_tile(i)), 0, 0))
    src = lambda w: pl.BlockSpec((tm, w), lambda i: (src_tile(i), 0))
    tok = lambda w: pl.BlockSpec((tm, w), lambda i: (i, 0))
    const = lambda shape: pl.BlockSpec(shape, lambda i: tuple(0 for _ in shape))
    lane_out = pl.BlockSpec((TOP_K, tm), lambda i: (0, i))
    return pl.pallas_call(
        _out_kernel,
        out_shape=(
            jax.ShapeDtypeStruct((t, d), F32),
            jax.ShapeDtypeStruct((t, d), BF16),
            jax.ShapeDtypeStruct((t, d // 2), U32),
            jax.ShapeDtypeStruct((TOP_K, t), I32),
            jax.ShapeDtypeStruct((TOP_K, t), F32),
            jax.ShapeDtypeStruct((TOP_K, t), I32),
            jax.ShapeDtypeStruct((N_EXPERTS, LANES), F32),
        ),
        grid=(t // tm,),
        in_specs=[src(d), src(RET_W), src(DIFF_W), src(S5_CH), const((d, d)),
                  mod_spec, const((1, d)), mod_spec, mod_spec,
                  const((N_EXPERTS, d)), const((N_EXPERTS, 1)), const((tm, tm))],
        out_specs=(tok(d), tok(d), tok(d // 2), lane_out, lane_out, lane_out,
                   const((N_EXPERTS, LANES))),
        scratch_shapes=[pltpu.VMEM((N_EXPERTS, 1), F32)],
        compiler_params=_cparams(1),
        name="out_proj_route",
    )(z, ret, dif, s5, w_out, g1, norm_w, sc, sh, router_w.T, router_bias[:, None], tri)


def _expert_kernel(te_ref, s0_ref, nv_ref, nt_ref, h2p_hbm, tok_hbm, dst_hbm, wg_ref, wu_ref, wd_ref,
                   ybuf_hbm, wg_s, wu_s, wd_s, xs_s, y_s, tokw, dstw, sem_g, sem_s, sem_w):
    i = pl.program_id(0)
    tm = xs_s.shape[1]
    half = xs_s.shape[2]
    n_valid = nt_ref[0]
    trash = ybuf_hbm.shape[0] - EXP_BUFS * tm

    def win_base(step):
        return (s0_ref[step + 1] // WIN_ALIGN) * WIN_ALIGN

    def window_copies(step, slot):
        base = pl.multiple_of(win_base(step), WIN_ALIGN)
        return (pltpu.make_async_copy(tok_hbm.at[pl.ds(base, WIN)], tokw.at[slot], sem_w.at[slot]),
                pltpu.make_async_copy(dst_hbm.at[pl.ds(base, WIN)], dstw.at[slot], sem_w.at[slot]))

    def issue_gathers(tile, step, slot):
        buf = tile % EXP_BUFS
        d = s0_ref[tile + 2] - win_base(step)
        for r in range(tm):
            pltpu.make_async_copy(h2p_hbm.at[pl.ds(tokw[slot, d + r], 1), :],
                                  xs_s.at[buf, pl.ds(r, 1), :], sem_g.at[buf]).start(priority=r % 2)

    def issue_scatters(tile, step, slot):
        buf = (tile + EXP_BUFS) % EXP_BUFS
        d = s0_ref[tile + 2] - win_base(step)
        nv = nv_ref[tile + 2]
        for r in range(tm):
            dest = jnp.where(r < nv, dstw[slot, d + r], trash + buf * tm + r)
            pltpu.make_async_copy(y_s.at[buf, pl.ds(r, 1), :],
                                  ybuf_hbm.at[pl.ds(dest, 1), :], sem_s.at[buf]).start(priority=r % 2)

    def wait_gathers(buf):
        pltpu.make_async_copy(xs_s.at[buf], xs_s.at[buf], sem_g.at[buf]).wait()

    def wait_scatters(buf):
        pltpu.make_async_copy(y_s.at[buf], y_s.at[buf], sem_s.at[buf]).wait()

    @pl.when(i < n_valid)
    def _step():
        cur = i % EXP_BUFS
        slot = i % 2

        @pl.when(i == 0)
        def _prologue():
            y_s[...] = jnp.zeros_like(y_s)
            for b in range(EXP_BUFS):
                cp = pltpu.make_async_copy(y_s.at[b], ybuf_hbm.at[pl.ds(trash + b * tm, tm), :],
                                           sem_s.at[b])
                cp.start()
                cp.wait()
            for cp in window_copies(0, 0):
                cp.start()
            for cp in window_copies(0, 0):
                cp.wait()
            issue_gathers(0, 0, 0)
            issue_gathers(1, 0, 0)
            issue_scatters(-2, 0, 0)

        @pl.when(i > 0)
        def _():
            for cp in window_copies(i, slot):
                cp.wait()

        for cp in window_copies(i + 1, 1 - slot):
            cp.start()
        wait_gathers(cur)

        @pl.when(i > 0)
        def _():
            wait_scatters(cur)

        @pl.when((i == 0) | (te_ref[i] != te_ref[jnp.maximum(i - 1, 0)]))
        def _():
            wg_s[...] = wg_ref[...].astype(BF16)
            wu_s[...] = wu_ref[...].astype(BF16)
            wd_s[...] = wd_ref[...].astype(BF16)

        issue_gathers(i + 2, i, slot)
        issue_scatters(i - 1, i, slot)
        lo, hi = _unpack_halves(xs_s[cur])
        lo = lo.astype(BF16)
        hi = hi.astype(BF16)
        g = _dot(lo, wg_s[0:half, :]) + _dot(hi, wg_s[half:, :])
        u = _dot(lo, wu_s[0:half, :]) + _dot(hi, wu_s[half:, :])
        y = _dot((_silu(g) * u).astype(BF16), wd_s[...])
        y_s[cur] = _pack_halves(y[:, :half], y[:, half:])

        @pl.when(i == n_valid - 1)
        def _epilogue():
            issue_scatters(i, i, slot)
            for b in range(EXP_BUFS):
                wait_scatters(b)
            wait_gathers((i + 1) % EXP_BUFS)
            wait_gathers((i + 2) % EXP_BUFS)
            for cp in window_copies(i + 1, 1 - slot):
                cp.wait()


def _experts(h2p, tok_sorted, dst_sorted, tile_expert, tile_s0, tile_nv, n_valid, w_gate, w_up,
             w_down, layer, n_rows_out):
    half = h2p.shape[1]
    _, e, d, f = w_gate.shape
    tm = EXP_TILE
    n_tiles = tile_expert.shape[0]
    hbm = pl.BlockSpec(memory_space=pl.ANY)
    wspec = lambda a, b: pl.BlockSpec((None, None, a, b),
                                      lambda i, te, s0, nv, nt: (layer, te[i], 0, 0))
    grid_spec = pltpu.PrefetchScalarGridSpec(
        num_scalar_prefetch=4,
        grid=(n_tiles,),
        in_specs=[hbm, hbm, hbm, wspec(d, f), wspec(d, f), wspec(f, d)],
        out_specs=hbm,
        scratch_shapes=[pltpu.VMEM((d, f), BF16), pltpu.VMEM((d, f), BF16), pltpu.VMEM((f, d), BF16),
                        pltpu.VMEM((EXP_BUFS, tm, half), U32), pltpu.VMEM((EXP_BUFS, tm, half), U32),
                        pltpu.SMEM((2, WIN), I32), pltpu.SMEM((2, WIN), I32),
                        pltpu.SemaphoreType.DMA((EXP_BUFS,)), pltpu.SemaphoreType.DMA((EXP_BUFS,)),
                        pltpu.SemaphoreType.DMA((2,))],
    )
    return pl.pallas_call(
        _expert_kernel,
        out_shape=jax.ShapeDtypeStruct((n_rows_out + EXP_BUFS * tm, half), U32),
        grid_spec=grid_spec,
        compiler_params=_cparams(1, 56),
        name="moe_experts",
    )(tile_expert, tile_s0, tile_nv, n_valid, h2p, tok_sorted, dst_sorted, w_gate, w_up, w_down)


def _combine_kernel(y_ref, x1_ref, h2_ref, w_ref, g2_ref, sg_ref, su_ref, sd_ref, o_ref):
    tm = x1_ref.shape[0]
    half = y_ref.shape[1]
    h2 = h2_ref[...]
    hid = _silu(_dot(h2, sg_ref[...])) * _dot(h2, su_ref[...])
    shared = _dot(hid.astype(BF16), sd_ref[...])

    w = w_ref[...]
    acc_lo = None
    acc_hi = None
    for k in range(TOP_K):
        lo, hi = _unpack_halves(y_ref[k * tm:(k + 1) * tm, :])
        wk = w[:, k:k + 1]
        acc_lo = wk * lo if acc_lo is None else acc_lo + wk * lo
        acc_hi = wk * hi if acc_hi is None else acc_hi + wk * hi
    g2 = g2_ref[...]
    o_ref[:, :half] = x1_ref[:, :half] + g2[:, :half] * (acc_lo + shared[:, :half])
    o_ref[:, half:] = x1_ref[:, half:] + g2[:, half:] * (acc_hi + shared[:, half:])


def _combine(ybuf, x1, h2, tw, g2, s_gate, s_up, s_down, mod_row):
    t, d = x1.shape
    tm = TOK_TILE
    f = s_gate.shape[1]
    tok = lambda w: pl.BlockSpec((tm, w), lambda i: (i, 0))
    const = lambda shape: pl.BlockSpec(shape, lambda i: tuple(0 for _ in shape))
    return pl.pallas_call(
        _combine_kernel,
        out_shape=jax.ShapeDtypeStruct((t, d), F32),
        grid=(t // tm,),
        in_specs=[pl.BlockSpec((TOP_K * tm, d // 2), lambda i: (i, 0)),
                  tok(d), tok(d), tok(TOP_K),
                  pl.BlockSpec((None, 1, d), lambda i: (mod_row(i), 0, 0)),
                  const((d, f)), const((d, f)), const((f, d))],
        out_specs=tok(d),
        compiler_params=_cparams(1),
        name="moe_combine",
    )(ybuf, x1, h2, tw, g2, s_gate, s_up, s_down)


def _moe(x1, h2, h2p, te, tw, rk, cnt, g2, w_gate, w_up, w_down, layer, s_gate, s_up, s_down,
         mod_row):
    t = x1.shape[0]
    n = t * TOP_K
    tm = EXP_TILE
    n_tiles = n // tm + N_EXPERTS
    counts = cnt[:, 0].astype(I32)
    padded = ((counts + tm - 1) // tm) * tm
    pad_end = jnp.cumsum(padded)
    pad_off = pad_end - padded
    off = jnp.cumsum(counts) - counts
    experts = jnp.arange(N_EXPERTS, dtype=I32)
    off_of = jnp.sum(jnp.where(te[:, :, None] == experts, off, 0), axis=-1)
    key = (off_of + rk).T.reshape(-1)
    order = jnp.argsort(key).astype(I32)
    tok_sorted = order // TOP_K
    k_sorted = order % TOP_K
    dst_sorted = ((tok_sorted // TOK_TILE) * (TOP_K * TOK_TILE) + k_sorted * TOK_TILE
                  + tok_sorted % TOK_TILE)
    tok_sorted = jnp.pad(tok_sorted, (0, WIN))
    dst_sorted = jnp.pad(dst_sorted, (0, WIN))

    tile_start = jnp.arange(n_tiles, dtype=I32) * tm
    tile_expert = jnp.minimum(jnp.sum((tile_start[:, None] >= pad_end[None, :]).astype(I32), axis=-1),
                              N_EXPERTS - 1)
    is_e = tile_expert[:, None] == experts
    row0 = tile_start - jnp.sum(jnp.where(is_e, pad_off, 0), axis=-1)
    valid = tile_start < pad_end[-1]
    s0 = jnp.where(valid, jnp.sum(jnp.where(is_e, off, 0), axis=-1) + row0, n)
    nv = jnp.where(valid, jnp.clip(jnp.sum(jnp.where(is_e, counts, 0), axis=-1) - row0, 0, tm), 0)
    tile_s0 = jnp.concatenate([jnp.zeros((2,), I32), s0, jnp.full((3,), n, I32)]).astype(I32)
    tile_nv = jnp.concatenate([jnp.zeros((2,), I32), nv, jnp.zeros((3,), I32)]).astype(I32)
    n_valid = (pad_end[-1:] // tm).astype(I32)
    ybuf = _experts(h2p, tok_sorted, dst_sorted, tile_expert, tile_s0, tile_nv, n_valid,
                    w_gate, w_up, w_down, layer, n)
    return _combine(ybuf, x1, h2, tw.T, g2, s_gate, s_up, s_down, mod_row)


def _rope_angles(rows, head_dim):
    axis_dim = head_dim // 2
    inv_freq = ROPE_BASE ** (-jnp.arange(0, axis_dim, 2, dtype=F32) / axis_dim)
    row = jnp.repeat(jnp.arange(rows, dtype=F32), GRID_W)
    col = jnp.tile(jnp.arange(GRID_W, dtype=F32), rows)
    return jnp.concatenate([row[:, None] * inv_freq, col[:, None] * inv_freq], axis=-1)


def _rope_tables(n, ctx_len):
    rows = n // GRID_W
    ang = _rope_angles(rows, RET_DIM)
    cos, sin = jnp.cos(ang), jnp.sin(ang)
    ret_cos = jnp.concatenate([cos, cos], axis=-1)
    ret_sin = jnp.concatenate([-sin, sin], axis=-1)
    ang = _rope_angles(rows, DIFF_QK)
    cos, sin = jnp.cos(ang), jnp.sin(ang)
    zero = jnp.zeros_like(sin)
    dc = jnp.tile(jnp.concatenate([cos, cos], axis=-1), (1, 2))
    ds1 = jnp.tile(jnp.concatenate([-sin, zero], axis=-1), (1, 2))
    ds2 = jnp.tile(jnp.concatenate([zero, sin], axis=-1), (1, 2))

    def with_ctx(tab, fill):
        return jnp.concatenate([jnp.full((ctx_len, tab.shape[1]), fill, F32), tab], axis=0)

    return (with_ctx(ret_cos, 1.0), with_ctx(ret_sin, 0.0),
            with_ctx(dc, 1.0), with_ctx(ds1, 0.0), with_ctx(ds2, 0.0))


def kernel(x, c, ctx, c_ctx, w_ada, b_ada, norm1_w, norm2_w, w_in, ret_decay_logit, ret_norm_w,
           diff_qnorm_w, diff_knorm_w, diff_lambda, diff_subln_w, s5_lambda_re, s5_lambda_im,
           s5_log_dt, s5_b_re, s5_b_im, s5_c_re, s5_c_im, s5_d, s5_w_glu, w_out, router_w,
           router_bias, exp_w_gate, exp_w_up, exp_w_down, shared_w_gate, shared_w_up, shared_w_down):
    out_dtype = x.dtype
    batch, n, d = x.shape
    ctx_len = ctx.shape[1]
    depth = w_ada.shape[0]
    s = ctx_len + n
    t = batch * s
    assert ctx_len % TOK_TILE == 0 and n % TOK_TILE == 0 and n % GRID_W == 0
    assert batch + 1 <= SUBLANES
    tiles_per_seq = s // TOK_TILE
    ctx_tiles = ctx_len // TOK_TILE

    def mod_row(i):
        return jnp.where(i % tiles_per_seq < ctx_tiles, batch, i // tiles_per_seq)

    z = jnp.concatenate([ctx.astype(F32), x.astype(F32)], axis=1).reshape(t, d)
    cond = jnp.zeros((SUBLANES, d), F32).at[:batch].set(c.astype(F32)).at[batch].set(c_ctx.astype(F32))
    mod = _adaln(cond, w_ada.astype(F32), b_ada.astype(F32))
    ret_cos, ret_sin, dc, ds1, ds2 = _rope_tables(n, ctx_len)

    for l in range(depth):
        lambda_init = 0.8 - 0.6 * math.exp(-0.3 * l)
        m = mod[l].reshape(SUBLANES, N_MOD, 1, d)
        sh1, sc1, g1, sh2, sc2, g2 = (m[:, j] for j in range(N_MOD))
        proj = _in_proj(z, norm1_w[l][None], sc1, sh1, w_in[l].astype(BF16), mod_row)
        proj3 = proj.reshape(batch, s, proj.shape[1])
        log_gamma = jax.nn.log_sigmoid(ret_decay_logit[l].astype(F32))
        ret = _retention(proj3, log_gamma, ret_cos, ret_sin, ret_norm_w[l][None], ctx_len)
        q0t, q1t, kk, vt = _diff_prep(proj, diff_qnorm_w[l], diff_knorm_w[l], dc, ds1, ds2,
                                      batch, s)
        dif = _diff_attention(q0t, q1t, kk, vt, diff_lambda[l], diff_subln_w[l], ctx_len,
                              lambda_init)
        s5 = _s5_mixer(proj3, s5_lambda_re[l], s5_lambda_im[l], s5_log_dt[l], s5_b_re[l],
                       s5_b_im[l], s5_c_re[l], s5_c_im[l], s5_d[l], s5_w_glu[l], ctx_len)
        if l < depth - 1:
            src_tile = lambda i: i
            n_out_tiles = t // TOK_TILE
        else:
            lat_tiles = tiles_per_seq - ctx_tiles
            src_tile = lambda i: i + (i // lat_tiles + 1) * ctx_tiles
            n_out_tiles = batch * lat_tiles
        x1, h2, h2p, te, tw, rk, cnt = _out_proj_route(
            z, ret.reshape(t, RET_W), dif.reshape(t, DIFF_W), s5.reshape(t, S5_CH),
            w_out[l].astype(BF16), g1, norm2_w[l][None], sc2, sh2, router_w[l], router_bias[l],
            mod_row, src_tile, n_out_tiles)
        z = _moe(x1, h2, h2p, te, tw, rk, cnt, g2, exp_w_gate.astype(F32),
                 exp_w_up.astype(F32), exp_w_down.astype(F32), l,
                 shared_w_gate[l].astype(BF16), shared_w_up[l].astype(BF16),
                 shared_w_down[l].astype(BF16), lambda i, f=src_tile: mod_row(f(i)))
    return z.reshape(batch, n, d).astype(out_dtype)
```

```python
import functools
import math

import jax
import jax.numpy as jnp
from jax import lax
from jax.experimental import pallas as pl
from jax.experimental.pallas import tpu as pltpu

F32 = jnp.float32
BF16 = jnp.bfloat16
U32 = jnp.uint32
I32 = jnp.int32

GRID_W = 64
N_MOD = 6
EPS = 1e-6
ROPE_BASE = 10000.0
RET_HEADS = 4
RET_DIM = 128
RET_W = RET_HEADS * RET_DIM
DIFF_HEADS = 8
DIFF_QK = 64
DIFF_V = 2 * DIFF_QK
DIFF_QK_W = DIFF_HEADS * 2 * DIFF_QK
DIFF_W = DIFF_HEADS * DIFF_V
S5_GROUP = 16
S5_CH = 512
S5_GROUPS = S5_CH // S5_GROUP
S5_STATE = 64
S5_W = S5_GROUPS * S5_STATE
N_EXPERTS = 64
EXPERT_FF = 512
TOP_K = 8
N_EXPERT_GROUPS = 8
TOPK_GROUPS = 4
ROUTE_SCALE = 2.5

V7X_VMEM_BYTES = 64 * 1024 * 1024
LANES = 128
SUBLANES = 8
BF16_SUBLANES = 16
TOK_TILE = 256
RET_CHUNK = 256
ATT_TQ = 256
ATT_TK = 256
ATT_HEADS = 4
ATT_UNROLL_PAIRS = 2
LOG2E = math.log2(math.e)
S5_BLOCK = 256
EXP_TILE = 256
ADA_TN = 1024
EXP_BUFS = 3
WIN = 2048
WIN_ALIGN = 1024


def _cparams(n_axes, vmem_mb=48):
    return pltpu.CompilerParams(
        dimension_semantics=("arbitrary",) * n_axes,
        vmem_limit_bytes=vmem_mb * 1024 * 1024,
    )


def _dot(a, b):
    return jnp.dot(a, b, preferred_element_type=F32)


def _dot_nt(a, b):
    return lax.dot_general(a, b, (((1,), (1,)), ((), ())), preferred_element_type=F32)


def _dot_tn(a, b):
    return lax.dot_general(a, b, (((0,), (0,)), ((), ())), preferred_element_type=F32)


def _split(x):
    hi = x.astype(BF16)
    lo = (x - hi.astype(F32)).astype(BF16)
    return hi, lo


def _dot3(a, b):
    ah, al = _split(a)
    bh, bl = _split(b)
    return _dot(ah, bh) + _dot(al, bh) + _dot(ah, bl)


def _dot3_nt(a, b):
    ah, al = _split(a)
    bh, bl = _split(b)
    return _dot_nt(ah, bh) + _dot_nt(al, bh) + _dot_nt(ah, bl)


def _silu(x):
    return x * jax.nn.sigmoid(x)


def _pack_halves(lo, hi):
    lo_b = pltpu.bitcast(lo.astype(BF16).astype(F32), U32) >> 16
    hi_b = pltpu.bitcast(hi.astype(BF16).astype(F32), U32) & jnp.uint32(0xFFFF0000)
    return hi_b | lo_b


def _unpack_halves(p):
    lo = pltpu.bitcast(p << 16, F32)
    hi = pltpu.bitcast(p & jnp.uint32(0xFFFF0000), F32)
    return lo, hi


def _ada_kernel(cond_ref, w_ref, b_ref, o_ref):
    o_ref[...] = _dot3(_silu(cond_ref[...]), w_ref[...]) + b_ref[...]


def _adaln(cond, w_ada, b_ada):
    n_layers, d, n = w_ada.shape
    rows = cond.shape[0]
    return pl.pallas_call(
        _ada_kernel,
        out_shape=jax.ShapeDtypeStruct((n_layers, rows, n), F32),
        grid=(n_layers, n // ADA_TN),
        in_specs=[
            pl.BlockSpec((rows, d), lambda l, j: (0, 0)),
            pl.BlockSpec((None, d, ADA_TN), lambda l, j: (l, 0, j)),
            pl.BlockSpec((None, 1, ADA_TN), lambda l, j: (l, 0, j)),
        ],
        out_specs=pl.BlockSpec((None, rows, ADA_TN), lambda l, j: (l, 0, j)),
        compiler_params=_cparams(2),
        name="adaln",
    )(cond, w_ada, b_ada.reshape(n_layers, 1, n))


def _proj_kernel(x_ref, nw_ref, sc_ref, sh_ref, w_ref, o_ref):
    x = x_ref[...]
    ms = jnp.mean(x * x, axis=-1, keepdims=True)
    h = x * lax.rsqrt(ms + EPS) * nw_ref[...]
    h = h * (1.0 + sc_ref[...]) + sh_ref[...]
    o_ref[...] = _dot(h.astype(BF16), w_ref[...]).astype(BF16)


def _in_proj(z, norm_w, sc, sh, w_in, mod_row):
    t, d = z.shape
    n = w_in.shape[1]
    mod_spec = pl.BlockSpec((None, 1, d), lambda i: (mod_row(i), 0, 0))
    return pl.pallas_call(
        _proj_kernel,
        out_shape=jax.ShapeDtypeStruct((t, n), BF16),
        grid=(t // TOK_TILE,),
        in_specs=[
            pl.BlockSpec((TOK_TILE, d), lambda i: (i, 0)),
            pl.BlockSpec((1, d), lambda i: (0, 0)),
            mod_spec,
            mod_spec,
            pl.BlockSpec((d, n), lambda i: (0, 0), pipeline_mode=pl.Buffered(1)),
        ],
        out_specs=pl.BlockSpec((TOK_TILE, n), lambda i: (i, 0)),
        compiler_params=_cparams(1, 56),
        name="in_proj",
    )(z, norm_w, sc, sh, w_in)


def _ret_kernel(lg_ref, q_ref, k_ref, v_ref, cos_ref, sin_ref, *rest, reverse, chunk):
    if reverse:
        of_ref, g_ref, nw_ref, o_ref, state_ref = rest
    else:
        o_ref, state_ref = rest
    t = pl.program_id(1)

    @pl.when(t == 0)
    def _():
        state_ref[...] = jnp.zeros_like(state_ref)

    ii = lax.broadcasted_iota(I32, (chunk, chunk), 0)
    jj = lax.broadcasted_iota(I32, (chunk, chunk), 1)
    rel = ((jj - ii) if reverse else (ii - jj)).astype(F32)
    idx = lax.broadcasted_iota(I32, (chunk, 1), 0).astype(F32)
    cos2 = cos_ref[...]
    sin2 = sin_ref[...]
    scale = RET_DIM ** -0.5
    direction = 1 if reverse else 0

    def rope(x):
        return x * cos2 + pltpu.roll(x, RET_DIM // 2, 1) * sin2

    for h in range(RET_HEADS):
        sl = slice(h * RET_DIM, (h + 1) * RET_DIM)
        lg = lg_ref[direction, h]
        dmask = jnp.where(rel >= 0.0, jnp.exp(lg * jnp.maximum(rel, 0.0)), 0.0)
        if reverse:
            qdec = jnp.exp(lg * (chunk - idx))
            kdec = jnp.exp(lg * idx)
        else:
            qdec = jnp.exp(lg * (idx + 1.0))
            kdec = jnp.exp(lg * (chunk - 1.0 - idx))
        cdec = jnp.exp(jnp.full((1, 1), lg * chunk, F32))
        q = rope(q_ref[:, sl].astype(F32))
        k = rope(k_ref[:, sl].astype(F32)) * scale
        v = v_ref[:, sl]
        qb = q.astype(BF16)
        scores = _dot_nt(qb, k.astype(BF16)) * dmask
        inner = _dot(scores.astype(BF16), v)
        s_old = state_ref[h]
        cross = _dot(qb, s_old.astype(BF16)) * qdec
        state_ref[h] = s_old * cdec + _dot_tn((k * kdec).astype(BF16), v)
        o = inner + cross
        if reverse:
            o = o + of_ref[:, sl]
            mu = jnp.mean(o, axis=-1, keepdims=True)
            oc = o - mu
            var = jnp.mean(oc * oc, axis=-1, keepdims=True)
            on = oc * lax.rsqrt(var + EPS) * nw_ref[:, sl]
            o_ref[:, sl] = (_silu(g_ref[:, sl].astype(F32)) * on).astype(BF16)
        else:
            o_ref[:, sl] = o


def _retention(proj3, log_gamma, cos2, sin2, norm_w, ctx_len):
    b, s, _ = proj3.shape
    c = RET_CHUNK
    nch = s // c
    nctx = ctx_len // c

    def fwd_chunk(t):
        return t

    def bwd_chunk(t):
        return jnp.where(t < nctx, nctx - 1 - t, nch - 1 + nctx - t)

    def specs(chunk_of):
        col = lambda j: pl.BlockSpec((None, c, RET_W), lambda bi, t, j=j: (bi, chunk_of(t), j))
        tab = pl.BlockSpec((c, RET_DIM), lambda bi, t: (chunk_of(t), 0))
        return col, tab

    smem = pl.BlockSpec(memory_space=pltpu.SMEM)
    scratch = [pltpu.VMEM((RET_HEADS, RET_DIM, RET_DIM), F32)]

    col, tab = specs(fwd_chunk)
    o_f = pl.pallas_call(
        functools.partial(_ret_kernel, reverse=False, chunk=c),
        out_shape=jax.ShapeDtypeStruct((b, s, RET_W), F32),
        grid=(b, nch),
        in_specs=[smem, col(0), col(1), col(2), tab, tab],
        out_specs=col(0),
        scratch_shapes=scratch,
        compiler_params=_cparams(2),
        name="retention_fwd",
    )(log_gamma, proj3, proj3, proj3, cos2, sin2)

    col, tab = specs(bwd_chunk)
    return pl.pallas_call(
        functools.partial(_ret_kernel, reverse=True, chunk=c),
        out_shape=jax.ShapeDtypeStruct((b, s, RET_W), BF16),
        grid=(b, nch),
        in_specs=[smem, col(0), col(1), col(2), tab, tab, col(0), col(3),
                  pl.BlockSpec((1, RET_W), lambda bi, t: (0, 0))],
        out_specs=col(0),
        scratch_shapes=scratch,
        compiler_params=_cparams(2),
        name="retention_bwd",
    )(log_gamma, proj3, proj3, proj3, cos2, sin2, o_f, proj3, norm_w)


def _dprep_kernel(q_ref, k_ref, v_ref, qw_ref, kw_ref, c_ref, s1_ref, s2_ref, bd_ref,
                  q0t_ref, q1t_ref, kk_ref, vt_ref):
    cc = c_ref[...]
    s1 = s1_ref[...]
    s2 = s2_ref[...]
    bd = bd_ref[...]
    rows = q_ref.shape[0]
    first = lax.broadcasted_iota(I32, (LANES, rows), 0) < DIFF_QK
    quarter = DIFF_QK // 2

    def prep(x, w):
        x = x.astype(F32)
        hi, lo = _split(x * x)
        ss = _dot(hi, bd) + _dot(lo, bd)
        xn = x * lax.rsqrt(ss * (1.0 / DIFF_QK) + EPS) * w
        return xn * cc + pltpu.roll(xn, LANES - quarter, 1) * s1 + pltpu.roll(xn, quarter, 1) * s2

    for h in range(DIFF_HEADS):
        sl = slice(h * LANES, (h + 1) * LANES)
        qt = (prep(q_ref[:, sl], qw_ref[...]) * (DIFF_QK ** -0.5 * LOG2E)).T
        q0t_ref[sl, :] = jnp.where(first, qt, 0.0).astype(BF16)
        q1t_ref[sl, :] = jnp.where(first, 0.0, qt).astype(BF16)
        kk_ref[:, sl] = prep(k_ref[:, sl], kw_ref[...]).astype(BF16)
        vt_ref[sl, :] = v_ref[:, sl].astype(F32).T.astype(BF16)


def _diff_prep(proj, qnorm_w, knorm_w, dc, ds1, ds2, batch, seq):
    t, _ = proj.shape
    tiles_per_seq = seq // TOK_TILE
    qcol = (4 * RET_W) // DIFF_QK_W
    blockdiag = jnp.kron(jnp.eye(2, dtype=F32), jnp.ones((DIFF_QK, DIFF_QK), F32)).astype(BF16)
    tab = pl.BlockSpec((TOK_TILE, LANES), lambda i: (i % tiles_per_seq, 0))
    row = pl.BlockSpec((1, LANES), lambda i: (0, 0))
    col = lambda j: pl.BlockSpec((TOK_TILE, DIFF_QK_W), lambda i, j=j: (i, j))
    tr = pl.BlockSpec((None, DIFF_QK_W, TOK_TILE),
                      lambda i: (i // tiles_per_seq, 0, i % tiles_per_seq))
    shp_t = jax.ShapeDtypeStruct((batch, DIFF_QK_W, seq), BF16)
    return pl.pallas_call(
        _dprep_kernel,
        out_shape=(shp_t, shp_t, jax.ShapeDtypeStruct((t, DIFF_QK_W), BF16), shp_t),
        grid=(t // TOK_TILE,),
        in_specs=[col(qcol), col(qcol + 1), col(qcol + 2), row, row, tab, tab, tab,
                  pl.BlockSpec((LANES, LANES), lambda i: (0, 0))],
        out_specs=(tr, tr, col(0), tr),
        compiler_params=_cparams(1),
        name="diff_prep",
    )(proj, proj, proj, jnp.tile(qnorm_w, 2)[None], jnp.tile(knorm_w, 2)[None], dc, ds1, ds2,
      blockdiag)


def _flash_kernel(lam_ref, q0t_ref, q1t_ref, k_ref, vt_ref, sw_ref, o_ref, m_s, acc_s, s_s, *,
                  heads, nctx_tiles, ctx_kv, all_kv, lambda_init):
    i = pl.program_id(2)
    nkv = jnp.where(i < nctx_tiles, ctx_kv, all_kv)
    m_s[...] = jnp.full(m_s.shape, -jnp.inf, F32)
    acc_s[...] = jnp.zeros(acc_s.shape, F32)
    ones = jnp.ones((BF16_SUBLANES, ATT_TK), BF16)

    def scores(j, buf):
        off = pl.multiple_of(j * ATT_TK, ATT_TK)
        for h in range(heads):
            sl = slice(h * DIFF_V, (h + 1) * DIFF_V)
            kt = k_ref[pl.ds(off, ATT_TK), sl]
            for comp, qt_ref in enumerate((q0t_ref, q1t_ref)):
                s_s[buf, 2 * h + comp] = _dot(kt, qt_ref[sl, :])

    def softmax_pv(j, buf):
        off = pl.multiple_of(j * ATT_TK, ATT_TK)
        for h in range(heads):
            sl = slice(h * DIFF_V, (h + 1) * DIFF_V)
            vt = jnp.concatenate([vt_ref[sl, pl.ds(off, ATT_TK)], ones], axis=0)
            for comp in range(2):
                c = 2 * h + comp
                st = s_s[buf, c]
                m_old = m_s[c]
                m_new = jnp.maximum(m_old, jnp.max(st, axis=0, keepdims=True))
                alpha = jnp.exp2(m_old - m_new)
                pt = jnp.exp2(st - m_new)
                acc_s[c] = alpha * acc_s[c] + _dot(vt, pt.astype(BF16))
                m_s[c] = m_new

    scores(0, 0)

    def pairs(first, n_pairs):
        for p in range(n_pairs):
            scores(first + 2 * p + 1, 1)
            softmax_pv(first + 2 * p, 0)
            scores(first + 2 * p + 2, 0)
            softmax_pv(first + 2 * p + 1, 1)

    group = 2 * ATT_UNROLL_PAIRS
    n_groups = (nkv - 1) // group

    def group_body(g, carry):
        pairs(group * g, ATT_UNROLL_PAIRS)
        return carry

    def pair_body(p, carry):
        pairs(group * n_groups + 2 * p, 1)
        return carry

    lax.fori_loop(0, n_groups, group_body, 0)
    lax.fori_loop(0, ((nkv - 1) % group) // 2, pair_body, 0)
    softmax_pv(nkv - 1, 0)
    lv = lam_ref[...]
    lam = (jnp.exp(jnp.sum(lv[0:1] * lv[1:2], axis=-1, keepdims=True))
           - jnp.exp(jnp.sum(lv[2:3] * lv[3:4], axis=-1, keepdims=True)) + lambda_init)
    for h in range(heads):
        sl = slice(h * DIFF_V, (h + 1) * DIFF_V)
        o0 = acc_s[2 * h, 0:DIFF_V] / acc_s[2 * h, DIFF_V:DIFF_V + 1]
        o1 = acc_s[2 * h + 1, 0:DIFF_V] / acc_s[2 * h + 1, DIFF_V:DIFF_V + 1]
        d = (o0 - lam * o1).T
        ms = jnp.mean(d * d, axis=-1, keepdims=True)
        o_ref[:, sl] = (d * lax.rsqrt(ms + EPS) * sw_ref[...] * (1.0 - lambda_init)).astype(BF16)


def _diff_attention(q0t, q1t, kk, vt, lam_vecs, subln_w, ctx_len, lambda_init):
    b, _, s = q0t.shape
    kk = kk.reshape(b, s, DIFF_QK_W)
    hb = ATT_HEADS
    w = hb * DIFF_V
    assert (s // ATT_TK) % 2 == 1 and (ctx_len // ATT_TK) % 2 == 1
    qspec = pl.BlockSpec((None, w, ATT_TQ), lambda bi, h, i: (bi, h, i))
    return pl.pallas_call(
        functools.partial(_flash_kernel, heads=hb, nctx_tiles=ctx_len // ATT_TQ,
                          ctx_kv=ctx_len // ATT_TK, all_kv=s // ATT_TK, lambda_init=lambda_init),
        out_shape=jax.ShapeDtypeStruct((b, s, DIFF_W), BF16),
        grid=(b, DIFF_HEADS // hb, s // ATT_TQ),
        in_specs=[
            pl.BlockSpec((4, DIFF_QK), lambda bi, h, i: (0, 0)),
            qspec, qspec,
            pl.BlockSpec((None, s, w), lambda bi, h, i: (bi, 0, h)),
            pl.BlockSpec((None, w, s), lambda bi, h, i: (bi, h, 0)),
            pl.BlockSpec((1, DIFF_V), lambda bi, h, i: (0, 0)),
        ],
        out_specs=pl.BlockSpec((None, ATT_TQ, w), lambda bi, h, i: (bi, i, h)),
        scratch_shapes=[pltpu.VMEM((2 * hb, 1, ATT_TQ), F32),
                        pltpu.VMEM((2 * hb, DIFF_V + BF16_SUBLANES, ATT_TQ), F32),
                        pltpu.VMEM((2, 2 * hb, ATT_TK, ATT_TQ), F32)],
        compiler_params=_cparams(3),
        name="diff_flash",
    )(lam_vecs, q0t, q1t, kk, vt, subln_w[None])


def _s5_kernel(u_ref, br_ref, bi_ref, cr_ref, ci_ref, tab_ref, *rest, reverse, ntiles):
    if reverse:
        yf_ref, d_ref, wg_ref, o_ref, xr_s, xi_s, car_s, cai_s = rest
    else:
        o_ref, xr_s, xi_s, car_s, cai_s = rest
    t = pl.program_id(1)

    @pl.when(t == 0)
    def _():
        car_s[...] = jnp.zeros_like(car_s)
        cai_s[...] = jnp.zeros_like(cai_s)

    u = u_ref[...]
    nq = S5_CH // LANES
    sw = S5_W // nq
    for q in range(nq):
        uq = u[:, q * LANES:(q + 1) * LANES]
        xr_s[:, q * sw:(q + 1) * sw] = _dot(uq, br_ref[q])
        xi_s[:, q * sw:(q + 1) * sw] = _dot(uq, bi_ref[q])

    def body(n, carry):
        cr, ci = carry
        tile = (ntiles - 1 - n) if reverse else n
        off = pl.multiple_of(tile * SUBLANES, SUBLANES)
        xr = xr_s[pl.ds(off, SUBLANES), :]
        xi = xi_s[pl.ds(off, SUBLANES), :]
        for lvl, k in enumerate((1, 2, 4)):
            pr = tab_ref[2 * lvl]
            pi = tab_ref[2 * lvl + 1]
            shift = (SUBLANES - k) if reverse else k
            sr = pltpu.roll(xr, shift, 0)
            si = pltpu.roll(xi, shift, 0)
            xr, xi = xr + pr * sr - pi * si, xi + pr * si + pi * sr
        ar = tab_ref[6]
        ai = tab_ref[7]
        xr, xi = xr + ar * cr - ai * ci, xi + ar * ci + ai * cr
        xr_s[pl.ds(off, SUBLANES), :] = xr
        xi_s[pl.ds(off, SUBLANES), :] = xi
        last = 0 if reverse else SUBLANES - 1
        return xr[last:last + 1], xi[last:last + 1]

    cr, ci = lax.fori_loop(0, ntiles, body, (car_s[...], cai_s[...]))
    car_s[...] = cr
    cai_s[...] = ci
    y = jnp.concatenate(
        [_dot(xr_s[:, q * sw:(q + 1) * sw].astype(BF16), cr_ref[q])
         + _dot(xi_s[:, q * sw:(q + 1) * sw].astype(BF16), ci_ref[q]) for q in range(nq)], axis=1)
    if reverse:
        y = y + yf_ref[...] + d_ref[...] * u.astype(F32)
        y = jax.nn.gelu(y, approximate=True)
        o_ref[...] = (y * jax.nn.sigmoid(_dot(y.astype(BF16), wg_ref[...]))).astype(BF16)
    else:
        o_ref[...] = y


def _s5_tables(a_re_log, a_im_ang, reverse):
    w = a_re_log.reshape(1, S5_W)
    th = a_im_ang.reshape(1, S5_W)
    row = jnp.arange(SUBLANES, dtype=F32)[:, None]
    tabs = []
    for k in (1, 2, 4):
        keep = (row <= SUBLANES - 1 - k) if reverse else (row >= k)
        mag = jnp.exp(w * k)
        tabs.append(jnp.where(keep, mag * jnp.cos(th * k), 0.0))
        tabs.append(jnp.where(keep, mag * jnp.sin(th * k), 0.0))
    e = (SUBLANES - row) if reverse else (row + 1.0)
    mag = jnp.exp(w * e)
    tabs.append(mag * jnp.cos(th * e))
    tabs.append(mag * jnp.sin(th * e))
    return jnp.stack(tabs)


def _s5_params(lam_re, lam_im, log_dt, b_re, b_im):
    dt = jnp.exp(log_dt)[:, None]
    wlog = lam_re * dt
    ang = lam_im * dt
    mag = jnp.exp(wlog)
    a_re = mag * jnp.cos(ang)
    a_im = mag * jnp.sin(ang)
    den = lam_re * lam_re + lam_im * lam_im
    nr = a_re - 1.0
    f_re = (nr * lam_re + a_im * lam_im) / den
    f_im = (a_im * lam_re - nr * lam_im) / den
    bb_re = f_re[..., None] * b_re - f_im[..., None] * b_im
    bb_im = f_re[..., None] * b_im + f_im[..., None] * b_re
    eye = jnp.eye(S5_GROUPS, dtype=F32)
    bmat_re = jnp.einsum('gph,gk->ghkp', bb_re, eye).reshape(S5_CH, S5_W)
    bmat_im = jnp.einsum('gph,gk->ghkp', bb_im, eye).reshape(S5_CH, S5_W)
    return wlog, ang, _diag_blocks(bmat_re).astype(BF16), _diag_blocks(bmat_im).astype(BF16)


def _diag_blocks(m):
    nq = S5_CH // LANES
    r, c = m.shape[0] // nq, m.shape[1] // nq
    return jnp.stack([m[q * r:(q + 1) * r, q * c:(q + 1) * c] for q in range(nq)])


def _s5_mixer(proj3, lam_re, lam_im, log_dt, b_re, b_im, c_re, c_im, d, w_glu, ctx_len):
    b, s, _ = proj3.shape
    tb = S5_BLOCK
    nb = s // tb
    nctx = ctx_len // tb
    ucol = (4 * RET_W + 2 * DIFF_QK_W + DIFF_W) // S5_CH
    eye = jnp.eye(S5_GROUPS, dtype=F32)
    cmat_re = _diag_blocks(jnp.einsum('ghp,gk->gpkh', c_re, eye).reshape(S5_W, S5_CH)).astype(BF16)
    cmat_im = _diag_blocks(-jnp.einsum('ghp,gk->gpkh', c_im, eye).reshape(S5_W, S5_CH)).astype(BF16)
    full = lambda shape: pl.BlockSpec(shape, lambda bi, t: tuple(0 for _ in shape))
    scratch = [pltpu.VMEM((tb, S5_W), F32), pltpu.VMEM((tb, S5_W), F32),
               pltpu.VMEM((1, S5_W), F32), pltpu.VMEM((1, S5_W), F32)]
    nq = S5_CH // LANES
    weights = [full((nq, LANES, S5_W // nq)), full((nq, LANES, S5_W // nq)),
               full((nq, S5_W // nq, LANES)), full((nq, S5_W // nq, LANES)),
               full((8, SUBLANES, S5_W))]

    def blk_f(t):
        return t

    def blk_b(t):
        return jnp.where(t < nctx, nctx - 1 - t, nb - 1 + nctx - t)

    wl, ang, bre, bim = _s5_params(lam_re[0], lam_im[0], log_dt[0], b_re, b_im)
    y_f = pl.pallas_call(
        functools.partial(_s5_kernel, reverse=False, ntiles=tb // SUBLANES),
        out_shape=jax.ShapeDtypeStruct((b, s, S5_CH), F32),
        grid=(b, nb),
        in_specs=[pl.BlockSpec((None, tb, S5_CH), lambda bi, t: (bi, blk_f(t), ucol))] + weights,
        out_specs=pl.BlockSpec((None, tb, S5_CH), lambda bi, t: (bi, blk_f(t), 0)),
        scratch_shapes=scratch,
        compiler_params=_cparams(2),
        name="s5_fwd",
    )(proj3, bre, bim, cmat_re, cmat_im, _s5_tables(wl, ang, False))

    wl, ang, bre, bim = _s5_params(lam_re[1], lam_im[1], log_dt[1], b_re, b_im)
    return pl.pallas_call(
        functools.partial(_s5_kernel, reverse=True, ntiles=tb // SUBLANES),
        out_shape=jax.ShapeDtypeStruct((b, s, S5_CH), BF16),
        grid=(b, nb),
        in_specs=[pl.BlockSpec((None, tb, S5_CH), lambda bi, t: (bi, blk_b(t), ucol))] + weights + [
            pl.BlockSpec((None, tb, S5_CH), lambda bi, t: (bi, blk_b(t), 0)),
            full((1, S5_CH)), full((S5_CH, S5_CH))],
        out_specs=pl.BlockSpec((None, tb, S5_CH), lambda bi, t: (bi, blk_b(t), 0)),
        scratch_shapes=scratch,
        compiler_params=_cparams(2),
        name="s5_bwd",
    )(proj3, bre, bim, cmat_re, cmat_im, _s5_tables(wl, ang, True), y_f, d[None],
      w_glu.astype(BF16))


def _neg_inf_like(x):
    return jnp.full(x.shape, -jnp.inf, x.dtype)


def _out_kernel(x_ref, ret_ref, dif_ref, s5_ref, wo_ref, g1_ref, nw_ref, sc_ref, sh_ref,
                rw_ref, rb_ref, tri_ref,
                x1_ref, h2_ref, h2p_ref, te_ref, tw_ref, rk_ref, cnt_ref, cnt_s):
    i = pl.program_id(0)

    @pl.when(i == 0)
    def _():
        cnt_s[...] = jnp.zeros_like(cnt_s)

    d = x_ref.shape[1]
    tm = x_ref.shape[0]
    mix = (_dot(ret_ref[...], wo_ref[0:RET_W, :])
           + _dot(dif_ref[...], wo_ref[RET_W:RET_W + DIFF_W, :])
           + _dot(s5_ref[...], wo_ref[RET_W + DIFF_W:, :]))
    x1 = x_ref[...] + g1_ref[...] * mix
    x1_ref[...] = x1
    ms = jnp.mean(x1 * x1, axis=-1, keepdims=True)
    h2 = x1 * lax.rsqrt(ms + EPS) * nw_ref[...]
    h2 = h2 * (1.0 + sc_ref[...]) + sh_ref[...]
    h2_ref[...] = h2.astype(BF16)
    h2p_ref[...] = _pack_halves(h2[:, :d // 2], h2[:, d // 2:])

    ng = N_EXPERT_GROUPS
    pg = N_EXPERTS // N_EXPERT_GROUPS
    logits = _dot3_nt(rw_ref[...], h2)
    scores = jax.nn.sigmoid(logits)
    sel3 = (scores + rb_ref[...]).reshape(ng, pg, tm)
    scores3 = scores.reshape(ng, pg, tm)
    in_grp = lax.broadcasted_iota(I32, (ng, pg, tm), 1).astype(F32)
    grp = lax.broadcasted_iota(I32, (ng, pg, tm), 0).astype(F32)
    eidx = grp * pg + in_grp
    gidx = lax.broadcasted_iota(I32, (ng, 1, tm), 0).astype(F32)

    m1 = jnp.max(sel3, axis=1, keepdims=True)
    first = jnp.min(jnp.where(sel3 == m1, in_grp, float(pg)), axis=1, keepdims=True)
    m2 = jnp.max(jnp.where(in_grp == first, -jnp.inf, sel3), axis=1, keepdims=True)
    rem = m1 + m2
    gsel = jnp.zeros((ng, 1, tm), F32)
    for _ in range(TOPK_GROUPS):
        mx = jnp.max(rem, axis=0, keepdims=True)
        fi = jnp.min(jnp.where(rem == mx, gidx, float(ng)), axis=0, keepdims=True)
        pick = gidx == fi
        gsel = jnp.where(pick, 1.0, gsel)
        rem = jnp.where(pick, -jnp.inf, rem)
    masked = jnp.where(gsel > 0.0, sel3, -jnp.inf)

    def red2(fn, x):
        return fn(fn(x, axis=1, keepdims=True), axis=0, keepdims=True)

    chosen = jnp.zeros((ng, pg, tm), F32)
    picks = []
    weights = []
    for _ in range(TOP_K):
        mx = red2(jnp.max, masked)
        fi = red2(jnp.min, jnp.where(masked == mx, eidx, float(N_EXPERTS)))
        pick = eidx == fi
        picks.append(fi)
        weights.append(red2(jnp.sum, jnp.where(pick, scores3, 0.0)))
        chosen = jnp.where(pick, 1.0, chosen)
        masked = jnp.where(pick, -jnp.inf, masked)
    wsum = weights[0]
    for w in weights[1:]:
        wsum = wsum + w
    inv = ROUTE_SCALE / wsum

    chosen2 = chosen.reshape(N_EXPERTS, tm)
    cum = _dot(chosen2.astype(BF16), tri_ref[...]) + cnt_s[...]
    cum3 = cum.reshape(ng, pg, tm)
    for k in range(TOP_K):
        te_ref[k:k + 1, :] = picks[k].reshape(1, tm).astype(I32)
        tw_ref[k:k + 1, :] = (weights[k] * inv).reshape(1, tm)
        rk = red2(jnp.sum, jnp.where(eidx == picks[k], cum3, 0.0))
        rk_ref[k:k + 1, :] = rk.reshape(1, tm).astype(I32)
    cnt_new = cnt_s[...] + jnp.sum(chosen2, axis=-1, keepdims=True)
    cnt_s[...] = cnt_new
    cnt_ref[...] = jnp.broadcast_to(cnt_new, cnt_ref.shape)


def _out_proj_route(z, ret, dif, s5, w_out, g1, norm_w, sc, sh, router_w, router_bias, mod_row,
                    src_tile, n_out_tiles):
    d = z.shape[1]
    tm = TOK_TILE
    t = n_out_tiles * tm
    tri = (jnp.arange(tm)[:, None] < jnp.arange(tm)[None, :]).astype(BF16)
    mod_spec = pl.BlockSpec((None, 1, d), lambda i: (mod_row(src_tile(i)), 0, 0))
    src = lambda w: pl.BlockSpec((tm, w), lambda i: (src_tile(i), 0))
    tok = lambda w: pl.BlockSpec((tm, w), lambda i: (i, 0))
    const = lambda shape: pl.BlockSpec(shape, lambda i: tuple(0 for _ in shape))
    lane_out = pl.BlockSpec((TOP_K, tm), lambda i: (0, i))
    return pl.pallas_call(
        _out_kernel,
        out_shape=(
            jax.ShapeDtypeStruct((t, d), F32),
            jax.ShapeDtypeStruct((t, d), BF16),
            jax.ShapeDtypeStruct((t, d // 2), U32),
            jax.ShapeDtypeStruct((TOP_K, t), I32),
            jax.ShapeDtypeStruct((TOP_K, t), F32),
            jax.ShapeDtypeStruct((TOP_K, t), I32),
            jax.ShapeDtypeStruct((N_EXPERTS, LANES), F32),
        ),
        grid=(t // tm,),
        in_specs=[src(d), src(RET_W), src(DIFF_W), src(S5_CH), const((d, d)),
                  mod_spec, const((1, d)), mod_spec, mod_spec,
                  const((N_EXPERTS, d)), const((N_EXPERTS, 1)), const((tm, tm))],
        out_specs=(tok(d), tok(d), tok(d // 2), lane_out, lane_out, lane_out,
                   const((N_EXPERTS, LANES))),
        scratch_shapes=[pltpu.VMEM((N_EXPERTS, 1), F32)],
        compiler_params=_cparams(1),
        name="out_proj_route",
    )(z, ret, dif, s5, w_out, g1, norm_w, sc, sh, router_w.T, router_bias[:, None], tri)


def _expert_kernel(te_ref, s0_ref, nv_ref, nt_ref, h2p_hbm, tok_hbm, dst_hbm, wg_ref, wu_ref, wd_ref,
                   ybuf_hbm, wg_s, wu_s, wd_s, xs_s, y_s, tokw, dstw, sem_g, sem_s, sem_w):
    i = pl.program_id(0)
    tm = xs_s.shape[1]
    half = xs_s.shape[2]
    n_valid = nt_ref[0]
    trash = ybuf_hbm.shape[0] - EXP_BUFS * tm

    def win_base(step):
        return (s0_ref[step + 1] // WIN_ALIGN) * WIN_ALIGN

    def window_copies(step, slot):
        base = pl.multiple_of(win_base(step), WIN_ALIGN)
        dst = pl.ds(pl.multiple_of(slot * WIN, WIN), WIN)
        return (pltpu.make_async_copy(tok_hbm.at[pl.ds(base, WIN)], tokw.at[dst], sem_w.at[slot]),
                pltpu.make_async_copy(dst_hbm.at[pl.ds(base, WIN)], dstw.at[dst], sem_w.at[slot]))

    def issue_gathers(tile, step, slot):
        buf = tile % EXP_BUFS
        first = slot * WIN + s0_ref[tile + 2] - win_base(step)
        for r in range(tm):
            pltpu.make_async_copy(h2p_hbm.at[pl.ds(tokw[first + r], 1), :],
                                  xs_s.at[buf, pl.ds(r, 1), :], sem_g.at[buf]).start(priority=r % 2)

    def issue_scatters(tile, step, slot):
        buf = (tile + EXP_BUFS) % EXP_BUFS
        first = slot * WIN + s0_ref[tile + 2] - win_base(step)
        spare = trash + buf * tm
        nv = nv_ref[tile + 2]
        for r in range(tm):
            dest = jnp.where(r < nv, dstw[first + r], spare + r)
            pltpu.make_async_copy(y_s.at[buf, pl.ds(r, 1), :],
                                  ybuf_hbm.at[pl.ds(dest, 1), :], sem_s.at[buf]).start(priority=r % 2)

    def wait_gathers(buf):
        pltpu.make_async_copy(xs_s.at[buf], xs_s.at[buf], sem_g.at[buf]).wait()

    def wait_scatters(buf):
        pltpu.make_async_copy(y_s.at[buf], y_s.at[buf], sem_s.at[buf]).wait()

    @pl.when(i < n_valid)
    def _step():
        cur = i % EXP_BUFS
        slot = i % 2

        @pl.when(i == 0)
        def _prologue():
            y_s[...] = jnp.zeros_like(y_s)
            for b in range(EXP_BUFS):
                cp = pltpu.make_async_copy(y_s.at[b], ybuf_hbm.at[pl.ds(trash + b * tm, tm), :],
                                           sem_s.at[b])
                cp.start()
                cp.wait()
            for cp in window_copies(0, 0):
                cp.start()
            for cp in window_copies(0, 0):
                cp.wait()
            issue_gathers(0, 0, 0)
            issue_gathers(1, 0, 0)
            issue_scatters(-2, 0, 0)

        @pl.when(i > 0)
        def _():
            for cp in window_copies(i, slot):
                cp.wait()

        for cp in window_copies(i + 1, 1 - slot):
            cp.start()
        wait_gathers(cur)

        @pl.when(i > 0)
        def _():
            wait_scatters(cur)

        @pl.when((i == 0) | (te_ref[i] != te_ref[jnp.maximum(i - 1, 0)]))
        def _():
            wg_s[...] = wg_ref[...].astype(BF16)
            wu_s[...] = wu_ref[...].astype(BF16)
            wd_s[...] = wd_ref[...].astype(BF16)

        issue_gathers(i + 2, i, slot)
        issue_scatters(i - 1, i, slot)
        lo, hi = _unpack_halves(xs_s[cur])
        lo = lo.astype(BF16)
        hi = hi.astype(BF16)
        g = _dot(lo, wg_s[0:half, :]) + _dot(hi, wg_s[half:, :])
        u = _dot(lo, wu_s[0:half, :]) + _dot(hi, wu_s[half:, :])
        y = _dot((_silu(g) * u).astype(BF16), wd_s[...])
        y_s[cur] = _pack_halves(y[:, :half], y[:, half:])

        @pl.when(i == n_valid - 1)
        def _epilogue():
            issue_scatters(i, i, slot)
            for b in range(EXP_BUFS):
                wait_scatters(b)
            wait_gathers((i + 1) % EXP_BUFS)
            wait_gathers((i + 2) % EXP_BUFS)
            for cp in window_copies(i + 1, 1 - slot):
                cp.wait()


def _experts(h2p, tok_sorted, dst_sorted, tile_expert, tile_s0, tile_nv, n_valid, w_gate, w_up,
             w_down, layer, n_rows_out):
    half = h2p.shape[1]
    _, e, d, f = w_gate.shape
    tm = EXP_TILE
    n_tiles = tile_expert.shape[0]
    hbm = pl.BlockSpec(memory_space=pl.ANY)
    wspec = lambda a, b: pl.BlockSpec((None, None, a, b),
                                      lambda i, te, s0, nv, nt: (layer, te[i], 0, 0))
    grid_spec = pltpu.PrefetchScalarGridSpec(
        num_scalar_prefetch=4,
        grid=(n_tiles,),
        in_specs=[hbm, hbm, hbm, wspec(d, f), wspec(d, f), wspec(f, d)],
        out_specs=hbm,
        scratch_shapes=[pltpu.VMEM((d, f), BF16), pltpu.VMEM((d, f), BF16), pltpu.VMEM((f, d), BF16),
                        pltpu.VMEM((EXP_BUFS, tm, half), U32), pltpu.VMEM((EXP_BUFS, tm, half), U32),
                        pltpu.SMEM((2 * WIN,), I32), pltpu.SMEM((2 * WIN,), I32),
                        pltpu.SemaphoreType.DMA((EXP_BUFS,)), pltpu.SemaphoreType.DMA((EXP_BUFS,)),
                        pltpu.SemaphoreType.DMA((2,))],
    )
    return pl.pallas_call(
        _expert_kernel,
        out_shape=jax.ShapeDtypeStruct((n_rows_out + EXP_BUFS * tm, half), U32),
        grid_spec=grid_spec,
        compiler_params=_cparams(1, 56),
        name="moe_experts",
    )(tile_expert, tile_s0, tile_nv, n_valid, h2p, tok_sorted, dst_sorted, w_gate, w_up, w_down)


def _combine_kernel(y_ref, x1_ref, h2_ref, w_ref, g2_ref, sg_ref, su_ref, sd_ref, o_ref):
    tm = x1_ref.shape[0]
    half = y_ref.shape[1]
    h2 = h2_ref[...]
    hid = _silu(_dot(h2, sg_ref[...])) * _dot(h2, su_ref[...])
    shared = _dot(hid.astype(BF16), sd_ref[...])

    w = w_ref[...]
    acc_lo = None
    acc_hi = None
    for k in range(TOP_K):
        lo, hi = _unpack_halves(y_ref[k * tm:(k + 1) * tm, :])
        wk = w[:, k:k + 1]
        acc_lo = wk * lo if acc_lo is None else acc_lo + wk * lo
        acc_hi = wk * hi if acc_hi is None else acc_hi + wk * hi
    g2 = g2_ref[...]
    o_ref[:, :half] = x1_ref[:, :half] + g2[:, :half] * (acc_lo + shared[:, :half])
    o_ref[:, half:] = x1_ref[:, half:] + g2[:, half:] * (acc_hi + shared[:, half:])


def _combine(ybuf, x1, h2, tw, g2, s_gate, s_up, s_down, mod_row):
    t, d = x1.shape
    tm = TOK_TILE
    f = s_gate.shape[1]
    tok = lambda w: pl.BlockSpec((tm, w), lambda i: (i, 0))
    const = lambda shape: pl.BlockSpec(shape, lambda i: tuple(0 for _ in shape))
    return pl.pallas_call(
        _combine_kernel,
        out_shape=jax.ShapeDtypeStruct((t, d), F32),
        grid=(t // tm,),
        in_specs=[pl.BlockSpec((TOP_K * tm, d // 2), lambda i: (i, 0)),
                  tok(d), tok(d), tok(TOP_K),
                  pl.BlockSpec((None, 1, d), lambda i: (mod_row(i), 0, 0)),
                  const((d, f)), const((d, f)), const((f, d))],
        out_specs=tok(d),
        compiler_params=_cparams(1),
        name="moe_combine",
    )(ybuf, x1, h2, tw, g2, s_gate, s_up, s_down)


def _moe(x1, h2, h2p, te, tw, rk, cnt, g2, w_gate, w_up, w_down, layer, s_gate, s_up, s_down,
         mod_row):
    t = x1.shape[0]
    n = t * TOP_K
    tm = EXP_TILE
    n_tiles = n // tm + N_EXPERTS
    counts = cnt[:, 0].astype(I32)
    padded = ((counts + tm - 1) // tm) * tm
    pad_end = jnp.cumsum(padded)
    pad_off = pad_end - padded
    off = jnp.cumsum(counts) - counts
    experts = jnp.arange(N_EXPERTS, dtype=I32)
    off_of = jnp.sum(jnp.where(te[:, :, None] == experts, off, 0), axis=-1)
    key = (off_of + rk).T.reshape(-1)
    order = jnp.argsort(key).astype(I32)
    tok_sorted = order // TOP_K
    k_sorted = order % TOP_K
    dst_sorted = ((tok_sorted // TOK_TILE) * (TOP_K * TOK_TILE) + k_sorted * TOK_TILE
                  + tok_sorted % TOK_TILE)
    tok_sorted = jnp.pad(tok_sorted, (0, WIN))
    dst_sorted = jnp.pad(dst_sorted, (0, WIN))

    tile_start = jnp.arange(n_tiles, dtype=I32) * tm
    tile_expert = jnp.minimum(jnp.sum((tile_start[:, None] >= pad_end[None, :]).astype(I32), axis=-1),
                              N_EXPERTS - 1)
    is_e = tile_expert[:, None] == experts
    row0 = tile_start - jnp.sum(jnp.where(is_e, pad_off, 0), axis=-1)
    valid = tile_start < pad_end[-1]
    s0 = jnp.where(valid, jnp.sum(jnp.where(is_e, off, 0), axis=-1) + row0, n)
    nv = jnp.where(valid, jnp.clip(jnp.sum(jnp.where(is_e, counts, 0), axis=-1) - row0, 0, tm), 0)
    tile_s0 = jnp.concatenate([jnp.zeros((2,), I32), s0, jnp.full((3,), n, I32)]).astype(I32)
    tile_nv = jnp.concatenate([jnp.zeros((2,), I32), nv, jnp.zeros((3,), I32)]).astype(I32)
    n_valid = (pad_end[-1:] // tm).astype(I32)
    ybuf = _experts(h2p, tok_sorted, dst_sorted, tile_expert, tile_s0, tile_nv, n_valid,
                    w_gate, w_up, w_down, layer, n)
    return _combine(ybuf, x1, h2, tw.T, g2, s_gate, s_up, s_down, mod_row)


def _rope_angles(rows, head_dim):
    axis_dim = head_dim // 2
    inv_freq = ROPE_BASE ** (-jnp.arange(0, axis_dim, 2, dtype=F32) / axis_dim)
    row = jnp.repeat(jnp.arange(rows, dtype=F32), GRID_W)
    col = jnp.tile(jnp.arange(GRID_W, dtype=F32), rows)
    return jnp.concatenate([row[:, None] * inv_freq, col[:, None] * inv_freq], axis=-1)


def _rope_tables(n, ctx_len):
    rows = n // GRID_W
    ang = _rope_angles(rows, RET_DIM)
    cos, sin = jnp.cos(ang), jnp.sin(ang)
    ret_cos = jnp.concatenate([cos, cos], axis=-1)
    ret_sin = jnp.concatenate([-sin, sin], axis=-1)
    ang = _rope_angles(rows, DIFF_QK)
    cos, sin = jnp.cos(ang), jnp.sin(ang)
    zero = jnp.zeros_like(sin)
    dc = jnp.tile(jnp.concatenate([cos, cos], axis=-1), (1, 2))
    ds1 = jnp.tile(jnp.concatenate([-sin, zero], axis=-1), (1, 2))
    ds2 = jnp.tile(jnp.concatenate([zero, sin], axis=-1), (1, 2))

    def with_ctx(tab, fill):
        return jnp.concatenate([jnp.full((ctx_len, tab.shape[1]), fill, F32), tab], axis=0)

    return (with_ctx(ret_cos, 1.0), with_ctx(ret_sin, 0.0),
            with_ctx(dc, 1.0), with_ctx(ds1, 0.0), with_ctx(ds2, 0.0))


def kernel(x, c, ctx, c_ctx, w_ada, b_ada, norm1_w, norm2_w, w_in, ret_decay_logit, ret_norm_w,
           diff_qnorm_w, diff_knorm_w, diff_lambda, diff_subln_w, s5_lambda_re, s5_lambda_im,
           s5_log_dt, s5_b_re, s5_b_im, s5_c_re, s5_c_im, s5_d, s5_w_glu, w_out, router_w,
           router_bias, exp_w_gate, exp_w_up, exp_w_down, shared_w_gate, shared_w_up, shared_w_down):
    out_dtype = x.dtype
    batch, n, d = x.shape
    ctx_len = ctx.shape[1]
    depth = w_ada.shape[0]
    s = ctx_len + n
    t = batch * s
    assert ctx_len % TOK_TILE == 0 and n % TOK_TILE == 0 and n % GRID_W == 0
    assert batch + 1 <= SUBLANES
    tiles_per_seq = s // TOK_TILE
    ctx_tiles = ctx_len // TOK_TILE

    def mod_row(i):
        return jnp.where(i % tiles_per_seq < ctx_tiles, batch, i // tiles_per_seq)

    z = jnp.concatenate([ctx.astype(F32), x.astype(F32)], axis=1).reshape(t, d)
    cond = jnp.zeros((SUBLANES, d), F32).at[:batch].set(c.astype(F32)).at[batch].set(c_ctx.astype(F32))
    mod = _adaln(cond, w_ada.astype(F32), b_ada.astype(F32))
    ret_cos, ret_sin, dc, ds1, ds2 = _rope_tables(n, ctx_len)

    for l in range(depth):
        lambda_init = 0.8 - 0.6 * math.exp(-0.3 * l)
        m = mod[l].reshape(SUBLANES, N_MOD, 1, d)
        sh1, sc1, g1, sh2, sc2, g2 = (m[:, j] for j in range(N_MOD))
        proj = _in_proj(z, norm1_w[l][None], sc1, sh1, w_in[l].astype(BF16), mod_row)
        proj3 = proj.reshape(batch, s, proj.shape[1])
        log_gamma = jax.nn.log_sigmoid(ret_decay_logit[l].astype(F32))
        ret = _retention(proj3, log_gamma, ret_cos, ret_sin, ret_norm_w[l][None], ctx_len)
        q0t, q1t, kk, vt = _diff_prep(proj, diff_qnorm_w[l], diff_knorm_w[l], dc, ds1, ds2,
                                      batch, s)
        dif = _diff_attention(q0t, q1t, kk, vt, diff_lambda[l], diff_subln_w[l], ctx_len,
                              lambda_init)
        s5 = _s5_mixer(proj3, s5_lambda_re[l], s5_lambda_im[l], s5_log_dt[l], s5_b_re[l],
                       s5_b_im[l], s5_c_re[l], s5_c_im[l], s5_d[l], s5_w_glu[l], ctx_len)
        if l < depth - 1:
            src_tile = lambda i: i
            n_out_tiles = t // TOK_TILE
        else:
            lat_tiles = tiles_per_seq - ctx_tiles
            src_tile = lambda i: i + (i // lat_tiles + 1) * ctx_tiles
            n_out_tiles = batch * lat_tiles
        x1, h2, h2p, te, tw, rk, cnt = _out_proj_route(
            z, ret.reshape(t, RET_W), dif.reshape(t, DIFF_W), s5.reshape(t, S5_CH),
            w_out[l].astype(BF16), g1, norm2_w[l][None], sc2, sh2, router_w[l], router_bias[l],
            mod_row, src_tile, n_out_tiles)
        z = _moe(x1, h2, h2p, te, tw, rk, cnt, g2, exp_w_gate.astype(F32),
                 exp_w_up.astype(F32), exp_w_down.astype(F32), l,
                 shared_w_gate[l].astype(BF16), shared_w_up[l].astype(BF16),
                 shared_w_down[l].astype(BF16), lambda i, f=src_tile: mod_row(f(i)))
    return z.reshape(batch, n, d).astype(out_dtype)
```

```python
import functools
import math

import jax
import jax.numpy as jnp
from jax import lax
from jax.experimental import pallas as pl
from jax.experimental.pallas import tpu as pltpu

F32 = jnp.float32
BF16 = jnp.bfloat16
U32 = jnp.uint32
I32 = jnp.int32

GRID_W = 64
N_MOD = 6
EPS = 1e-6
ROPE_BASE = 10000.0
RET_HEADS = 4
RET_DIM = 128
RET_W = RET_HEADS * RET_DIM
DIFF_HEADS = 8
DIFF_QK = 64
DIFF_V = 2 * DIFF_QK
DIFF_QK_W = DIFF_HEADS * 2 * DIFF_QK
DIFF_W = DIFF_HEADS * DIFF_V
S5_GROUP = 16
S5_CH = 512
S5_GROUPS = S5_CH // S5_GROUP
S5_STATE = 64
S5_W = S5_GROUPS * S5_STATE
N_EXPERTS = 64
EXPERT_FF = 512
TOP_K = 8
N_EXPERT_GROUPS = 8
TOPK_GROUPS = 4
ROUTE_SCALE = 2.5

V7X_VMEM_BYTES = 64 * 1024 * 1024
LANES = 128
SUBLANES = 8
BF16_SUBLANES = 16
TOK_TILE = 256
RET_CHUNK = 256
ATT_TQ = 256
ATT_TK = 256
ATT_HEADS = 4
ATT_UNROLL_PAIRS = 4
LOG2E = math.log2(math.e)
S5_BLOCK = 256
EXP_TILE = 256
ADA_TN = 1024
EXP_BUFS = 3
WIN = 2048
WIN_ALIGN = 1024


def _cparams(n_axes, vmem_mb=48):
    return pltpu.CompilerParams(
        dimension_semantics=("arbitrary",) * n_axes,
        vmem_limit_bytes=vmem_mb * 1024 * 1024,
    )


def _dot(a, b):
    return jnp.dot(a, b, preferred_element_type=F32)


def _dot_nt(a, b):
    return lax.dot_general(a, b, (((1,), (1,)), ((), ())), preferred_element_type=F32)


def _dot_tn(a, b):
    return lax.dot_general(a, b, (((0,), (0,)), ((), ())), preferred_element_type=F32)


def _split(x):
    hi = x.astype(BF16)
    lo = (x - hi.astype(F32)).astype(BF16)
    return hi, lo


def _dot3(a, b):
    ah, al = _split(a)
    bh, bl = _split(b)
    return _dot(ah, bh) + _dot(al, bh) + _dot(ah, bl)


def _dot3_nt(a, b):
    ah, al = _split(a)
    bh, bl = _split(b)
    return _dot_nt(ah, bh) + _dot_nt(al, bh) + _dot_nt(ah, bl)


def _silu(x):
    return x * jax.nn.sigmoid(x)


def _pack_halves(lo, hi):
    lo_b = pltpu.bitcast(lo.astype(BF16).astype(F32), U32) >> 16
    hi_b = pltpu.bitcast(hi.astype(BF16).astype(F32), U32) & jnp.uint32(0xFFFF0000)
    return hi_b | lo_b


def _unpack_halves(p):
    lo = pltpu.bitcast(p << 16, F32)
    hi = pltpu.bitcast(p & jnp.uint32(0xFFFF0000), F32)
    return lo, hi


def _ada_kernel(cond_ref, w_ref, b_ref, o_ref):
    o_ref[...] = _dot3(_silu(cond_ref[...]), w_ref[...]) + b_ref[...]


def _adaln(cond, w_ada, b_ada):
    n_layers, d, n = w_ada.shape
    rows = cond.shape[0]
    return pl.pallas_call(
        _ada_kernel,
        out_shape=jax.ShapeDtypeStruct((n_layers, rows, n), F32),
        grid=(n_layers, n // ADA_TN),
        in_specs=[
            pl.BlockSpec((rows, d), lambda l, j: (0, 0)),
            pl.BlockSpec((None, d, ADA_TN), lambda l, j: (l, 0, j)),
            pl.BlockSpec((None, 1, ADA_TN), lambda l, j: (l, 0, j)),
        ],
        out_specs=pl.BlockSpec((None, rows, ADA_TN), lambda l, j: (l, 0, j)),
        compiler_params=_cparams(2),
        name="adaln",
    )(cond, w_ada, b_ada.reshape(n_layers, 1, n))


def _proj_kernel(x_ref, nw_ref, sc_ref, sh_ref, w_ref, o_ref):
    x = x_ref[...]
    ms = jnp.mean(x * x, axis=-1, keepdims=True)
    h = x * lax.rsqrt(ms + EPS) * nw_ref[...]
    h = h * (1.0 + sc_ref[...]) + sh_ref[...]
    o_ref[...] = _dot(h.astype(BF16), w_ref[...]).astype(BF16)


def _in_proj(z, norm_w, sc, sh, w_in, mod_row):
    t, d = z.shape
    n = w_in.shape[1]
    mod_spec = pl.BlockSpec((None, 1, d), lambda i: (mod_row(i), 0, 0))
    return pl.pallas_call(
        _proj_kernel,
        out_shape=jax.ShapeDtypeStruct((t, n), BF16),
        grid=(t // TOK_TILE,),
        in_specs=[
            pl.BlockSpec((TOK_TILE, d), lambda i: (i, 0)),
            pl.BlockSpec((1, d), lambda i: (0, 0)),
            mod_spec,
            mod_spec,
            pl.BlockSpec((d, n), lambda i: (0, 0), pipeline_mode=pl.Buffered(1)),
        ],
        out_specs=pl.BlockSpec((TOK_TILE, n), lambda i: (i, 0)),
        compiler_params=_cparams(1, 56),
        name="in_proj",
    )(z, norm_w, sc, sh, w_in)


def _ret_kernel(lg_ref, q_ref, k_ref, v_ref, cos_ref, sin_ref, *rest, reverse, chunk):
    if reverse:
        of_ref, g_ref, nw_ref, o_ref, state_ref = rest
    else:
        o_ref, state_ref = rest
    t = pl.program_id(1)

    @pl.when(t == 0)
    def _():
        state_ref[...] = jnp.zeros_like(state_ref)

    ii = lax.broadcasted_iota(I32, (chunk, chunk), 0)
    jj = lax.broadcasted_iota(I32, (chunk, chunk), 1)
    rel = ((jj - ii) if reverse else (ii - jj)).astype(F32)
    idx = lax.broadcasted_iota(I32, (chunk, 1), 0).astype(F32)
    cos2 = cos_ref[...]
    sin2 = sin_ref[...]
    scale = RET_DIM ** -0.5
    direction = 1 if reverse else 0

    def rope(x):
        return x * cos2 + pltpu.roll(x, RET_DIM // 2, 1) * sin2

    for h in range(RET_HEADS):
        sl = slice(h * RET_DIM, (h + 1) * RET_DIM)
        lg = lg_ref[direction, h]
        dmask = jnp.where(rel >= 0.0, jnp.exp(lg * jnp.maximum(rel, 0.0)), 0.0)
        if reverse:
            qdec = jnp.exp(lg * (chunk - idx))
            kdec = jnp.exp(lg * idx)
        else:
            qdec = jnp.exp(lg * (idx + 1.0))
            kdec = jnp.exp(lg * (chunk - 1.0 - idx))
        cdec = jnp.exp(jnp.full((1, 1), lg * chunk, F32))
        q = rope(q_ref[:, sl].astype(F32))
        k = rope(k_ref[:, sl].astype(F32)) * scale
        v = v_ref[:, sl]
        qb = q.astype(BF16)
        scores = _dot_nt(qb, k.astype(BF16)) * dmask
        inner = _dot(scores.astype(BF16), v)
        s_old = state_ref[h]
        cross = _dot(qb, s_old.astype(BF16)) * qdec
        state_ref[h] = s_old * cdec + _dot_tn((k * kdec).astype(BF16), v)
        o = inner + cross
        if reverse:
            o = o + of_ref[:, sl]
            mu = jnp.mean(o, axis=-1, keepdims=True)
            oc = o - mu
            var = jnp.mean(oc * oc, axis=-1, keepdims=True)
            on = oc * lax.rsqrt(var + EPS) * nw_ref[:, sl]
            o_ref[:, sl] = (_silu(g_ref[:, sl].astype(F32)) * on).astype(BF16)
        else:
            o_ref[:, sl] = o


def _retention(proj3, log_gamma, cos2, sin2, norm_w, ctx_len):
    b, s, _ = proj3.shape
    c = RET_CHUNK
    nch = s // c
    nctx = ctx_len // c

    def fwd_chunk(t):
        return t

    def bwd_chunk(t):
        return jnp.where(t < nctx, nctx - 1 - t, nch - 1 + nctx - t)

    def specs(chunk_of):
        col = lambda j: pl.BlockSpec((None, c, RET_W), lambda bi, t, j=j: (bi, chunk_of(t), j))
        tab = pl.BlockSpec((c, RET_DIM), lambda bi, t: (chunk_of(t), 0))
        return col, tab

    smem = pl.BlockSpec(memory_space=pltpu.SMEM)
    scratch = [pltpu.VMEM((RET_HEADS, RET_DIM, RET_DIM), F32)]

    col, tab = specs(fwd_chunk)
    o_f = pl.pallas_call(
        functools.partial(_ret_kernel, reverse=False, chunk=c),
        out_shape=jax.ShapeDtypeStruct((b, s, RET_W), F32),
        grid=(b, nch),
        in_specs=[smem, col(0), col(1), col(2), tab, tab],
        out_specs=col(0),
        scratch_shapes=scratch,
        compiler_params=_cparams(2),
        name="retention_fwd",
    )(log_gamma, proj3, proj3, proj3, cos2, sin2)

    col, tab = specs(bwd_chunk)
    return pl.pallas_call(
        functools.partial(_ret_kernel, reverse=True, chunk=c),
        out_shape=jax.ShapeDtypeStruct((b, s, RET_W), BF16),
        grid=(b, nch),
        in_specs=[smem, col(0), col(1), col(2), tab, tab, col(0), col(3),
                  pl.BlockSpec((1, RET_W), lambda bi, t: (0, 0))],
        out_specs=col(0),
        scratch_shapes=scratch,
        compiler_params=_cparams(2),
        name="retention_bwd",
    )(log_gamma, proj3, proj3, proj3, cos2, sin2, o_f, proj3, norm_w)


def _dprep_kernel(q_ref, k_ref, v_ref, qw_ref, kw_ref, c_ref, s1_ref, s2_ref, bd_ref,
                  q0t_ref, q1t_ref, kk_ref, vt_ref):
    cc = c_ref[...]
    s1 = s1_ref[...]
    s2 = s2_ref[...]
    bd = bd_ref[...]
    rows = q_ref.shape[0]
    first = lax.broadcasted_iota(I32, (LANES, rows), 0) < DIFF_QK
    quarter = DIFF_QK // 2

    def prep(x, w):
        x = x.astype(F32)
        hi, lo = _split(x * x)
        ss = _dot(hi, bd) + _dot(lo, bd)
        xn = x * lax.rsqrt(ss * (1.0 / DIFF_QK) + EPS) * w
        return xn * cc + pltpu.roll(xn, LANES - quarter, 1) * s1 + pltpu.roll(xn, quarter, 1) * s2

    for h in range(DIFF_HEADS):
        sl = slice(h * LANES, (h + 1) * LANES)
        qt = (prep(q_ref[:, sl], qw_ref[...]) * (DIFF_QK ** -0.5 * LOG2E)).T
        q0t_ref[sl, :] = jnp.where(first, qt, 0.0).astype(BF16)
        q1t_ref[sl, :] = jnp.where(first, 0.0, qt).astype(BF16)
        kk_ref[:, sl] = prep(k_ref[:, sl], kw_ref[...]).astype(BF16)
        vt_ref[sl, :] = v_ref[:, sl].astype(F32).T.astype(BF16)


def _diff_prep(proj, qnorm_w, knorm_w, dc, ds1, ds2, batch, seq):
    t, _ = proj.shape
    tiles_per_seq = seq // TOK_TILE
    qcol = (4 * RET_W) // DIFF_QK_W
    blockdiag = jnp.kron(jnp.eye(2, dtype=F32), jnp.ones((DIFF_QK, DIFF_QK), F32)).astype(BF16)
    tab = pl.BlockSpec((TOK_TILE, LANES), lambda i: (i % tiles_per_seq, 0))
    row = pl.BlockSpec((1, LANES), lambda i: (0, 0))
    col = lambda j: pl.BlockSpec((TOK_TILE, DIFF_QK_W), lambda i, j=j: (i, j))
    tr = pl.BlockSpec((None, DIFF_QK_W, TOK_TILE),
                      lambda i: (i // tiles_per_seq, 0, i % tiles_per_seq))
    shp_t = jax.ShapeDtypeStruct((batch, DIFF_QK_W, seq), BF16)
    return pl.pallas_call(
        _dprep_kernel,
        out_shape=(shp_t, shp_t, jax.ShapeDtypeStruct((t, DIFF_QK_W), BF16), shp_t),
        grid=(t // TOK_TILE,),
        in_specs=[col(qcol), col(qcol + 1), col(qcol + 2), row, row, tab, tab, tab,
                  pl.BlockSpec((LANES, LANES), lambda i: (0, 0))],
        out_specs=(tr, tr, col(0), tr),
        compiler_params=_cparams(1),
        name="diff_prep",
    )(proj, proj, proj, jnp.tile(qnorm_w, 2)[None], jnp.tile(knorm_w, 2)[None], dc, ds1, ds2,
      blockdiag)


def _flash_kernel(lam_ref, q0t_ref, q1t_ref, k_ref, vt_ref, sw_ref, o_ref, m_s, acc_s, s_s, *,
                  heads, nctx_tiles, ctx_kv, all_kv, lambda_init):
    i = pl.program_id(2)
    nkv = jnp.where(i < nctx_tiles, ctx_kv, all_kv)
    m_s[...] = jnp.full(m_s.shape, -jnp.inf, F32)
    acc_s[...] = jnp.zeros(acc_s.shape, F32)
    ones = jnp.ones((BF16_SUBLANES, ATT_TK), BF16)

    def scores(j, buf):
        off = pl.multiple_of(j * ATT_TK, ATT_TK)
        for h in range(heads):
            sl = slice(h * DIFF_V, (h + 1) * DIFF_V)
            kt = k_ref[pl.ds(off, ATT_TK), sl]
            for comp, qt_ref in enumerate((q0t_ref, q1t_ref)):
                s_s[buf, 2 * h + comp] = _dot(kt, qt_ref[sl, :])

    def softmax_pv(j, buf):
        off = pl.multiple_of(j * ATT_TK, ATT_TK)
        for h in range(heads):
            sl = slice(h * DIFF_V, (h + 1) * DIFF_V)
            vt = jnp.concatenate([vt_ref[sl, pl.ds(off, ATT_TK)], ones], axis=0)
            for comp in range(2):
                c = 2 * h + comp
                st = s_s[buf, c]
                m_old = m_s[c]
                m_new = jnp.maximum(m_old, jnp.max(st, axis=0, keepdims=True))
                alpha = jnp.exp2(m_old - m_new)
                pt = jnp.exp2(st - m_new)
                acc_s[c] = alpha * acc_s[c] + _dot(vt, pt.astype(BF16))
                m_s[c] = m_new

    scores(0, 0)

    def pairs(first, n_pairs):
        for p in range(n_pairs):
            scores(first + 2 * p + 1, 1)
            softmax_pv(first + 2 * p, 0)
            scores(first + 2 * p + 2, 0)
            softmax_pv(first + 2 * p + 1, 1)

    group = 2 * ATT_UNROLL_PAIRS
    n_groups = (nkv - 1) // group

    def group_body(g, carry):
        pairs(group * g, ATT_UNROLL_PAIRS)
        return carry

    def pair_body(p, carry):
        pairs(group * n_groups + 2 * p, 1)
        return carry

    lax.fori_loop(0, n_groups, group_body, 0)
    lax.fori_loop(0, ((nkv - 1) % group) // 2, pair_body, 0)
    softmax_pv(nkv - 1, 0)
    lv = lam_ref[...]
    lam = (jnp.exp(jnp.sum(lv[0:1] * lv[1:2], axis=-1, keepdims=True))
           - jnp.exp(jnp.sum(lv[2:3] * lv[3:4], axis=-1, keepdims=True)) + lambda_init)
    for h in range(heads):
        sl = slice(h * DIFF_V, (h + 1) * DIFF_V)
        o0 = acc_s[2 * h, 0:DIFF_V] / acc_s[2 * h, DIFF_V:DIFF_V + 1]
        o1 = acc_s[2 * h + 1, 0:DIFF_V] / acc_s[2 * h + 1, DIFF_V:DIFF_V + 1]
        d = (o0 - lam * o1).T
        ms = jnp.mean(d * d, axis=-1, keepdims=True)
        o_ref[:, sl] = (d * lax.rsqrt(ms + EPS) * sw_ref[...] * (1.0 - lambda_init)).astype(BF16)


def _diff_attention(q0t, q1t, kk, vt, lam_vecs, subln_w, ctx_len, lambda_init):
    b, _, s = q0t.shape
    kk = kk.reshape(b, s, DIFF_QK_W)
    hb = ATT_HEADS
    w = hb * DIFF_V
    assert (s // ATT_TK) % 2 == 1 and (ctx_len // ATT_TK) % 2 == 1
    qspec = pl.BlockSpec((None, w, ATT_TQ), lambda bi, h, i: (bi, h, i))
    return pl.pallas_call(
        functools.partial(_flash_kernel, heads=hb, nctx_tiles=ctx_len // ATT_TQ,
                          ctx_kv=ctx_len // ATT_TK, all_kv=s // ATT_TK, lambda_init=lambda_init),
        out_shape=jax.ShapeDtypeStruct((b, s, DIFF_W), BF16),
        grid=(b, DIFF_HEADS // hb, s // ATT_TQ),
        in_specs=[
            pl.BlockSpec((4, DIFF_QK), lambda bi, h, i: (0, 0)),
            qspec, qspec,
            pl.BlockSpec((None, s, w), lambda bi, h, i: (bi, 0, h)),
            pl.BlockSpec((None, w, s), lambda bi, h, i: (bi, h, 0)),
            pl.BlockSpec((1, DIFF_V), lambda bi, h, i: (0, 0)),
        ],
        out_specs=pl.BlockSpec((None, ATT_TQ, w), lambda bi, h, i: (bi, i, h)),
        scratch_shapes=[pltpu.VMEM((2 * hb, 1, ATT_TQ), F32),
                        pltpu.VMEM((2 * hb, DIFF_V + BF16_SUBLANES, ATT_TQ), F32),
                        pltpu.VMEM((2, 2 * hb, ATT_TK, ATT_TQ), F32)],
        compiler_params=_cparams(3),
        name="diff_flash",
    )(lam_vecs, q0t, q1t, kk, vt, subln_w[None])


def _s5_kernel(u_ref, br_ref, bi_ref, cr_ref, ci_ref, tab_ref, *rest, reverse, ntiles):
    if reverse:
        yf_ref, d_ref, wg_ref, o_ref, xr_s, xi_s, car_s, cai_s = rest
    else:
        o_ref, xr_s, xi_s, car_s, cai_s = rest
    t = pl.program_id(1)

    @pl.when(t == 0)
    def _():
        car_s[...] = jnp.zeros_like(car_s)
        cai_s[...] = jnp.zeros_like(cai_s)

    u = u_ref[...]
    nq = S5_CH // LANES
    sw = S5_W // nq
    for q in range(nq):
        uq = u[:, q * LANES:(q + 1) * LANES]
        xr_s[:, q * sw:(q + 1) * sw] = _dot(uq, br_ref[q])
        xi_s[:, q * sw:(q + 1) * sw] = _dot(uq, bi_ref[q])

    def body(n, carry):
        cr, ci = carry
        tile = (ntiles - 1 - n) if reverse else n
        off = pl.multiple_of(tile * SUBLANES, SUBLANES)
        xr = xr_s[pl.ds(off, SUBLANES), :]
        xi = xi_s[pl.ds(off, SUBLANES), :]
        for lvl, k in enumerate((1, 2, 4)):
            pr = tab_ref[2 * lvl]
            pi = tab_ref[2 * lvl + 1]
            shift = (SUBLANES - k) if reverse else k
            sr = pltpu.roll(xr, shift, 0)
            si = pltpu.roll(xi, shift, 0)
            xr, xi = xr + pr * sr - pi * si, xi + pr * si + pi * sr
        ar = tab_ref[6]
        ai = tab_ref[7]
        xr, xi = xr + ar * cr - ai * ci, xi + ar * ci + ai * cr
        xr_s[pl.ds(off, SUBLANES), :] = xr
        xi_s[pl.ds(off, SUBLANES), :] = xi
        last = 0 if reverse else SUBLANES - 1
        return xr[last:last + 1], xi[last:last + 1]

    cr, ci = lax.fori_loop(0, ntiles, body, (car_s[...], cai_s[...]))
    car_s[...] = cr
    cai_s[...] = ci
    y = jnp.concatenate(
        [_dot(xr_s[:, q * sw:(q + 1) * sw].astype(BF16), cr_ref[q])
         + _dot(xi_s[:, q * sw:(q + 1) * sw].astype(BF16), ci_ref[q]) for q in range(nq)], axis=1)
    if reverse:
        y = y + yf_ref[...] + d_ref[...] * u.astype(F32)
        y = jax.nn.gelu(y, approximate=True)
        o_ref[...] = (y * jax.nn.sigmoid(_dot(y.astype(BF16), wg_ref[...]))).astype(BF16)
    else:
        o_ref[...] = y


def _s5_tables(a_re_log, a_im_ang, reverse):
    w = a_re_log.reshape(1, S5_W)
    th = a_im_ang.reshape(1, S5_W)
    row = jnp.arange(SUBLANES, dtype=F32)[:, None]
    tabs = []
    for k in (1, 2, 4):
        keep = (row <= SUBLANES - 1 - k) if reverse else (row >= k)
        mag = jnp.exp(w * k)
        tabs.append(jnp.where(keep, mag * jnp.cos(th * k), 0.0))
        tabs.append(jnp.where(keep, mag * jnp.sin(th * k), 0.0))
    e = (SUBLANES - row) if reverse else (row + 1.0)
    mag = jnp.exp(w * e)
    tabs.append(mag * jnp.cos(th * e))
    tabs.append(mag * jnp.sin(th * e))
    return jnp.stack(tabs)


def _s5_params(lam_re, lam_im, log_dt, b_re, b_im):
    dt = jnp.exp(log_dt)[:, None]
    wlog = lam_re * dt
    ang = lam_im * dt
    mag = jnp.exp(wlog)
    a_re = mag * jnp.cos(ang)
    a_im = mag * jnp.sin(ang)
    den = lam_re * lam_re + lam_im * lam_im
    nr = a_re - 1.0
    f_re = (nr * lam_re + a_im * lam_im) / den
    f_im = (a_im * lam_re - nr * lam_im) / den
    bb_re = f_re[..., None] * b_re - f_im[..., None] * b_im
    bb_im = f_re[..., None] * b_im + f_im[..., None] * b_re
    eye = jnp.eye(S5_GROUPS, dtype=F32)
    bmat_re = jnp.einsum('gph,gk->ghkp', bb_re, eye).reshape(S5_CH, S5_W)
    bmat_im = jnp.einsum('gph,gk->ghkp', bb_im, eye).reshape(S5_CH, S5_W)
    return wlog, ang, _diag_blocks(bmat_re).astype(BF16), _diag_blocks(bmat_im).astype(BF16)


def _diag_blocks(m):
    nq = S5_CH // LANES
    r, c = m.shape[0] // nq, m.shape[1] // nq
    return jnp.stack([m[q * r:(q + 1) * r, q * c:(q + 1) * c] for q in range(nq)])


def _s5_mixer(proj3, lam_re, lam_im, log_dt, b_re, b_im, c_re, c_im, d, w_glu, ctx_len):
    b, s, _ = proj3.shape
    tb = S5_BLOCK
    nb = s // tb
    nctx = ctx_len // tb
    ucol = (4 * RET_W + 2 * DIFF_QK_W + DIFF_W) // S5_CH
    eye = jnp.eye(S5_GROUPS, dtype=F32)
    cmat_re = _diag_blocks(jnp.einsum('ghp,gk->gpkh', c_re, eye).reshape(S5_W, S5_CH)).astype(BF16)
    cmat_im = _diag_blocks(-jnp.einsum('ghp,gk->gpkh', c_im, eye).reshape(S5_W, S5_CH)).astype(BF16)
    full = lambda shape: pl.BlockSpec(shape, lambda bi, t: tuple(0 for _ in shape))
    scratch = [pltpu.VMEM((tb, S5_W), F32), pltpu.VMEM((tb, S5_W), F32),
               pltpu.VMEM((1, S5_W), F32), pltpu.VMEM((1, S5_W), F32)]
    nq = S5_CH // LANES
    weights = [full((nq, LANES, S5_W // nq)), full((nq, LANES, S5_W // nq)),
               full((nq, S5_W // nq, LANES)), full((nq, S5_W // nq, LANES)),
               full((8, SUBLANES, S5_W))]

    def blk_f(t):
        return t

    def blk_b(t):
        return jnp.where(t < nctx, nctx - 1 - t, nb - 1 + nctx - t)

    wl, ang, bre, bim = _s5_params(lam_re[0], lam_im[0], log_dt[0], b_re, b_im)
    y_f = pl.pallas_call(
        functools.partial(_s5_kernel, reverse=False, ntiles=tb // SUBLANES),
        out_shape=jax.ShapeDtypeStruct((b, s, S5_CH), F32),
        grid=(b, nb),
        in_specs=[pl.BlockSpec((None, tb, S5_CH), lambda bi, t: (bi, blk_f(t), ucol))] + weights,
        out_specs=pl.BlockSpec((None, tb, S5_CH), lambda bi, t: (bi, blk_f(t), 0)),
        scratch_shapes=scratch,
        compiler_params=_cparams(2),
        name="s5_fwd",
    )(proj3, bre, bim, cmat_re, cmat_im, _s5_tables(wl, ang, False))

    wl, ang, bre, bim = _s5_params(lam_re[1], lam_im[1], log_dt[1], b_re, b_im)
    return pl.pallas_call(
        functools.partial(_s5_kernel, reverse=True, ntiles=tb // SUBLANES),
        out_shape=jax.ShapeDtypeStruct((b, s, S5_CH), BF16),
        grid=(b, nb),
        in_specs=[pl.BlockSpec((None, tb, S5_CH), lambda bi, t: (bi, blk_b(t), ucol))] + weights + [
            pl.BlockSpec((None, tb, S5_CH), lambda bi, t: (bi, blk_b(t), 0)),
            full((1, S5_CH)), full((S5_CH, S5_CH))],
        out_specs=pl.BlockSpec((None, tb, S5_CH), lambda bi, t: (bi, blk_b(t), 0)),
        scratch_shapes=scratch,
        compiler_params=_cparams(2),
        name="s5_bwd",
    )(proj3, bre, bim, cmat_re, cmat_im, _s5_tables(wl, ang, True), y_f, d[None],
      w_glu.astype(BF16))


def _neg_inf_like(x):
    return jnp.full(x.shape, -jnp.inf, x.dtype)


def _out_kernel(x_ref, ret_ref, dif_ref, s5_ref, wo_ref, g1_ref, nw_ref, sc_ref, sh_ref,
                rw_ref, rb_ref, tri_ref,
                x1_ref, h2_ref, h2p_ref, te_ref, tw_ref, rk_ref, cnt_ref, cnt_s):
    i = pl.program_id(0)

    @pl.when(i == 0)
    def _():
        cnt_s[...] = jnp.zeros_like(cnt_s)

    d = x_ref.shape[1]
    tm = x_ref.shape[0]
    mix = (_dot(ret_ref[...], wo_ref[0:RET_W, :])
           + _dot(dif_ref[...], wo_ref[RET_W:RET_W + DIFF_W, :])
           + _dot(s5_ref[...], wo_ref[RET_W + DIFF_W:, :]))
    x1 = x_ref[...] + g1_ref[...] * mix
    x1_ref[...] = x1
    ms = jnp.mean(x1 * x1, axis=-1, keepdims=True)
    h2 = x1 * lax.rsqrt(ms + EPS) * nw_ref[...]
    h2 = h2 * (1.0 + sc_ref[...]) + sh_ref[...]
    h2_ref[...] = h2.astype(BF16)
    h2p_ref[...] = _pack_halves(h2[:, :d // 2], h2[:, d // 2:])

    ng = N_EXPERT_GROUPS
    pg = N_EXPERTS // N_EXPERT_GROUPS
    logits = _dot3_nt(rw_ref[...], h2)
    scores = jax.nn.sigmoid(logits)
    sel3 = (scores + rb_ref[...]).reshape(ng, pg, tm)
    scores3 = scores.reshape(ng, pg, tm)
    in_grp = lax.broadcasted_iota(I32, (ng, pg, tm), 1).astype(F32)
    grp = lax.broadcasted_iota(I32, (ng, pg, tm), 0).astype(F32)
    eidx = grp * pg + in_grp
    gidx = lax.broadcasted_iota(I32, (ng, 1, tm), 0).astype(F32)

    m1 = jnp.max(sel3, axis=1, keepdims=True)
    first = jnp.min(jnp.where(sel3 == m1, in_grp, float(pg)), axis=1, keepdims=True)
    m2 = jnp.max(jnp.where(in_grp == first, -jnp.inf, sel3), axis=1, keepdims=True)
    rem = m1 + m2
    gsel = jnp.zeros((ng, 1, tm), F32)
    for _ in range(TOPK_GROUPS):
        mx = jnp.max(rem, axis=0, keepdims=True)
        fi = jnp.min(jnp.where(rem == mx, gidx, float(ng)), axis=0, keepdims=True)
        pick = gidx == fi
        gsel = jnp.where(pick, 1.0, gsel)
        rem = jnp.where(pick, -jnp.inf, rem)
    masked = jnp.where(gsel > 0.0, sel3, -jnp.inf)

    def red2(fn, x):
        return fn(fn(x, axis=1, keepdims=True), axis=0, keepdims=True)

    chosen = jnp.zeros((ng, pg, tm), F32)
    picks = []
    weights = []
    for _ in range(TOP_K):
        mx = red2(jnp.max, masked)
        fi = red2(jnp.min, jnp.where(masked == mx, eidx, float(N_EXPERTS)))
        pick = eidx == fi
        picks.append(fi)
        weights.append(red2(jnp.sum, jnp.where(pick, scores3, 0.0)))
        chosen = jnp.where(pick, 1.0, chosen)
        masked = jnp.where(pick, -jnp.inf, masked)
    wsum = weights[0]
    for w in weights[1:]:
        wsum = wsum + w
    inv = ROUTE_SCALE / wsum

    chosen2 = chosen.reshape(N_EXPERTS, tm)
    cum = _dot(chosen2.astype(BF16), tri_ref[...]) + cnt_s[...]
    cum3 = cum.reshape(ng, pg, tm)
    for k in range(TOP_K):
        te_ref[k:k + 1, :] = picks[k].reshape(1, tm).astype(I32)
        tw_ref[k:k + 1, :] = (weights[k] * inv).reshape(1, tm)
        rk = red2(jnp.sum, jnp.where(eidx == picks[k], cum3, 0.0))
        rk_ref[k:k + 1, :] = rk.reshape(1, tm).astype(I32)
    cnt_new = cnt_s[...] + jnp.sum(chosen2, axis=-1, keepdims=True)
    cnt_s[...] = cnt_new
    cnt_ref[...] = jnp.broadcast_to(cnt_new, cnt_ref.shape)


def _out_proj_route(z, ret, dif, s5, w_out, g1, norm_w, sc, sh, router_w, router_bias, mod_row,
                    src_tile, n_out_tiles):
    d = z.shape[1]
    tm = TOK_TILE
    t = n_out_tiles * tm
    tri = (jnp.arange(tm)[:, None] < jnp.arange(tm)[None, :]).astype(BF16)
    mod_spec = pl.BlockSpec((None, 1, d), lambda i: (mod_row(src_tile(i)), 0, 0))
    src = lambda w: pl.BlockSpec((tm, w), lambda i: (src_tile(i), 0))
    tok = lambda w: pl.BlockSpec((tm, w), lambda i: (i, 0))
    const = lambda shape: pl.BlockSpec(shape, lambda i: tuple(0 for _ in shape))
    lane_out = pl.BlockSpec((TOP_K, tm), lambda i: (0, i))
    return pl.pallas_call(
        _out_kernel,
        out_shape=(
            jax.ShapeDtypeStruct((t, d), F32),
            jax.ShapeDtypeStruct((t, d), BF16),
            jax.ShapeDtypeStruct((t, d // 2), U32),
            jax.ShapeDtypeStruct((TOP_K, t), I32),
            jax.ShapeDtypeStruct((TOP_K, t), F32),
            jax.ShapeDtypeStruct((TOP_K, t), I32),
            jax.ShapeDtypeStruct((N_EXPERTS, LANES), F32),
        ),
        grid=(t // tm,),
        in_specs=[src(d), src(RET_W), src(DIFF_W), src(S5_CH), const((d, d)),
                  mod_spec, const((1, d)), mod_spec, mod_spec,
                  const((N_EXPERTS, d)), const((N_EXPERTS, 1)), const((tm, tm))],
        out_specs=(tok(d), tok(d), tok(d // 2), lane_out, lane_out, lane_out,
                   const((N_EXPERTS, LANES))),
        scratch_shapes=[pltpu.VMEM((N_EXPERTS, 1), F32)],
        compiler_params=_cparams(1),
        name="out_proj_route",
    )(z, ret, dif, s5, w_out, g1, norm_w, sc, sh, router_w.T, router_bias[:, None], tri)


def _expert_kernel(te_ref, s0_ref, nv_ref, nt_ref, h2p_hbm, tok_hbm, dst_hbm, wg_ref, wu_ref, wd_ref,
                   ybuf_hbm, wg_s, wu_s, wd_s, xs_s, y_s, tokw, dstw, sem_g, sem_s, sem_w):
    i = pl.program_id(0)
    tm = xs_s.shape[1]
    half = xs_s.shape[2]
    n_valid = nt_ref[0]
    trash = ybuf_hbm.shape[0] - EXP_BUFS * tm

    def win_base(step):
        return (s0_ref[step + 1] // WIN_ALIGN) * WIN_ALIGN

    def window_copies(step, slot):
        base = pl.multiple_of(win_base(step), WIN_ALIGN)
        dst = pl.ds(pl.multiple_of(slot * WIN, WIN), WIN)
        return (pltpu.make_async_copy(tok_hbm.at[pl.ds(base, WIN)], tokw.at[dst], sem_w.at[slot]),
                pltpu.make_async_copy(dst_hbm.at[pl.ds(base, WIN)], dstw.at[dst], sem_w.at[slot]))

    def issue_gathers(tile, step, slot):
        buf = tile % EXP_BUFS
        first = slot * WIN + s0_ref[tile + 2] - win_base(step)
        for r in range(tm):
            pltpu.make_async_copy(h2p_hbm.at[pl.ds(tokw[first + r], 1), :],
                                  xs_s.at[buf, pl.ds(r, 1), :], sem_g.at[buf]).start(priority=r % 2)

    def issue_scatters(tile, step, slot):
        buf = (tile + EXP_BUFS) % EXP_BUFS
        first = slot * WIN + s0_ref[tile + 2] - win_base(step)
        spare = trash + buf * tm
        nv = nv_ref[tile + 2]
        for r in range(tm):
            dest = jnp.where(r < nv, dstw[first + r], spare + r)
            pltpu.make_async_copy(y_s.at[buf, pl.ds(r, 1), :],
                                  ybuf_hbm.at[pl.ds(dest, 1), :], sem_s.at[buf]).start(priority=r % 2)

    def wait_gathers(buf):
        pltpu.make_async_copy(xs_s.at[buf], xs_s.at[buf], sem_g.at[buf]).wait()

    def wait_scatters(buf):
        pltpu.make_async_copy(y_s.at[buf], y_s.at[buf], sem_s.at[buf]).wait()

    @pl.when(i < n_valid)
    def _step():
        cur = i % EXP_BUFS
        slot = i % 2

        @pl.when(i == 0)
        def _prologue():
            y_s[...] = jnp.zeros_like(y_s)
            for b in range(EXP_BUFS):
                cp = pltpu.make_async_copy(y_s.at[b], ybuf_hbm.at[pl.ds(trash + b * tm, tm), :],
                                           sem_s.at[b])
                cp.start()
                cp.wait()
            for cp in window_copies(0, 0):
                cp.start()
            for cp in window_copies(0, 0):
                cp.wait()
            issue_gathers(0, 0, 0)
            issue_gathers(1, 0, 0)
            issue_scatters(-2, 0, 0)

        @pl.when(i > 0)
        def _():
            for cp in window_copies(i, slot):
                cp.wait()

        for cp in window_copies(i + 1, 1 - slot):
            cp.start()
        wait_gathers(cur)

        @pl.when(i > 0)
        def _():
            wait_scatters(cur)

        @pl.when((i == 0) | (te_ref[i] != te_ref[jnp.maximum(i - 1, 0)]))
        def _():
            wg_s[...] = wg_ref[...].astype(BF16)
            wu_s[...] = wu_ref[...].astype(BF16)
            wd_s[...] = wd_ref[...].astype(BF16)

        issue_gathers(i + 2, i, slot)
        issue_scatters(i - 1, i, slot)
        lo, hi = _unpack_halves(xs_s[cur])
        lo = lo.astype(BF16)
        hi = hi.astype(BF16)
        g = _dot(lo, wg_s[0:half, :]) + _dot(hi, wg_s[half:, :])
        u = _dot(lo, wu_s[0:half, :]) + _dot(hi, wu_s[half:, :])
        y = _dot((_silu(g) * u).astype(BF16), wd_s[...])
        y_s[cur] = _pack_halves(y[:, :half], y[:, half:])

        @pl.when(i == n_valid - 1)
        def _epilogue():
            issue_scatters(i, i, slot)
            for b in range(EXP_BUFS):
                wait_scatters(b)
            wait_gathers((i + 1) % EXP_BUFS)
            wait_gathers((i + 2) % EXP_BUFS)
            for cp in window_copies(i + 1, 1 - slot):
                cp.wait()


def _experts(h2p, tok_sorted, dst_sorted, tile_expert, tile_s0, tile_nv, n_valid, w_gate, w_up,
             w_down, layer, n_rows_out):
    half = h2p.shape[1]
    _, e, d, f = w_gate.shape
    tm = EXP_TILE
    n_tiles = tile_expert.shape[0]
    hbm = pl.BlockSpec(memory_space=pl.ANY)
    wspec = lambda a, b: pl.BlockSpec((None, None, a, b),
                                      lambda i, te, s0, nv, nt: (layer, te[i], 0, 0))
    grid_spec = pltpu.PrefetchScalarGridSpec(
        num_scalar_prefetch=4,
        grid=(n_tiles,),
        in_specs=[hbm, hbm, hbm, wspec(d, f), wspec(d, f), wspec(f, d)],
        out_specs=hbm,
        scratch_shapes=[pltpu.VMEM((d, f), BF16), pltpu.VMEM((d, f), BF16), pltpu.VMEM((f, d), BF16),
                        pltpu.VMEM((EXP_BUFS, tm, half), U32), pltpu.VMEM((EXP_BUFS, tm, half), U32),
                        pltpu.SMEM((2 * WIN,), I32), pltpu.SMEM((2 * WIN,), I32),
                        pltpu.SemaphoreType.DMA((EXP_BUFS,)), pltpu.SemaphoreType.DMA((EXP_BUFS,)),
                        pltpu.SemaphoreType.DMA((2,))],
    )
    return pl.pallas_call(
        _expert_kernel,
        out_shape=jax.ShapeDtypeStruct((n_rows_out + EXP_BUFS * tm, half), U32),
        grid_spec=grid_spec,
        compiler_params=_cparams(1, 56),
        name="moe_experts",
    )(tile_expert, tile_s0, tile_nv, n_valid, h2p, tok_sorted, dst_sorted, w_gate, w_up, w_down)


def _combine_kernel(y_ref, x1_ref, h2_ref, w_ref, g2_ref, sg_ref, su_ref, sd_ref, o_ref):
    tm = x1_ref.shape[0]
    half = y_ref.shape[1]
    h2 = h2_ref[...]
    hid = _silu(_dot(h2, sg_ref[...])) * _dot(h2, su_ref[...])
    shared = _dot(hid.astype(BF16), sd_ref[...])

    w = w_ref[...]
    acc_lo = None
    acc_hi = None
    for k in range(TOP_K):
        lo, hi = _unpack_halves(y_ref[k * tm:(k + 1) * tm, :])
        wk = w[:, k:k + 1]
        acc_lo = wk * lo if acc_lo is None else acc_lo + wk * lo
        acc_hi = wk * hi if acc_hi is None else acc_hi + wk * hi
    g2 = g2_ref[...]
    o_ref[:, :half] = x1_ref[:, :half] + g2[:, :half] * (acc_lo + shared[:, :half])
    o_ref[:, half:] = x1_ref[:, half:] + g2[:, half:] * (acc_hi + shared[:, half:])


def _combine(ybuf, x1, h2, tw, g2, s_gate, s_up, s_down, mod_row):
    t, d = x1.shape
    tm = TOK_TILE
    f = s_gate.shape[1]
    tok = lambda w: pl.BlockSpec((tm, w), lambda i: (i, 0))
    const = lambda shape: pl.BlockSpec(shape, lambda i: tuple(0 for _ in shape))
    return pl.pallas_call(
        _combine_kernel,
        out_shape=jax.ShapeDtypeStruct((t, d), F32),
        grid=(t // tm,),
        in_specs=[pl.BlockSpec((TOP_K * tm, d // 2), lambda i: (i, 0)),
                  tok(d), tok(d), tok(TOP_K),
                  pl.BlockSpec((None, 1, d), lambda i: (mod_row(i), 0, 0)),
                  const((d, f)), const((d, f)), const((f, d))],
        out_specs=tok(d),
        compiler_params=_cparams(1),
        name="moe_combine",
    )(ybuf, x1, h2, tw, g2, s_gate, s_up, s_down)


def _moe(x1, h2, h2p, te, tw, rk, cnt, g2, w_gate, w_up, w_down, layer, s_gate, s_up, s_down,
         mod_row):
    t = x1.shape[0]
    n = t * TOP_K
    tm = EXP_TILE
    n_tiles = n // tm + N_EXPERTS
    counts = cnt[:, 0].astype(I32)
    padded = ((counts + tm - 1) // tm) * tm
    pad_end = jnp.cumsum(padded)
    pad_off = pad_end - padded
    off = jnp.cumsum(counts) - counts
    experts = jnp.arange(N_EXPERTS, dtype=I32)
    off_of = jnp.sum(jnp.where(te[:, :, None] == experts, off, 0), axis=-1)
    key = (off_of + rk).T.reshape(-1)
    order = jnp.argsort(key).astype(I32)
    tok_sorted = order // TOP_K
    k_sorted = order % TOP_K
    dst_sorted = ((tok_sorted // TOK_TILE) * (TOP_K * TOK_TILE) + k_sorted * TOK_TILE
                  + tok_sorted % TOK_TILE)
    tok_sorted = jnp.pad(tok_sorted, (0, WIN))
    dst_sorted = jnp.pad(dst_sorted, (0, WIN))

    tile_start = jnp.arange(n_tiles, dtype=I32) * tm
    tile_expert = jnp.minimum(jnp.sum((tile_start[:, None] >= pad_end[None, :]).astype(I32), axis=-1),
                              N_EXPERTS - 1)
    is_e = tile_expert[:, None] == experts
    row0 = tile_start - jnp.sum(jnp.where(is_e, pad_off, 0), axis=-1)
    valid = tile_start < pad_end[-1]
    s0 = jnp.where(valid, jnp.sum(jnp.where(is_e, off, 0), axis=-1) + row0, n)
    nv = jnp.where(valid, jnp.clip(jnp.sum(jnp.where(is_e, counts, 0), axis=-1) - row0, 0, tm), 0)
    tile_s0 = jnp.concatenate([jnp.zeros((2,), I32), s0, jnp.full((3,), n, I32)]).astype(I32)
    tile_nv = jnp.concatenate([jnp.zeros((2,), I32), nv, jnp.zeros((3,), I32)]).astype(I32)
    n_valid = (pad_end[-1:] // tm).astype(I32)
    ybuf = _experts(h2p, tok_sorted, dst_sorted, tile_expert, tile_s0, tile_nv, n_valid,
                    w_gate, w_up, w_down, layer, n)
    return _combine(ybuf, x1, h2, tw.T, g2, s_gate, s_up, s_down, mod_row)


def _rope_angles(rows, head_dim):
    axis_dim = head_dim // 2
    inv_freq = ROPE_BASE ** (-jnp.arange(0, axis_dim, 2, dtype=F32) / axis_dim)
    row = jnp.repeat(jnp.arange(rows, dtype=F32), GRID_W)
    col = jnp.tile(jnp.arange(GRID_W, dtype=F32), rows)
    return jnp.concatenate([row[:, None] * inv_freq, col[:, None] * inv_freq], axis=-1)


def _rope_tables(n, ctx_len):
    rows = n // GRID_W
    ang = _rope_angles(rows, RET_DIM)
    cos, sin = jnp.cos(ang), jnp.sin(ang)
    ret_cos = jnp.concatenate([cos, cos], axis=-1)
    ret_sin = jnp.concatenate([-sin, sin], axis=-1)
    ang = _rope_angles(rows, DIFF_QK)
    cos, sin = jnp.cos(ang), jnp.sin(ang)
    zero = jnp.zeros_like(sin)
    dc = jnp.tile(jnp.concatenate([cos, cos], axis=-1), (1, 2))
    ds1 = jnp.tile(jnp.concatenate([-sin, zero], axis=-1), (1, 2))
    ds2 = jnp.tile(jnp.concatenate([zero, sin], axis=-1), (1, 2))

    def with_ctx(tab, fill):
        return jnp.concatenate([jnp.full((ctx_len, tab.shape[1]), fill, F32), tab], axis=0)

    return (with_ctx(ret_cos, 1.0), with_ctx(ret_sin, 0.0),
            with_ctx(dc, 1.0), with_ctx(ds1, 0.0), with_ctx(ds2, 0.0))


def kernel(x, c, ctx, c_ctx, w_ada, b_ada, norm1_w, norm2_w, w_in, ret_decay_logit, ret_norm_w,
           diff_qnorm_w, diff_knorm_w, diff_lambda, diff_subln_w, s5_lambda_re, s5_lambda_im,
           s5_log_dt, s5_b_re, s5_b_im, s5_c_re, s5_c_im, s5_d, s5_w_glu, w_out, router_w,
           router_bias, exp_w_gate, exp_w_up, exp_w_down, shared_w_gate, shared_w_up, shared_w_down):
    out_dtype = x.dtype
    batch, n, d = x.shape
    ctx_len = ctx.shape[1]
    depth = w_ada.shape[0]
    s = ctx_len + n
    t = batch * s
    assert ctx_len % TOK_TILE == 0 and n % TOK_TILE == 0 and n % GRID_W == 0
    assert batch + 1 <= SUBLANES
    tiles_per_seq = s // TOK_TILE
    ctx_tiles = ctx_len // TOK_TILE

    def mod_row(i):
        return jnp.where(i % tiles_per_seq < ctx_tiles, batch, i // tiles_per_seq)

    z = jnp.concatenate([ctx.astype(F32), x.astype(F32)], axis=1).reshape(t, d)
    cond = jnp.zeros((SUBLANES, d), F32).at[:batch].set(c.astype(F32)).at[batch].set(c_ctx.astype(F32))
    mod = _adaln(cond, w_ada.astype(F32), b_ada.astype(F32))
    ret_cos, ret_sin, dc, ds1, ds2 = _rope_tables(n, ctx_len)

    for l in range(depth):
        lambda_init = 0.8 - 0.6 * math.exp(-0.3 * l)
        m = mod[l].reshape(SUBLANES, N_MOD, 1, d)
        sh1, sc1, g1, sh2, sc2, g2 = (m[:, j] for j in range(N_MOD))
        proj = _in_proj(z, norm1_w[l][None], sc1, sh1, w_in[l].astype(BF16), mod_row)
        proj3 = proj.reshape(batch, s, proj.shape[1])
        log_gamma = jax.nn.log_sigmoid(ret_decay_logit[l].astype(F32))
        ret = _retention(proj3, log_gamma, ret_cos, ret_sin, ret_norm_w[l][None], ctx_len)
        q0t, q1t, kk, vt = _diff_prep(proj, diff_qnorm_w[l], diff_knorm_w[l], dc, ds1, ds2,
                                      batch, s)
        dif = _diff_attention(q0t, q1t, kk, vt, diff_lambda[l], diff_subln_w[l], ctx_len,
                              lambda_init)
        s5 = _s5_mixer(proj3, s5_lambda_re[l], s5_lambda_im[l], s5_log_dt[l], s5_b_re[l],
                       s5_b_im[l], s5_c_re[l], s5_c_im[l], s5_d[l], s5_w_glu[l], ctx_len)
        if l < depth - 1:
            src_tile = lambda i: i
            n_out_tiles = t // TOK_TILE
        else:
            lat_tiles = tiles_per_seq - ctx_tiles
            src_tile = lambda i: i + (i // lat_tiles + 1) * ctx_tiles
            n_out_tiles = batch * lat_tiles
        x1, h2, h2p, te, tw, rk, cnt = _out_proj_route(
            z, ret.reshape(t, RET_W), dif.reshape(t, DIFF_W), s5.reshape(t, S5_CH),
            w_out[l].astype(BF16), g1, norm2_w[l][None], sc2, sh2, router_w[l], router_bias[l],
            mod_row, src_tile, n_out_tiles)
        z = _moe(x1, h2, h2p, te, tw, rk, cnt, g2, exp_w_gate.astype(F32),
                 exp_w_up.astype(F32), exp_w_down.astype(F32), l,
                 shared_w_gate[l].astype(BF16), shared_w_up[l].astype(BF16),
                 shared_w_down[l].astype(BF16), lambda i, f=src_tile: mod_row(f(i)))
    return z.reshape(batch, n, d).astype(out_dtype)
```
